```python
import math
import jax, jax.numpy as jnp
from jax import lax
import numpy as np

D_MODEL = 1024
BATCH = 2
SEQ = 16384
DEPTH = 4

GRID_W = 64
CTX_LEN = 256
N_MIXERS = 2
N_SSD_LAYERS = (DEPTH + 1) // 2
N_ATTN_LAYERS = DEPTH // 2
EPS = 1e-6

SSD_EXPAND = 2
SSD_D_INNER = SSD_EXPAND * D_MODEL
SSD_HEADDIM = 64
SSD_HEADS = SSD_D_INNER // SSD_HEADDIM
SSD_GROUPS = 4
SSD_HPG = SSD_HEADS // SSD_GROUPS
SSD_STATE = 128
SSD_CONV_W = 5
SSD_CHUNK = 128
SSD_CONV_DIM = SSD_D_INNER + 2 * SSD_GROUPS * SSD_STATE
SSD_IN_DIM = SSD_D_INNER + SSD_CONV_DIM + 2 * SSD_HEADS

DA_HEADS = 8
DA_HEAD_DIM = 64
DA_QK_DIM = 2 * DA_HEADS * DA_HEAD_DIM
DA_V_DIM = DA_HEADS * 2 * DA_HEAD_DIM
Q_BLOCK = 128
ROPE_BASE = 10000.0

PEER_HEADS = 8
PEER_NKEYS = 128
PEER_EXPERTS = PEER_NKEYS ** 2
PEER_DKEY = 256
PEER_TOPK = 16
PEER_BLOCK = 128

kernel_name = 'hybrid_ssd_diffattn_peer_dit'


def rmsnorm(x, g):
    xf = x.astype(jnp.float32)
    y = xf * lax.rsqrt(jnp.mean(xf * xf, axis=-1, keepdims=True) + EPS)
    return (y * g.astype(jnp.float32)).astype(x.dtype)


def modulate(x, g, shift, scale):
    return rmsnorm(x, g) * (1.0 + scale) + shift


def ada_mod(cvec, w, b):
    m = jax.nn.silu(cvec) @ w + b
    return jnp.split(m[..., None, :], 6, axis=-1)


def flip(t):
    return jnp.flip(t, axis=1)


def axial_rope_tables(length):
    rows = length // GRID_W
    row = jnp.repeat(jnp.arange(rows, dtype=jnp.float32), GRID_W)
    col = jnp.tile(jnp.arange(GRID_W, dtype=jnp.float32), rows)
    half = DA_HEAD_DIM // 2
    freqs = ROPE_BASE ** (-jnp.arange(0, half, 2, dtype=jnp.float32) / half)
    ang_r = row[:, None] * freqs
    ang_c = col[:, None] * freqs
    ang = jnp.concatenate([ang_r, ang_r, ang_c, ang_c], axis=-1)
    return jnp.cos(ang), jnp.sin(ang)


def rotate_half_axial(x):
    x1, x2, x3, x4 = jnp.split(x, 4, axis=-1)
    return jnp.concatenate([-x2, x1, -x4, x3], axis=-1)


def apply_rope(x, cos, sin):
    return x * cos.astype(x.dtype) + rotate_half_axial(x) * sin.astype(x.dtype)


def centred_dwconv(x, w, b):
    pad = SSD_CONV_W // 2
    length = x.shape[1]
    xp = jnp.pad(x, ((0, 0), (pad, pad), (0, 0)))
    y = b + sum(xp[:, k:k + length] * w[k] for k in range(SSD_CONV_W))
    return jax.nn.silu(y)


def ssd_chunk_scan(xs, dt, a, bm, cm, d, h0):
    bsz, length = xs.shape[:2]
    nc = length // SSD_CHUNK

    def chunks(t):
        return jnp.moveaxis(t.reshape(bsz, nc, SSD_CHUNK, *t.shape[2:]), 1, 0)

    lower = jnp.tril(jnp.ones((SSD_CHUNK, SSD_CHUNK), dtype=bool))[None, :, :, None, None]

    def step(h, inp):
        xc, dtc, bc, cc = inp
        dtc = dtc.astype(jnp.float32)
        xf = xc.astype(jnp.float32)
        bf = bc.astype(jnp.float32)
        cf = cc.astype(jnp.float32)
        cs = jnp.cumsum(dtc * a, axis=1)
        seg = cs[:, :, None] - cs[:, None, :]
        decay = jnp.exp(jnp.where(lower, seg, -jnp.inf))
        cb = jnp.einsum('bign,bjgn->bijg', cf, bf)
        w = cb[..., None] * decay * dtc[:, None]
        y = jnp.einsum('bijgr,bjgrp->bigrp', w, xf)
        y = y + jnp.einsum('bign,bgrpn->bigrp', cf, h) * jnp.exp(cs)[..., None]
        y = y + d[..., None].astype(jnp.float32) * xf
        w_end = jnp.exp(cs[:, -1:] - cs) * dtc
        h = jnp.exp(cs[:, -1])[..., None, None] * h + jnp.einsum('bjgr,bjgn,bjgrp->bgrpn', w_end, bf, xf)
        return h, y.astype(xc.dtype)

    h_final, ys = lax.scan(step, h0, (chunks(xs), chunks(dt), chunks(bm), chunks(cm)))
    return jnp.moveaxis(ys, 0, 1).reshape(xs.shape), h_final


def ssd_project(h, in_w, conv_w, conv_b, dt_bias):
    bsz, length, _ = h.shape
    zxbcdt = h @ in_w
    z = zxbcdt[..., :SSD_D_INNER]
    xbc = centred_dwconv(zxbcdt[..., SSD_D_INNER:SSD_D_INNER + SSD_CONV_DIM], conv_w, conv_b)
    dt = zxbcdt[..., SSD_D_INNER + SSD_CONV_DIM:].reshape(bsz, length, 2, SSD_HEADS)
    dt = jax.nn.softplus(dt.astype(jnp.float32) + dt_bias.astype(jnp.float32))
    dt = dt.reshape(bsz, length, 2, SSD_GROUPS, SSD_HPG)
    gn = SSD_GROUPS * SSD_STATE
    xs = xbc[..., :SSD_D_INNER].reshape(bsz, length, SSD_GROUPS, SSD_HPG, SSD_HEADDIM)
    bm = xbc[..., SSD_D_INNER:SSD_D_INNER + gn].reshape(bsz, length, SSD_GROUPS, SSD_STATE)
    cm = xbc[..., SSD_D_INNER + gn:].reshape(bsz, length, SSD_GROUPS, SSD_STATE)
    return z, xs, dt, bm, cm


def ssd_layer(h_lat, h_ctx, in_w, conv_w, conv_b, dt_bias, a_log, d_skip, norm_g, out_w):
    a = -jnp.exp(a_log.astype(jnp.float32)).reshape(2, SSD_GROUPS, SSD_HPG)
    d = d_skip.reshape(2, SSD_GROUPS, SSD_HPG)

    def mix(h, h0_f, h0_b):
        z, xs, dt, bm, cm = ssd_project(h, in_w, conv_w, conv_b, dt_bias)
        y_f, h_f = ssd_chunk_scan(xs, dt[:, :, 0], a[0], bm, cm, d[0], h0_f)
        y_b, h_b = ssd_chunk_scan(flip(xs), flip(dt[:, :, 1]), a[1], flip(bm), flip(cm), d[1], h0_b)
        y = (y_f + flip(y_b)).reshape(h.shape[0], h.shape[1], SSD_D_INNER)
        return rmsnorm(y * jax.nn.silu(z), norm_g) @ out_w, h_f, h_b

    h0 = jnp.zeros((h_ctx.shape[0], SSD_GROUPS, SSD_HPG, SSD_HEADDIM, SSD_STATE), jnp.float32)
    y_ctx, hc_f, hc_b = mix(h_ctx, h0, h0)
    y_lat, _, _ = mix(h_lat, hc_f, hc_b)
    return y_lat, y_ctx


def diff_attend(q, k, v, lam):
    s = jnp.einsum('mbhqd,mbhkd->mbhqk', q, k).astype(jnp.float32)
    p = jax.nn.softmax(s, axis=-1)
    w = p[0] - lam * p[1]
    return jnp.einsum('bhqk,bhkv->bhqv', w.astype(v.dtype), v)


def diff_attention_layer(h_lat, h_ctx, qkv_w, lam_q, lam_k, subln_g, out_w, lambda_init, cos, sin, ctx_out):
    lam = (jnp.exp(jnp.sum(lam_q[0] * lam_k[0]).astype(jnp.float32))
           - jnp.exp(jnp.sum(lam_q[1] * lam_k[1]).astype(jnp.float32)) + lambda_init)

    def project(h):
        bsz, length, _ = h.shape
        q, k, v = jnp.split(h @ qkv_w, [DA_QK_DIM, 2 * DA_QK_DIM], axis=-1)
        q = q.reshape(bsz, length, 2, DA_HEADS, DA_HEAD_DIM).transpose(2, 0, 3, 1, 4) * DA_HEAD_DIM ** -0.5
        k = k.reshape(bsz, length, 2, DA_HEADS, DA_HEAD_DIM).transpose(2, 0, 3, 1, 4)
        v = v.reshape(bsz, length, DA_HEADS, 2 * DA_HEAD_DIM).transpose(0, 2, 1, 3)
        return q, k, v

    def merge_heads(o):
        bsz, _, length, _ = o.shape
        o = rmsnorm(o, subln_g) * (1.0 - lambda_init)
        return o.transpose(0, 2, 1, 3).reshape(bsz, length, DA_V_DIM) @ out_w

    q_c, k_c, v_c = project(h_ctx)
    q_l, k_l, v_l = project(h_lat)
    q_l = apply_rope(q_l, cos, sin)
    k_l = apply_rope(k_l, cos, sin)
    k_all = jnp.concatenate([k_c, k_l], axis=3)
    v_all = jnp.concatenate([v_c, v_l], axis=2)
    bsz, length = h_lat.shape[:2]
    nb = length // Q_BLOCK
    q_blocks = jnp.moveaxis(q_l.reshape(2, bsz, DA_HEADS, nb, Q_BLOCK, DA_HEAD_DIM), 3, 0)
    o_blocks = lax.map(lambda qb: diff_attend(qb, k_all, v_all, lam), q_blocks)
    o_lat = jnp.moveaxis(o_blocks, 0, 2).reshape(bsz, DA_HEADS, length, 2 * DA_HEAD_DIM)
    y_lat = merge_heads(o_lat)
    y_ctx = merge_heads(diff_attend(q_c, k_c, v_c, lam)) if ctx_out else None
    return y_lat, y_ctx


def peer_ffn(h, w_q, sub_keys, u_tab, v_tab):
    bsz, length, _ = h.shape
    q = (h @ w_q).reshape(bsz, length, PEER_HEADS, 2, PEER_DKEY // 2)
    s = jnp.einsum('blhcd,hcnd->blhcn', q, sub_keys).astype(jnp.float32)
    s_top, i_top = lax.top_k(s, PEER_TOPK)
    kk = PEER_TOPK * PEER_TOPK
    cand_s = (s_top[..., 0, :, None] + s_top[..., 1, None, :]).reshape(bsz, length, PEER_HEADS, kk)
    cand_i = (i_top[..., 0, :, None] * PEER_NKEYS + i_top[..., 1, None, :]).reshape(bsz, length, PEER_HEADS, kk)
    best_s, best_pos = lax.top_k(cand_s, PEER_TOPK)
    best_i = jnp.take_along_axis(cand_i, best_pos, axis=-1)
    gate = jax.nn.softmax(best_s, axis=-1)
    ids = best_i.reshape(bsz, length, PEER_HEADS * PEER_TOPK)
    gate = gate.reshape(bsz, length, PEER_HEADS * PEER_TOPK).astype(h.dtype)
    nb = length // PEER_BLOCK

    def blocks(t):
        return jnp.moveaxis(t.reshape(bsz, nb, PEER_BLOCK, *t.shape[2:]), 1, 0)

    def expert_block(args):
        hb, ib, gb = args
        act = jax.nn.gelu(jnp.einsum('btd,bted->bte', hb, u_tab[ib]), approximate=False)
        return jnp.einsum('bte,bted->btd', gb * act, v_tab[ib])

    out = lax.map(expert_block, (blocks(h), blocks(ids), blocks(gate)))
    return jnp.moveaxis(out, 0, 1).reshape(bsz, length, D_MODEL)


def setup_inputs(seed: int = 0) -> dict:
    key = jax.random.key(seed)
    ks = jax.random.split(key, 32)
    f32 = jnp.float32
    D = D_MODEL

    def nrm(k, shape, s):
        return jax.random.normal(k, shape, f32) * s

    dt0 = jnp.exp(jax.random.uniform(ks[11], (N_SSD_LAYERS, 2, SSD_HEADS), f32, math.log(1e-3), math.log(1e-1)))
    return {
        'x': nrm(ks[0], (BATCH, SEQ, D), 1.0),
        'c': nrm(ks[1], (BATCH, D), 1.0),
        'ctx': nrm(ks[2], (BATCH, CTX_LEN, D), 1.0),
        'c_ctx': nrm(ks[3], (D,), 1.0),
        'ada_w': nrm(ks[4], (DEPTH, D, 6 * D), 0.5 * D ** -0.5),
        'ada_b': nrm(ks[5], (DEPTH, 6 * D), 0.02),
        'norm_mix_g': 1.0 + nrm(ks[6], (DEPTH, D), 0.02),
        'norm_ffn_g': 1.0 + nrm(ks[7], (DEPTH, D), 0.02),
        'ssd_in_w': nrm(ks[8], (N_SSD_LAYERS, D, SSD_IN_DIM), D ** -0.5),
        'ssd_conv_w': nrm(ks[9], (N_SSD_LAYERS, SSD_CONV_W, SSD_CONV_DIM), SSD_CONV_W ** -0.5),
        'ssd_conv_b': nrm(ks[10], (N_SSD_LAYERS, SSD_CONV_DIM), 0.02),
        'ssd_dt_bias': dt0 + jnp.log(-jnp.expm1(-dt0)),
        'ssd_a_log': jnp.log(jax.random.uniform(ks[12], (N_SSD_LAYERS, 2, SSD_HEADS), f32, 1.0, 16.0)),
        'ssd_d': 1.0 + nrm(ks[13], (N_SSD_LAYERS, 2, SSD_HEADS), 0.1),
        'ssd_norm_g': 1.0 + nrm(ks[14], (N_SSD_LAYERS, SSD_D_INNER), 0.02),
        'ssd_out_w': nrm(ks[15], (N_SSD_LAYERS, SSD_D_INNER, D), SSD_D_INNER ** -0.5),
        'attn_qkv_w': nrm(ks[16], (N_ATTN_LAYERS, D, 2 * DA_QK_DIM + DA_V_DIM), D ** -0.5),
        'attn_lambda_q': nrm(ks[17], (N_ATTN_LAYERS, 2, DA_HEAD_DIM), 0.1),
        'attn_lambda_k': nrm(ks[18], (N_ATTN_LAYERS, 2, DA_HEAD_DIM), 0.1),
        'attn_subln_g': 1.0 + nrm(ks[19], (N_ATTN_LAYERS, 2 * DA_HEAD_DIM), 0.02),
        'attn_out_w': nrm(ks[20], (N_ATTN_LAYERS, DA_V_DIM, D), DA_V_DIM ** -0.5),
        'peer_q_w': nrm(ks[21], (DEPTH, D, PEER_HEADS * PEER_DKEY), D ** -0.5),
        'peer_keys': nrm(ks[22], (DEPTH, PEER_HEADS, 2, PEER_NKEYS, PEER_DKEY // 2), (PEER_DKEY // 2) ** -0.5),
        'peer_u': nrm(ks[23], (DEPTH, PEER_EXPERTS, D), D ** -0.5),
        'peer_v': nrm(ks[24], (DEPTH, PEER_EXPERTS, D), 0.5),
        'final_norm_g': 1.0 + nrm(ks[25], (D,), 0.02),
    }


def reference(x, c, ctx, c_ctx, ada_w, ada_b, norm_mix_g, norm_ffn_g, ssd_in_w, ssd_conv_w, ssd_conv_b,
              ssd_dt_bias, ssd_a_log, ssd_d, ssd_norm_g, ssd_out_w, attn_qkv_w, attn_lambda_q, attn_lambda_k,
              attn_subln_g, attn_out_w, peer_q_w, peer_keys, peer_u, peer_v, final_norm_g):
    cos, sin = axial_rope_tables(x.shape[1])
    for i in range(DEPTH):
        last = i == DEPTH - 1
        sh1, sc1, g1, sh2, sc2, g2 = ada_mod(c, ada_w[i], ada_b[i])
        csh1, csc1, cg1, csh2, csc2, cg2 = ada_mod(c_ctx, ada_w[i], ada_b[i])
        h = modulate(x, norm_mix_g[i], sh1, sc1)
        hc = modulate(ctx, norm_mix_g[i], csh1, csc1)
        j = i // N_MIXERS
        if i % N_MIXERS == 0:
            y, yc = ssd_layer(h, hc, ssd_in_w[j], ssd_conv_w[j], ssd_conv_b[j], ssd_dt_bias[j],
                              ssd_a_log[j], ssd_d[j], ssd_norm_g[j], ssd_out_w[j])
        else:
            lambda_init = 0.8 - 0.6 * math.exp(-0.3 * i)
            y, yc = diff_attention_layer(h, hc, attn_qkv_w[j], attn_lambda_q[j], attn_lambda_k[j],
                                         attn_subln_g[j], attn_out_w[j], lambda_init, cos, sin, not last)
        x = x + g1 * y
        x = x + g2 * peer_ffn(modulate(x, norm_ffn_g[i], sh2, sc2), peer_q_w[i], peer_keys[i], peer_u[i], peer_v[i])
        if not last:
            ctx = ctx + cg1 * yc
            ctx = ctx + cg2 * peer_ffn(modulate(ctx, norm_ffn_g[i], csh2, csc2),
                                       peer_q_w[i], peer_keys[i], peer_u[i], peer_v[i])
    return rmsnorm(x, final_norm_g)
```

```python
import functools
import math

import numpy as np
import jax
import jax.numpy as jnp
from jax import lax
from jax.experimental import pallas as pl
from jax.experimental.pallas import tpu as pltpu

F32 = jnp.float32
BF16 = jnp.bfloat16

EPS = 1e-6
GRID_W = 64
ROPE_BASE = 10000.0

SSD_HEADDIM = 64
SSD_GROUPS = 4
SSD_HPG = 8
SSD_HEADS = SSD_GROUPS * SSD_HPG
SSD_STATE = 128
SSD_CONV_W = 5
SSD_CHUNK = 128
SSD_D_INNER = SSD_HEADS * SSD_HEADDIM
SSD_GN = SSD_GROUPS * SSD_STATE
SSD_CONV_DIM = SSD_D_INNER + 2 * SSD_GN

DA_HEADS = 8
DA_HEAD_DIM = 64

PEER_HEADS = 8
PEER_NKEYS = 128
PEER_TOPK = 16
PEER_EBLK = 1024

LANE = 128
TOK_TILE = 512
CONV_TILE = 256
ATT_TQ = 512
ATT_TK = 512
VMEM_LIMIT = 56 * 1024 * 1024

NT_DIMS = (((1,), (1,)), ((), ()))


def _cparams(sem):
    return pltpu.CompilerParams(dimension_semantics=sem, vmem_limit_bytes=VMEM_LIMIT)


def _rms(xf, g):
    return xf * lax.rsqrt(jnp.mean(xf * xf, axis=-1, keepdims=True) + EPS) * g


def _silu(x):
    return x * (1.0 / (1.0 + jnp.exp(-x)))


def _ada_kernel(c_ref, w_ref, b_ref, o_ref):
    c = c_ref[...]
    o_ref[0] = jnp.dot(_silu(c), w_ref[0], preferred_element_type=F32,
                       precision=lax.Precision.HIGHEST) + b_ref[0]


def _ada_call(cc, ada_w, ada_b):
    depth, d, n = ada_w.shape
    tn = 1536
    return pl.pallas_call(
        _ada_kernel,
        grid=(depth, n // tn),
        in_specs=[pl.BlockSpec((8, d), lambda l, j: (0, 0)),
                  pl.BlockSpec((1, d, tn), lambda l, j: (l, 0, j)),
                  pl.BlockSpec((1, 1, tn), lambda l, j: (l, 0, j))],
        out_specs=pl.BlockSpec((1, 8, tn), lambda l, j: (l, 0, j)),
        out_shape=jax.ShapeDtypeStruct((depth, 8, n), F32),
        compiler_params=_cparams(("parallel", "parallel")),
        name="ada_mod",
    )(cc, ada_w, ada_b.reshape(depth, 1, n))


def _mod_prologue(x_ref, g_ref, mod_ref, hn_ref, sh, sc):
    m = mod_ref[0]
    hn = _rms(x_ref[0], g_ref[...]) * (1.0 + m[sc:sc + 1, :]) + m[sh:sh + 1, :]
    hn_ref[...] = hn.astype(BF16)


def _proj_mod_kernel(x_ref, g_ref, mod_ref, w_ref, o_ref, hn_ref, *, sh, sc):
    @pl.when(pl.program_id(2) == 0)
    def _():
        _mod_prologue(x_ref, g_ref, mod_ref, hn_ref, sh, sc)

    o_ref[0] = jnp.dot(hn_ref[...], w_ref[...], preferred_element_type=F32).astype(o_ref.dtype)


def _rope_block(a, cos, sin, lane):
    fwd = pltpu.roll(a, LANE - 16, axis=1)
    bwd = pltpu.roll(a, 16, axis=1)
    rot = jnp.where(lane % 32 < 16, -fwd, bwd)
    return a * cos + rot * sin


def _proj_qkv_kernel(x_ref, g_ref, mod_ref, w_ref, cos_ref, sin_ref, o_ref, hn_ref, *, sh, sc):
    j = pl.program_id(2)

    @pl.when(j == 0)
    def _():
        _mod_prologue(x_ref, g_ref, mod_ref, hn_ref, sh, sc)

    acc = jnp.dot(hn_ref[...], w_ref[...], preferred_element_type=F32)
    tn = acc.shape[1]

    @pl.when(j < 2)
    def _():
        cos = cos_ref[...]
        sin = sin_ref[...]
        lane = lax.broadcasted_iota(jnp.int32, cos.shape, 1)
        scale = jnp.where(j == 0, DA_HEAD_DIM ** -0.5, 1.0).astype(F32)
        for cb in range(tn // LANE):
            a = acc[:, cb * LANE:(cb + 1) * LANE]
            o_ref[0, :, cb * LANE:(cb + 1) * LANE] = (
                _rope_block(a, cos, sin, lane) * scale).astype(o_ref.dtype)

    @pl.when(j >= 2)
    def _():
        o_ref[0] = acc.astype(o_ref.dtype)


def _proj_gate_resid_kernel(yf_ref, yb_ref, z_ref, g_ref, w_ref, xr_ref, gate_ref, o_ref, hn_ref,
                            *, gi):
    @pl.when(pl.program_id(2) == 0)
    def _():
        u = (yf_ref[0, 0].astype(F32) + yb_ref[0, 0].astype(F32)) * _silu(z_ref[0].astype(F32))
        hn_ref[...] = _rms(u, g_ref[...]).astype(BF16)

    acc = jnp.dot(hn_ref[...], w_ref[...], preferred_element_type=F32)
    o_ref[0] = xr_ref[0] + gate_ref[0][gi:gi + 1, :] * acc


def _proj_plain_resid_kernel(h_ref, w_ref, xr_ref, gate_ref, o_ref, *, gi):
    acc = jnp.dot(h_ref[0], w_ref[...], preferred_element_type=F32)
    o_ref[0] = xr_ref[0] + gate_ref[0][gi:gi + 1, :] * acc


def _proj_scores_kernel(x_ref, g_ref, mod_ref, w_ref, keys_ref, s_ref, hq_ref, hn_ref, *, sh, sc):
    @pl.when(pl.program_id(2) == 0)
    def _():
        _mod_prologue(x_ref, g_ref, mod_ref, hn_ref, sh, sc)
        hq_ref[0] = hn_ref[...]

    q = jnp.dot(hn_ref[...], w_ref[...], preferred_element_type=F32)
    half = PEER_NKEYS
    for c in range(2):
        s_ref[0, 0, c] = lax.dot_general(
            keys_ref[0, c], q[:, c * half:(c + 1) * half], NT_DIMS,
            preferred_element_type=F32, precision=lax.Precision.HIGHEST)


class _Stream:
    def __init__(self, batch, s_lat, s_ctx, d):
        assert s_lat % TOK_TILE == 0 and s_ctx % CONV_TILE == 0 and s_lat % GRID_W == 0
        self.batch, self.s_lat, self.s_ctx, self.d = batch, s_lat, s_ctx, d
        self.s_real = s_lat + s_ctx
        self.s_pad = -(-self.s_real // TOK_TILE) * TOK_TILE
        self.n_tok = self.s_pad // TOK_TILE
        self.n_lat_tok = s_lat // TOK_TILE

    def mod_row(self, b, i):
        return b * 2 + jnp.where(i < self.n_lat_tok, 1, 0)


def _mod_specs(st, d):
    return [pl.BlockSpec((1, TOK_TILE, d), lambda b, i, j: (b, i, 0)),
            pl.BlockSpec((1, d), lambda b, i, j: (0, 0)),
            pl.BlockSpec((1, 6, d), lambda b, i, j: (st.mod_row(b, i), 0, 0))]


def _proj_mod(st, x, g, modtab, w, sh, sc, tn, out_dtype, name):
    d, n = w.shape
    return pl.pallas_call(
        functools.partial(_proj_mod_kernel, sh=sh, sc=sc),
        grid=(st.batch, st.n_tok, n // tn),
        in_specs=_mod_specs(st, d) + [pl.BlockSpec((d, tn), lambda b, i, j: (0, j))],
        out_specs=pl.BlockSpec((1, TOK_TILE, tn), lambda b, i, j: (b, i, j)),
        out_shape=jax.ShapeDtypeStruct((st.batch, st.s_pad, n), out_dtype),
        scratch_shapes=[pltpu.VMEM((TOK_TILE, d), BF16)],
        compiler_params=_cparams(("parallel", "parallel", "arbitrary")),
        name=name,
    )(x, g.reshape(1, d), modtab, w)


def _proj_qkv(st, x, g, modtab, w, cos, sin, sh, sc):
    d, n = w.shape
    tn = 1024
    return pl.pallas_call(
        functools.partial(_proj_qkv_kernel, sh=sh, sc=sc),
        grid=(st.batch, st.n_tok, n // tn),
        in_specs=_mod_specs(st, d) + [
            pl.BlockSpec((d, tn), lambda b, i, j: (0, j)),
            pl.BlockSpec((TOK_TILE, LANE), lambda b, i, j: (i, 0)),
            pl.BlockSpec((TOK_TILE, LANE), lambda b, i, j: (i, 0))],
        out_specs=pl.BlockSpec((1, TOK_TILE, tn), lambda b, i, j: (b, i, j)),
        out_shape=jax.ShapeDtypeStruct((st.batch, st.s_pad, n), BF16),
        scratch_shapes=[pltpu.VMEM((TOK_TILE, d), BF16)],
        compiler_params=_cparams(("parallel", "parallel", "arbitrary")),
        name="attn_qkv_rope",
    )(x, g.reshape(1, d), modtab, w, cos, sin)


def _proj_gate_resid(st, y2, zx, g, w, x, modtab, gi):
    k, n = w.shape
    tn = n
    return pl.pallas_call(
        functools.partial(_proj_gate_resid_kernel, gi=gi),
        grid=(st.batch, st.n_tok, n // tn),
        in_specs=[pl.BlockSpec((1, 1, TOK_TILE, k), lambda b, i, j: (0, b, i, 0)),
                  pl.BlockSpec((1, 1, TOK_TILE, k), lambda b, i, j: (1, b, i, 0)),
                  pl.BlockSpec((1, TOK_TILE, k), lambda b, i, j: (b, i, 0)),
                  pl.BlockSpec((1, k), lambda b, i, j: (0, 0)),
                  pl.BlockSpec((k, tn), lambda b, i, j: (0, j)),
                  pl.BlockSpec((1, TOK_TILE, tn), lambda b, i, j: (b, i, j)),
                  pl.BlockSpec((1, 6, tn), lambda b, i, j: (st.mod_row(b, i), 0, j))],
        out_specs=pl.BlockSpec((1, TOK_TILE, tn), lambda b, i, j: (b, i, j)),
        out_shape=jax.ShapeDtypeStruct((st.batch, st.s_pad, n), F32),
        scratch_shapes=[pltpu.VMEM((TOK_TILE, k), BF16)],
        compiler_params=_cparams(("parallel", "parallel", "arbitrary")),
        name="ssd_out_proj",
    )(y2, y2, zx, g.reshape(1, k), w, x, modtab)


def _proj_plain_resid(st, h, w, x, modtab, gi):
    k, n = w.shape
    tn = n
    return pl.pallas_call(
        functools.partial(_proj_plain_resid_kernel, gi=gi),
        grid=(st.batch, st.n_tok, n // tn),
        in_specs=[pl.BlockSpec((1, TOK_TILE, k), lambda b, i, j: (b, i, 0)),
                  pl.BlockSpec((k, tn), lambda b, i, j: (0, j)),
                  pl.BlockSpec((1, TOK_TILE, tn), lambda b, i, j: (b, i, j)),
                  pl.BlockSpec((1, 6, tn), lambda b, i, j: (st.mod_row(b, i), 0, j))],
        out_specs=pl.BlockSpec((1, TOK_TILE, tn), lambda b, i, j: (b, i, j)),
        out_shape=jax.ShapeDtypeStruct((st.batch, st.s_pad, n), F32),
        compiler_params=_cparams(("parallel", "parallel", "arbitrary")),
        name="attn_out_proj",
    )(h, w, x, modtab)


def _proj_scores(st, x, g, modtab, w, keys, sh, sc):
    d, n = w.shape
    tn = 2 * PEER_NKEYS
    return pl.pallas_call(
        functools.partial(_proj_scores_kernel, sh=sh, sc=sc),
        grid=(st.batch, st.n_tok, PEER_HEADS),
        in_specs=_mod_specs(st, d) + [
            pl.BlockSpec((d, tn), lambda b, i, j: (0, j)),
            pl.BlockSpec((1, 2, PEER_NKEYS, PEER_NKEYS), lambda b, i, j: (j, 0, 0, 0))],
        out_specs=[pl.BlockSpec((1, 1, 2, PEER_NKEYS, TOK_TILE), lambda b, i, j: (b, j, 0, 0, i)),
                   pl.BlockSpec((1, TOK_TILE, d), lambda b, i, j: (b, i, 0))],
        out_shape=[jax.ShapeDtypeStruct((st.batch, PEER_HEADS, 2, PEER_NKEYS, st.s_pad), F32),
                   jax.ShapeDtypeStruct((st.batch, st.s_pad, d), BF16)],
        scratch_shapes=[pltpu.VMEM((TOK_TILE, d), BF16)],
        compiler_params=_cparams(("parallel", "parallel", "arbitrary")),
        name="peer_scores",
    )(x, g.reshape(1, d), modtab, w, keys)


def _conv_kernel(x_ref, prev_ref, next_ref, w_ref, b_ref, o_ref, buf_ref, *, seg_starts, n_tiles):
    i = pl.program_id(1)
    pad = SSD_CONV_W // 2
    first = functools.reduce(jnp.logical_or, [i == s for s in seg_starts])
    last = functools.reduce(jnp.logical_or, [i == s - 1 for s in seg_starts[1:] + (n_tiles,)])
    t = x_ref.shape[1]
    buf_ref[8:8 + t, :] = x_ref[0].astype(F32)
    buf_ref[0:8, :] = jnp.where(first, 0.0, prev_ref[0].astype(F32)[8:16, :])
    buf_ref[8 + t:16 + t, :] = jnp.where(last, 0.0, next_ref[0].astype(F32)[0:8, :])
    w = w_ref[...]
    acc = b_ref[...] + w[0:1, :] * buf_ref[8 - pad:8 - pad + t, :]
    for k in range(1, SSD_CONV_W):
        acc = acc + w[k:k + 1, :] * buf_ref[8 - pad + k:8 - pad + k + t, :]
    o_ref[0] = _silu(acc).astype(o_ref.dtype)


def _conv_call(st, zx, conv_w, conv_b):
    cw = 1024
    n_cblk = SSD_CONV_DIM // cw
    off = SSD_D_INNER // cw
    n_tiles = st.s_pad // CONV_TILE
    hb = CONV_TILE // 16
    n_hblk = st.s_pad // 16
    seg_starts = (0, st.s_lat // CONV_TILE, st.s_real // CONV_TILE)
    w8 = jnp.zeros((8, SSD_CONV_DIM), F32).at[:SSD_CONV_W].set(conv_w)
    return pl.pallas_call(
        functools.partial(_conv_kernel, seg_starts=seg_starts, n_tiles=n_tiles),
        grid=(st.batch, n_tiles, n_cblk),
        in_specs=[pl.BlockSpec((1, CONV_TILE, cw), lambda b, i, c: (b, i, off + c)),
                  pl.BlockSpec((1, 16, cw), lambda b, i, c: (b, jnp.maximum(i * hb - 1, 0), off + c)),
                  pl.BlockSpec((1, 16, cw),
                               lambda b, i, c: (b, jnp.minimum((i + 1) * hb, n_hblk - 1), off + c)),
                  pl.BlockSpec((8, cw), lambda b, i, c: (0, c)),
                  pl.BlockSpec((1, cw), lambda b, i, c: (0, c))],
        out_specs=pl.BlockSpec((1, CONV_TILE, cw), lambda b, i, c: (b, i, c)),
        out_shape=jax.ShapeDtypeStruct((st.batch, st.s_pad, SSD_CONV_DIM), BF16),
        scratch_shapes=[pltpu.VMEM((CONV_TILE + 16, cw), F32)],
        compiler_params=_cparams(("parallel", "parallel", "parallel")),
        name="ssd_conv",
    )(zx, zx, zx, w8, conv_b.reshape(1, SSD_CONV_DIM))


def _scan_kernel(xs_ref, bm_ref, cm_ref, dt_ref, tri_ref, par_ref, y_ref, h_ref):
    @pl.when(pl.program_id(3) == 0)
    def _():
        h_ref[...] = jnp.zeros_like(h_ref)

    par = par_ref[0]
    raw = dt_ref[0] + par[0:1, :]
    dt = jnp.maximum(raw, 0.0) + jnp.log1p(jnp.exp(-jnp.abs(raw)))
    a = -jnp.exp(par[1:2, :])
    dsk = par[2:3, :]
    dta = dt * a
    tri = tri_ref[0]
    cs = jnp.dot(tri, dta, preferred_element_type=F32, precision=lax.Precision.HIGHEST)
    tot = jnp.sum(dta, axis=0, keepdims=True)
    e_tot = jnp.exp(tot)
    e_cs = jnp.exp(cs)
    w_end = jnp.exp(tot - cs) * dt
    cs_t = cs.T
    dt_t = dt.T
    w_end_t = w_end.T
    mask = tri > 0.5

    xs = xs_ref[0]
    bm = bm_ref[0]
    cm = cm_ref[0]
    bm_t = bm.astype(F32).T
    cb = lax.dot_general(cm, bm, NT_DIMS, preferred_element_type=F32)
    p = SSD_HEADDIM
    for r in range(SSD_HPG):
        seg = cs[:, r:r + 1] - cs_t[r:r + 1, :]
        decay = jnp.where(mask, jnp.exp(seg), 0.0)
        w = cb * decay * dt_t[r:r + 1, :]
        xh = xs[:, r * p:(r + 1) * p]
        h_prev = h_ref[r]
        y = jnp.dot(w.astype(BF16), xh, preferred_element_type=F32)
        y = y + jnp.dot(cm, h_prev.astype(BF16), preferred_element_type=F32) * e_cs[:, r:r + 1]
        y = y + dsk[:, r:r + 1] * xh.astype(F32)
        y_ref[0, 0, :, r * p:(r + 1) * p] = y.astype(y_ref.dtype)
        upd = jnp.dot((bm_t * w_end_t[r:r + 1, :]).astype(BF16), xh, preferred_element_type=F32)
        h_ref[r] = e_tot[:, r:r + 1] * h_prev + upd


def _scan_call(st, xc, dtr, par):
    q = SSD_CHUNK
    n_lat, n_ctx = st.s_lat // q, st.s_ctx // q
    n_real = n_lat + n_ctx
    n_all = st.s_pad // q
    gw = SSD_HPG * SSD_HEADDIM
    b_off = SSD_D_INNER // SSD_STATE
    c_off = b_off + SSD_GROUPS

    def chunk(d, s):
        fwd = jnp.where(s < n_ctx, n_lat + s, s - n_ctx)
        bwd = n_real - 1 - s
        return jnp.where(s < n_real, jnp.where(d == 0, fwd, bwd), s)

    idx = np.arange(q)
    tri = jnp.asarray(np.stack([idx[:, None] >= idx[None, :], idx[:, None] <= idx[None, :]]), F32)
    return pl.pallas_call(
        _scan_kernel,
        grid=(st.batch, 2, SSD_GROUPS, n_all),
        in_specs=[pl.BlockSpec((1, q, gw), lambda b, d, g, s: (b, chunk(d, s), g)),
                  pl.BlockSpec((1, q, SSD_STATE), lambda b, d, g, s: (b, chunk(d, s), b_off + g)),
                  pl.BlockSpec((1, q, SSD_STATE), lambda b, d, g, s: (b, chunk(d, s), c_off + g)),
                  pl.BlockSpec((1, q, LANE), lambda b, d, g, s: (b, chunk(d, s), d * SSD_GROUPS + g)),
                  pl.BlockSpec((1, q, q), lambda b, d, g, s: (d, 0, 0)),
                  pl.BlockSpec((1, 8, LANE), lambda b, d, g, s: (d * SSD_GROUPS + g, 0, 0))],
        out_specs=pl.BlockSpec((1, 1, q, gw), lambda b, d, g, s: (d, b, chunk(d, s), g)),
        out_shape=jax.ShapeDtypeStruct((2, st.batch, st.s_pad, SSD_D_INNER), BF16),
        scratch_shapes=[pltpu.VMEM((SSD_HPG, SSD_STATE, SSD_HEADDIM), F32)],
        compiler_params=_cparams(("parallel", "parallel", "parallel", "arbitrary")),
        name="ssd_scan",
    )(xc, xc, xc, dtr, tri, par)


def _attn_kernel(q_ref, k_ref, v_ref, lq_ref, lk_ref, g_ref, o_ref, qs_ref, m_ref, l_ref, acc_ref,
                 *, s_lat, s_ctx, lambda_init):
    qi = pl.program_id(2)
    tq = q_ref.shape[1]
    q = q_ref[0]
    lane = lax.broadcasted_iota(jnp.int32, q.shape, 1)
    zero = jnp.zeros_like(q)
    qs_ref[0:tq, :] = jnp.where(lane < DA_HEAD_DIM, q, zero)
    qs_ref[tq:2 * tq, :] = jnp.where(lane >= DA_HEAD_DIM, q, zero)
    m_ref[...] = jnp.full_like(m_ref, -jnp.inf)
    l_ref[...] = jnp.zeros_like(l_ref)
    acc_ref[...] = jnp.zeros_like(acc_ref)

    def step(start, size):
        k = k_ref[0, pl.ds(start, size), :]
        v = v_ref[0, pl.ds(start, size), :]
        s = lax.dot_general(qs_ref[...], k, NT_DIMS, preferred_element_type=F32)
        m_prev = m_ref[...]
        m_new = jnp.maximum(m_prev, jnp.max(s, axis=-1, keepdims=True))
        alpha = jnp.exp(m_prev - m_new)
        p = jnp.exp(s - m_new)
        l_ref[...] = alpha * l_ref[...] + jnp.sum(p, axis=-1, keepdims=True)
        acc_ref[...] = alpha * acc_ref[...] + jnp.dot(p.astype(BF16), v, preferred_element_type=F32)
        m_ref[...] = m_new

    n_lat_steps = jnp.where(qi < s_lat // tq, s_lat // ATT_TK, 0)

    def body(c, carry):
        step(pl.multiple_of(c * ATT_TK, ATT_TK), ATT_TK)
        return carry

    lax.fori_loop(0, n_lat_steps, body, 0)
    step(s_lat, s_ctx)

    lam = (jnp.exp(jnp.sum(lq_ref[0:1, :] * lk_ref[0:1, :], axis=-1, keepdims=True))
           - jnp.exp(jnp.sum(lq_ref[1:2, :] * lk_ref[1:2, :], axis=-1, keepdims=True)) + lambda_init)
    o_all = acc_ref[...] / l_ref[...]
    o = o_all[0:tq, :] - lam * o_all[tq:2 * tq, :]
    o_ref[0] = (_rms(o, g_ref[...]) * (1.0 - lambda_init)).astype(o_ref.dtype)


def _attn_call(st, qkv, lam_q, lam_k, subln_g, lambda_init):
    tq = ATT_TQ
    nq = st.s_pad // tq
    hw = 2 * DA_HEAD_DIM
    nh = DA_HEADS
    return pl.pallas_call(
        functools.partial(_attn_kernel, s_lat=st.s_lat, s_ctx=st.s_ctx, lambda_init=lambda_init),
        grid=(st.batch, nh, nq),
        in_specs=[pl.BlockSpec((1, tq, hw), lambda b, h, i: (b, i, h)),
                  pl.BlockSpec((1, st.s_pad, hw), lambda b, h, i: (b, 0, nh + h)),
                  pl.BlockSpec((1, st.s_pad, hw), lambda b, h, i: (b, 0, 2 * nh + h)),
                  pl.BlockSpec((2, DA_HEAD_DIM), lambda b, h, i: (0, 0)),
                  pl.BlockSpec((2, DA_HEAD_DIM), lambda b, h, i: (0, 0)),
                  pl.BlockSpec((1, hw), lambda b, h, i: (0, 0))],
        out_specs=pl.BlockSpec((1, tq, hw), lambda b, h, i: (b, i, h)),
        out_shape=jax.ShapeDtypeStruct((st.batch, st.s_pad, nh * hw), BF16),
        scratch_shapes=[pltpu.VMEM((2 * tq, hw), BF16),
                        pltpu.VMEM((2 * tq, 1), F32),
                        pltpu.VMEM((2 * tq, 1), F32),
                        pltpu.VMEM((2 * tq, hw), F32)],
        compiler_params=_cparams(("parallel", "parallel", "parallel")),
        name="diff_attn",
    )(qkv, qkv, qkv, lam_q, lam_k, subln_g.reshape(1, hw))


def _extract_top(s, dst_ref, n):
    w = s
    for k in range(n):
        m = jnp.max(w, axis=0, keepdims=True)
        dst_ref[k:k + 1, :] = m
        w = jnp.where(w == m, -jnp.inf, w)


def _gate_kernel(s_ref, e1_ref, tau_ref, e2_ref, a_ref, b_ref):
    k = PEER_TOPK
    s1 = s_ref[0, 0, 0]
    s2 = s_ref[0, 0, 1]
    _extract_top(s1, a_ref, k)
    _extract_top(s2, b_ref, k)
    a = a_ref[...]
    b = b_ref[...]
    cand = jnp.concatenate(
        [a[0:1, :] + b] + [a[i:i + 1, :] + b[0:8, :] for i in range(1, 8)] + [a[8:16, :] + b[0:1, :]],
        axis=0)
    w = cand
    tau = None
    for it in range(k):
        tau = jnp.max(w, axis=0, keepdims=True)
        if it + 1 < k:
            w = jnp.where(w == tau, -jnp.inf, w)
    top = a[0:1, :] + b[0:1, :]
    z = jnp.sum(jnp.where(cand >= tau, jnp.exp(cand - top), 0.0), axis=0, keepdims=True)
    e1_ref[0, 0] = jnp.exp(s1 - a[0:1, :])
    e2_ref[0, 0] = jnp.exp(s2 - b[0:1, :]) / z
    tau_ref[0, 0] = tau - s1


def _gate_call(st, scores):
    nk = PEER_NKEYS
    shp = jax.ShapeDtypeStruct((st.batch, PEER_HEADS, nk, st.s_pad), F32)
    spec = pl.BlockSpec((1, 1, nk, TOK_TILE), lambda b, i, h: (b, h, 0, i))
    return pl.pallas_call(
        _gate_kernel,
        grid=(st.batch, st.n_tok, PEER_HEADS),
        in_specs=[pl.BlockSpec((1, 1, 2, nk, TOK_TILE), lambda b, i, h: (b, h, 0, 0, i))],
        out_specs=[spec, spec, spec],
        out_shape=[shp, shp, shp],
        scratch_shapes=[pltpu.VMEM((PEER_TOPK, TOK_TILE), F32), pltpu.VMEM((PEER_TOPK, TOK_TILE), F32)],
        compiler_params=_cparams(("parallel", "parallel", "parallel")),
        name="peer_gates",
    )(scores)


def _expert_kernel(hq_ref, u_ref, vt_ref, e1_ref, tau_ref, s2_ref, e2_ref, xr_ref, gate_ref, o_ref,
                   acc_ref, p_ref, *, gi):
    e = pl.program_id(2)

    @pl.when(e == 0)
    def _():
        acc_ref[...] = jnp.zeros_like(acc_ref)

    nk = PEER_NKEYS
    act = lax.dot_general(u_ref[...], hq_ref[0], NT_DIMS, preferred_element_type=F32)
    for a in range(PEER_EBLK // nk):
        g = None
        for h in range(PEER_HEADS):
            sel = jnp.where(s2_ref[0, h, 0] >= tau_ref[0, h, a:a + 1, :], e2_ref[0, h], 0.0)
            term = e1_ref[0, h, a:a + 1, :] * sel
            g = term if g is None else g + term
        x = act[a * nk:(a + 1) * nk, :]
        gelu = 0.5 * x * (1.0 + lax.erf(x * np.float32(math.sqrt(0.5))))
        p_ref[a * nk:(a + 1) * nk, :] = (g * gelu).astype(BF16)
    acc_ref[...] += jnp.dot(vt_ref[...], p_ref[...], preferred_element_type=F32)

    @pl.when(e == pl.num_programs(2) - 1)
    def _():
        o_ref[0] = xr_ref[0] + gate_ref[0][gi:gi + 1, :] * acc_ref[...].T


def _expert_call(st, hq, u, vt, e1, tau, scores, e2, x, modtab, gi):
    n_exp, d = u.shape
    nk = PEER_NKEYS
    rows = PEER_EBLK // nk
    tt = TOK_TILE
    return pl.pallas_call(
        functools.partial(_expert_kernel, gi=gi),
        grid=(st.batch, st.n_tok, n_exp // PEER_EBLK),
        in_specs=[pl.BlockSpec((1, tt, d), lambda b, i, e: (b, i, 0)),
                  pl.BlockSpec((PEER_EBLK, d), lambda b, i, e: (e, 0)),
                  pl.BlockSpec((d, PEER_EBLK), lambda b, i, e: (0, e)),
                  pl.BlockSpec((1, PEER_HEADS, rows, tt), lambda b, i, e: (b, 0, e, i)),
                  pl.BlockSpec((1, PEER_HEADS, rows, tt), lambda b, i, e: (b, 0, e, i)),
                  pl.BlockSpec((1, PEER_HEADS, 1, nk, tt), lambda b, i, e: (b, 0, 1, 0, i)),
                  pl.BlockSpec((1, PEER_HEADS, nk, tt), lambda b, i, e: (b, 0, 0, i)),
                  pl.BlockSpec((1, tt, d), lambda b, i, e: (b, i, 0)),
                  pl.BlockSpec((1, 6, d), lambda b, i, e: (st.mod_row(b, i), 0, 0))],
        out_specs=pl.BlockSpec((1, tt, d), lambda b, i, e: (b, i, 0)),
        out_shape=jax.ShapeDtypeStruct((st.batch, st.s_pad, d), F32),
        scratch_shapes=[pltpu.VMEM((d, tt), F32), pltpu.VMEM((PEER_EBLK, tt), BF16)],
        compiler_params=_cparams(("parallel", "parallel", "arbitrary")),
        name="peer_experts",
    )(hq, u, vt, e1, tau, scores, e2, x, modtab)


def _final_norm_kernel(x_ref, g_ref, o_ref):
    o_ref[0] = _rms(x_ref[0], g_ref[...])


def _final_norm(st, x, g):
    d = st.d
    return pl.pallas_call(
        _final_norm_kernel,
        grid=(st.batch, st.n_lat_tok),
        in_specs=[pl.BlockSpec((1, TOK_TILE, d), lambda b, i: (b, i, 0)),
                  pl.BlockSpec((1, d), lambda b, i: (0, 0))],
        out_specs=pl.BlockSpec((1, TOK_TILE, d), lambda b, i: (b, i, 0)),
        out_shape=jax.ShapeDtypeStruct((st.batch, st.s_lat, d), F32),
        compiler_params=_cparams(("parallel", "parallel")),
        name="final_norm",
    )(x, g.reshape(1, d))


def _rope_tables(st):
    rows = st.s_lat // GRID_W
    row = jnp.repeat(jnp.arange(rows, dtype=F32), GRID_W)
    col = jnp.tile(jnp.arange(GRID_W, dtype=F32), rows)
    half = DA_HEAD_DIM // 2
    freqs = ROPE_BASE ** (-jnp.arange(0, half, 2, dtype=F32) / half)
    ang_r = row[:, None] * freqs
    ang_c = col[:, None] * freqs
    ang = jnp.concatenate([ang_r, ang_r, ang_c, ang_c] * 2, axis=-1)
    extra = st.s_pad - st.s_lat
    cos = jnp.concatenate([jnp.cos(ang), jnp.ones((extra, LANE), F32)], axis=0)
    sin = jnp.concatenate([jnp.sin(ang), jnp.zeros((extra, LANE), F32)], axis=0)
    return cos, sin


def _qkv_weight(w):
    d = w.shape[0]
    qk = DA_HEADS * DA_HEAD_DIM * 2

    def regroup(t):
        return t.reshape(d, 2, DA_HEADS, DA_HEAD_DIM).transpose(0, 2, 1, 3).reshape(d, qk)

    return jnp.concatenate([regroup(w[:, :qk]), regroup(w[:, qk:2 * qk]), w[:, 2 * qk:]], axis=1).astype(BF16)


def _dt_weight(w_dt):
    d = w_dt.shape[0]
    t = w_dt.reshape(d, 2 * SSD_GROUPS, SSD_HPG)
    return jnp.pad(t, ((0, 0), (0, 0), (0, LANE - SSD_HPG))).reshape(d, 2 * SSD_GROUPS * LANE).astype(BF16)


def _scan_params(dt_bias, a_log, d_skip):
    def lanes(t):
        return jnp.pad(t.reshape(2 * SSD_GROUPS, SSD_HPG), ((0, 0), (0, LANE - SSD_HPG)))

    rows = jnp.stack([lanes(dt_bias), lanes(a_log), lanes(d_skip)], axis=1)
    return jnp.pad(rows, ((0, 0), (0, 5), (0, 0))).astype(F32)


def kernel(x, c, ctx, c_ctx, ada_w, ada_b, norm_mix_g, norm_ffn_g, ssd_in_w, ssd_conv_w, ssd_conv_b,
           ssd_dt_bias, ssd_a_log, ssd_d, ssd_norm_g, ssd_out_w, attn_qkv_w, attn_lambda_q, attn_lambda_k,
           attn_subln_g, attn_out_w, peer_q_w, peer_keys, peer_u, peer_v, final_norm_g):
    batch, s_lat, d = x.shape
    s_ctx = ctx.shape[1]
    depth = ada_w.shape[0]
    st = _Stream(batch, s_lat, s_ctx, d)

    xs = jnp.concatenate([x, ctx, jnp.zeros((batch, st.s_pad - st.s_real, d), F32)], axis=1)

    cc = jnp.zeros((8, d), F32).at[:batch].set(c).at[batch].set(c_ctx)
    mods = _ada_call(cc, ada_w, ada_b).reshape(depth, 8, 6, d)
    cos, sin = _rope_tables(st)
    zxw = SSD_D_INNER + SSD_CONV_DIM

    for i in range(depth):
        lat = mods[i, :batch]
        con = jnp.broadcast_to(mods[i, batch], (batch, 6, d))
        modtab = jnp.stack([con, lat], axis=1).reshape(batch * 2, 6, d)
        jm = i // 2
        if i % 2 == 0:
            w_in = ssd_in_w[jm]
            zx = _proj_mod(st, xs, norm_mix_g[i], modtab, w_in[:, :zxw].astype(BF16), 0, 1, 1024, BF16,
                           "ssd_in_proj")
            dtr = _proj_mod(st, xs, norm_mix_g[i], modtab, _dt_weight(w_in[:, zxw:]), 0, 1, 1024, F32,
                            "ssd_dt_proj")
            xc = _conv_call(st, zx, ssd_conv_w[jm], ssd_conv_b[jm])
            y2 = _scan_call(st, xc, dtr, _scan_params(ssd_dt_bias[jm], ssd_a_log[jm], ssd_d[jm]))
            xs = _proj_gate_resid(st, y2, zx, ssd_norm_g[jm], ssd_out_w[jm].astype(BF16), xs, modtab, 2)
        else:
            lambda_init = 0.8 - 0.6 * math.exp(-0.3 * i)
            qkv = _proj_qkv(st, xs, norm_mix_g[i], modtab, _qkv_weight(attn_qkv_w[jm]), cos, sin, 0, 1)
            o = _attn_call(st, qkv, attn_lambda_q[jm], attn_lambda_k[jm], attn_subln_g[jm], lambda_init)
            xs = _proj_plain_resid(st, o, attn_out_w[jm].astype(BF16), xs, modtab, 2)
        scores, hq = _proj_scores(st, xs, norm_ffn_g[i], modtab, peer_q_w[i].astype(BF16), peer_keys[i], 3, 4)
        e1, tau, e2 = _gate_call(st, scores)
        xs = _expert_call(st, hq, peer_u[i].astype(BF16), peer_v[i].T.astype(BF16), e1, tau, scores, e2,
                          xs, modtab, 5)
    return _final_norm(st, xs, final_norm_g)
```

```python
import functools
import math

import numpy as np
import jax
import jax.numpy as jnp
from jax import lax
from jax.experimental import pallas as pl
from jax.experimental.pallas import tpu as pltpu

F32 = jnp.float32
BF16 = jnp.bfloat16

EPS = 1e-6
GRID_W = 64
ROPE_BASE = 10000.0

SSD_HEADDIM = 64
SSD_GROUPS = 4
SSD_HPG = 8
SSD_HEADS = SSD_GROUPS * SSD_HPG
SSD_STATE = 128
SSD_CONV_W = 5
SSD_CHUNK = 128
SSD_D_INNER = SSD_HEADS * SSD_HEADDIM
SSD_GN = SSD_GROUPS * SSD_STATE
SSD_CONV_DIM = SSD_D_INNER + 2 * SSD_GN

DA_HEADS = 8
DA_HEAD_DIM = 64

PEER_HEADS = 8
PEER_NKEYS = 128
PEER_TOPK = 16
PEER_EBLK = 1024

LANE = 128
TOK_TILE = 512
CONV_TILE = 256
ATT_TQ = 512
ATT_TK = 512
VMEM_LIMIT = 56 * 1024 * 1024

NT_DIMS = (((1,), (1,)), ((), ()))


def _cparams(sem):
    return pltpu.CompilerParams(dimension_semantics=sem, vmem_limit_bytes=VMEM_LIMIT)


def _rms(xf, g):
    return xf * lax.rsqrt(jnp.mean(xf * xf, axis=-1, keepdims=True) + EPS) * g


def _silu(x):
    return x * (1.0 / (1.0 + jnp.exp(-x)))


def _ada_kernel(c_ref, w_ref, b_ref, o_ref):
    c = c_ref[...]
    o_ref[0] = jnp.dot(_silu(c), w_ref[0], preferred_element_type=F32,
                       precision=lax.Precision.HIGHEST) + b_ref[0]


def _ada_call(cc, ada_w, ada_b):
    depth, d, n = ada_w.shape
    tn = 1536
    return pl.pallas_call(
        _ada_kernel,
        grid=(depth, n // tn),
        in_specs=[pl.BlockSpec((8, d), lambda l, j: (0, 0)),
                  pl.BlockSpec((1, d, tn), lambda l, j: (l, 0, j)),
                  pl.BlockSpec((1, 1, tn), lambda l, j: (l, 0, j))],
        out_specs=pl.BlockSpec((1, 8, tn), lambda l, j: (l, 0, j)),
        out_shape=jax.ShapeDtypeStruct((depth, 8, n), F32),
        compiler_params=_cparams(("parallel", "parallel")),
        name="ada_mod",
    )(cc, ada_w, ada_b.reshape(depth, 1, n))


def _mod_prologue(x_ref, g_ref, mod_ref, hn_ref, sh, sc):
    m = mod_ref[0]
    hn = _rms(x_ref[0], g_ref[...]) * (1.0 + m[sc:sc + 1, :]) + m[sh:sh + 1, :]
    hn_ref[...] = hn.astype(BF16)


def _proj_mod_kernel(x_ref, g_ref, mod_ref, w_ref, o_ref, hn_ref, *, sh, sc):
    @pl.when(pl.program_id(2) == 0)
    def _():
        _mod_prologue(x_ref, g_ref, mod_ref, hn_ref, sh, sc)

    o_ref[0] = jnp.dot(hn_ref[...], w_ref[...], preferred_element_type=F32).astype(o_ref.dtype)


def _rope_block(a, cos, sin, lane):
    fwd = pltpu.roll(a, LANE - 16, axis=1)
    bwd = pltpu.roll(a, 16, axis=1)
    rot = jnp.where(lane % 32 < 16, -fwd, bwd)
    return a * cos + rot * sin


def _proj_qkv_kernel(x_ref, g_ref, mod_ref, w_ref, cos_ref, sin_ref, o_ref, hn_ref, *, sh, sc):
    j = pl.program_id(2)

    @pl.when(j == 0)
    def _():
        _mod_prologue(x_ref, g_ref, mod_ref, hn_ref, sh, sc)

    acc = jnp.dot(hn_ref[...], w_ref[...], preferred_element_type=F32)
    tn = acc.shape[1]

    @pl.when(j < 2)
    def _():
        cos = cos_ref[...]
        sin = sin_ref[...]
        lane = lax.broadcasted_iota(jnp.int32, cos.shape, 1)
        scale = jnp.where(j == 0, DA_HEAD_DIM ** -0.5 * math.log2(math.e), 1.0).astype(F32)
        for cb in range(tn // LANE):
            a = acc[:, cb * LANE:(cb + 1) * LANE]
            o_ref[0, :, cb * LANE:(cb + 1) * LANE] = (
                _rope_block(a, cos, sin, lane) * scale).astype(o_ref.dtype)

    @pl.when(j >= 2)
    def _():
        o_ref[0] = acc.astype(o_ref.dtype)


def _proj_gate_resid_kernel(yf_ref, yb_ref, z_ref, g_ref, w_ref, xr_ref, gate_ref, o_ref, hn_ref,
                            *, gi):
    @pl.when(pl.program_id(2) == 0)
    def _():
        u = (yf_ref[0, 0].astype(F32) + yb_ref[0, 0].astype(F32)) * _silu(z_ref[0].astype(F32))
        hn_ref[...] = _rms(u, g_ref[...]).astype(BF16)

    acc = jnp.dot(hn_ref[...], w_ref[...], preferred_element_type=F32)
    o_ref[0] = xr_ref[0] + gate_ref[0][gi:gi + 1, :] * acc


def _proj_plain_resid_kernel(h_ref, w_ref, xr_ref, gate_ref, o_ref, *, gi):
    acc = jnp.dot(h_ref[0], w_ref[...], preferred_element_type=F32)
    o_ref[0] = xr_ref[0] + gate_ref[0][gi:gi + 1, :] * acc


def _proj_scores_kernel(x_ref, g_ref, mod_ref, w_ref, keys_ref, s_ref, hq_ref, hn_ref, *, sh, sc):
    @pl.when(pl.program_id(2) == 0)
    def _():
        _mod_prologue(x_ref, g_ref, mod_ref, hn_ref, sh, sc)
        hq_ref[0] = hn_ref[...]

    q = jnp.dot(hn_ref[...], w_ref[...], preferred_element_type=F32)
    half = PEER_NKEYS
    for c in range(2):
        s_ref[0, 0, c] = lax.dot_general(
            keys_ref[0, c], q[:, c * half:(c + 1) * half], NT_DIMS,
            preferred_element_type=F32, precision=lax.Precision.HIGHEST)


class _Stream:
    def __init__(self, batch, s_lat, s_ctx, d):
        assert s_lat % TOK_TILE == 0 and s_ctx % CONV_TILE == 0 and s_lat % GRID_W == 0
        self.batch, self.s_lat, self.s_ctx, self.d = batch, s_lat, s_ctx, d
        self.s_real = s_lat + s_ctx
        self.s_pad = -(-self.s_real // TOK_TILE) * TOK_TILE
        self.n_tok = self.s_pad // TOK_TILE
        self.n_lat_tok = s_lat // TOK_TILE

    def mod_row(self, b, i):
        return b * 2 + jnp.where(i < self.n_lat_tok, 1, 0)


def _mod_specs(st, d):
    return [pl.BlockSpec((1, TOK_TILE, d), lambda b, i, j: (b, i, 0)),
            pl.BlockSpec((1, d), lambda b, i, j: (0, 0)),
            pl.BlockSpec((1, 6, d), lambda b, i, j: (st.mod_row(b, i), 0, 0))]


def _proj_mod(st, x, g, modtab, w, sh, sc, tn, out_dtype, name):
    d, n = w.shape
    return pl.pallas_call(
        functools.partial(_proj_mod_kernel, sh=sh, sc=sc),
        grid=(st.batch, st.n_tok, n // tn),
        in_specs=_mod_specs(st, d) + [pl.BlockSpec((d, tn), lambda b, i, j: (0, j))],
        out_specs=pl.BlockSpec((1, TOK_TILE, tn), lambda b, i, j: (b, i, j)),
        out_shape=jax.ShapeDtypeStruct((st.batch, st.s_pad, n), out_dtype),
        scratch_shapes=[pltpu.VMEM((TOK_TILE, d), BF16)],
        compiler_params=_cparams(("parallel", "parallel", "arbitrary")),
        name=name,
    )(x, g.reshape(1, d), modtab, w)


def _proj_qkv(st, x, g, modtab, w, cos, sin, sh, sc):
    d, n = w.shape
    tn = 1024
    return pl.pallas_call(
        functools.partial(_proj_qkv_kernel, sh=sh, sc=sc),
        grid=(st.batch, st.n_tok, n // tn),
        in_specs=_mod_specs(st, d) + [
            pl.BlockSpec((d, tn), lambda b, i, j: (0, j)),
            pl.BlockSpec((TOK_TILE, LANE), lambda b, i, j: (i, 0)),
            pl.BlockSpec((TOK_TILE, LANE), lambda b, i, j: (i, 0))],
        out_specs=pl.BlockSpec((1, TOK_TILE, tn), lambda b, i, j: (b, i, j)),
        out_shape=jax.ShapeDtypeStruct((st.batch, st.s_pad, n), BF16),
        scratch_shapes=[pltpu.VMEM((TOK_TILE, d), BF16)],
        compiler_params=_cparams(("parallel", "parallel", "arbitrary")),
        name="attn_qkv_rope",
    )(x, g.reshape(1, d), modtab, w, cos, sin)


def _proj_gate_resid(st, y2, zx, g, w, x, modtab, gi):
    k, n = w.shape
    tn = n
    return pl.pallas_call(
        functools.partial(_proj_gate_resid_kernel, gi=gi),
        grid=(st.batch, st.n_tok, n // tn),
        in_specs=[pl.BlockSpec((1, 1, TOK_TILE, k), lambda b, i, j: (0, b, i, 0)),
                  pl.BlockSpec((1, 1, TOK_TILE, k), lambda b, i, j: (1, b, i, 0)),
                  pl.BlockSpec((1, TOK_TILE, k), lambda b, i, j: (b, i, 0)),
                  pl.BlockSpec((1, k), lambda b, i, j: (0, 0)),
                  pl.BlockSpec((k, tn), lambda b, i, j: (0, j)),
                  pl.BlockSpec((1, TOK_TILE, tn), lambda b, i, j: (b, i, j)),
                  pl.BlockSpec((1, 6, tn), lambda b, i, j: (st.mod_row(b, i), 0, j))],
        out_specs=pl.BlockSpec((1, TOK_TILE, tn), lambda b, i, j: (b, i, j)),
        out_shape=jax.ShapeDtypeStruct((st.batch, st.s_pad, n), F32),
        scratch_shapes=[pltpu.VMEM((TOK_TILE, k), BF16)],
        compiler_params=_cparams(("parallel", "parallel", "arbitrary")),
        name="ssd_out_proj",
    )(y2, y2, zx, g.reshape(1, k), w, x, modtab)


def _proj_plain_resid(st, h, w, x, modtab, gi):
    k, n = w.shape
    tn = n
    return pl.pallas_call(
        functools.partial(_proj_plain_resid_kernel, gi=gi),
        grid=(st.batch, st.n_tok, n // tn),
        in_specs=[pl.BlockSpec((1, TOK_TILE, k), lambda b, i, j: (b, i, 0)),
                  pl.BlockSpec((k, tn), lambda b, i, j: (0, j)),
                  pl.BlockSpec((1, TOK_TILE, tn), lambda b, i, j: (b, i, j)),
                  pl.BlockSpec((1, 6, tn), lambda b, i, j: (st.mod_row(b, i), 0, j))],
        out_specs=pl.BlockSpec((1, TOK_TILE, tn), lambda b, i, j: (b, i, j)),
        out_shape=jax.ShapeDtypeStruct((st.batch, st.s_pad, n), F32),
        compiler_params=_cparams(("parallel", "parallel", "arbitrary")),
        name="attn_out_proj",
    )(h, w, x, modtab)


def _proj_scores(st, x, g, modtab, w, keys, sh, sc):
    d, n = w.shape
    tn = 2 * PEER_NKEYS
    return pl.pallas_call(
        functools.partial(_proj_scores_kernel, sh=sh, sc=sc),
        grid=(st.batch, st.n_tok, PEER_HEADS),
        in_specs=_mod_specs(st, d) + [
            pl.BlockSpec((d, tn), lambda b, i, j: (0, j)),
            pl.BlockSpec((1, 2, PEER_NKEYS, PEER_NKEYS), lambda b, i, j: (j, 0, 0, 0))],
        out_specs=[pl.BlockSpec((1, 1, 2, PEER_NKEYS, TOK_TILE), lambda b, i, j: (b, j, 0, 0, i)),
                   pl.BlockSpec((1, TOK_TILE, d), lambda b, i, j: (b, i, 0))],
        out_shape=[jax.ShapeDtypeStruct((st.batch, PEER_HEADS, 2, PEER_NKEYS, st.s_pad), F32),
                   jax.ShapeDtypeStruct((st.batch, st.s_pad, d), BF16)],
        scratch_shapes=[pltpu.VMEM((TOK_TILE, d), BF16)],
        compiler_params=_cparams(("parallel", "parallel", "arbitrary")),
        name="peer_scores",
    )(x, g.reshape(1, d), modtab, w, keys)


def _conv_kernel(x_ref, prev_ref, next_ref, w_ref, b_ref, o_ref, buf_ref, *, seg_starts, n_tiles):
    i = pl.program_id(1)
    pad = SSD_CONV_W // 2
    first = functools.reduce(jnp.logical_or, [i == s for s in seg_starts])
    last = functools.reduce(jnp.logical_or, [i == s - 1 for s in seg_starts[1:] + (n_tiles,)])
    t = x_ref.shape[1]
    buf_ref[8:8 + t, :] = x_ref[0].astype(F32)
    buf_ref[0:8, :] = jnp.where(first, 0.0, prev_ref[0].astype(F32)[8:16, :])
    buf_ref[8 + t:16 + t, :] = jnp.where(last, 0.0, next_ref[0].astype(F32)[0:8, :])
    w = w_ref[...]
    acc = b_ref[...] + w[0:1, :] * buf_ref[8 - pad:8 - pad + t, :]
    for k in range(1, SSD_CONV_W):
        acc = acc + w[k:k + 1, :] * buf_ref[8 - pad + k:8 - pad + k + t, :]
    o_ref[0] = _silu(acc).astype(o_ref.dtype)


def _conv_call(st, zx, conv_w, conv_b):
    cw = 1024
    n_cblk = SSD_CONV_DIM // cw
    off = SSD_D_INNER // cw
    n_tiles = st.s_pad // CONV_TILE
    hb = CONV_TILE // 16
    n_hblk = st.s_pad // 16
    seg_starts = (0, st.s_lat // CONV_TILE, st.s_real // CONV_TILE)
    w8 = jnp.zeros((8, SSD_CONV_DIM), F32).at[:SSD_CONV_W].set(conv_w)
    return pl.pallas_call(
        functools.partial(_conv_kernel, seg_starts=seg_starts, n_tiles=n_tiles),
        grid=(st.batch, n_tiles, n_cblk),
        in_specs=[pl.BlockSpec((1, CONV_TILE, cw), lambda b, i, c: (b, i, off + c)),
                  pl.BlockSpec((1, 16, cw), lambda b, i, c: (b, jnp.maximum(i * hb - 1, 0), off + c)),
                  pl.BlockSpec((1, 16, cw),
                               lambda b, i, c: (b, jnp.minimum((i + 1) * hb, n_hblk - 1), off + c)),
                  pl.BlockSpec((8, cw), lambda b, i, c: (0, c)),
                  pl.BlockSpec((1, cw), lambda b, i, c: (0, c))],
        out_specs=pl.BlockSpec((1, CONV_TILE, cw), lambda b, i, c: (b, i, c)),
        out_shape=jax.ShapeDtypeStruct((st.batch, st.s_pad, SSD_CONV_DIM), BF16),
        scratch_shapes=[pltpu.VMEM((CONV_TILE + 16, cw), F32)],
        compiler_params=_cparams(("parallel", "parallel", "parallel")),
        name="ssd_conv",
    )(zx, zx, zx, w8, conv_b.reshape(1, SSD_CONV_DIM))


def _scan_kernel(xs_ref, bm_ref, cm_ref, dt_ref, tri_ref, par_ref, y_ref, h_ref):
    @pl.when(pl.program_id(3) == 0)
    def _():
        h_ref[...] = jnp.zeros_like(h_ref)

    par = par_ref[0]
    raw = dt_ref[0] + par[0:1, :]
    dt = jnp.maximum(raw, 0.0) + jnp.log1p(jnp.exp(-jnp.abs(raw)))
    a = -jnp.exp(par[1:2, :])
    dsk = par[2:3, :]
    dta = dt * a
    tri = tri_ref[0]
    cs = jnp.dot(tri, dta, preferred_element_type=F32, precision=lax.Precision.HIGHEST)
    tot = jnp.sum(dta, axis=0, keepdims=True)
    e_tot = jnp.exp(tot)
    e_cs = jnp.exp(cs)
    w_end = jnp.exp(tot - cs) * dt
    cs_t = cs.T
    dt_t = dt.T
    w_end_t = w_end.T
    mask = tri > 0.5

    xs = xs_ref[0]
    bm = bm_ref[0]
    cm = cm_ref[0]
    bm_t = bm.astype(F32).T
    cb = lax.dot_general(cm, bm, NT_DIMS, preferred_element_type=F32)
    p = SSD_HEADDIM
    for r in range(SSD_HPG):
        seg = cs[:, r:r + 1] - cs_t[r:r + 1, :]
        decay = jnp.where(mask, jnp.exp(seg), 0.0)
        w = cb * decay * dt_t[r:r + 1, :]
        xh = xs[:, r * p:(r + 1) * p]
        h_prev = h_ref[r]
        y = jnp.dot(w.astype(BF16), xh, preferred_element_type=F32)
        y = y + jnp.dot(cm, h_prev.astype(BF16), preferred_element_type=F32) * e_cs[:, r:r + 1]
        y = y + dsk[:, r:r + 1] * xh.astype(F32)
        y_ref[0, 0, :, r * p:(r + 1) * p] = y.astype(y_ref.dtype)
        upd = jnp.dot((bm_t * w_end_t[r:r + 1, :]).astype(BF16), xh, preferred_element_type=F32)
        h_ref[r] = e_tot[:, r:r + 1] * h_prev + upd


def _scan_call(st, xc, dtr, par):
    q = SSD_CHUNK
    n_lat, n_ctx = st.s_lat // q, st.s_ctx // q
    n_real = n_lat + n_ctx
    n_all = st.s_pad // q
    gw = SSD_HPG * SSD_HEADDIM
    b_off = SSD_D_INNER // SSD_STATE
    c_off = b_off + SSD_GROUPS

    def chunk(d, s):
        fwd = jnp.where(s < n_ctx, n_lat + s, s - n_ctx)
        bwd = n_real - 1 - s
        return jnp.where(s < n_real, jnp.where(d == 0, fwd, bwd), s)

    idx = np.arange(q)
    tri = jnp.asarray(np.stack([idx[:, None] >= idx[None, :], idx[:, None] <= idx[None, :]]), F32)
    return pl.pallas_call(
        _scan_kernel,
        grid=(st.batch, 2, SSD_GROUPS, n_all),
        in_specs=[pl.BlockSpec((1, q, gw), lambda b, d, g, s: (b, chunk(d, s), g)),
                  pl.BlockSpec((1, q, SSD_STATE), lambda b, d, g, s: (b, chunk(d, s), b_off + g)),
                  pl.BlockSpec((1, q, SSD_STATE), lambda b, d, g, s: (b, chunk(d, s), c_off + g)),
                  pl.BlockSpec((1, q, LANE), lambda b, d, g, s: (b, chunk(d, s), d * SSD_GROUPS + g)),
                  pl.BlockSpec((1, q, q), lambda b, d, g, s: (d, 0, 0)),
                  pl.BlockSpec((1, 8, LANE), lambda b, d, g, s: (d * SSD_GROUPS + g, 0, 0))],
        out_specs=pl.BlockSpec((1, 1, q, gw), lambda b, d, g, s: (d, b, chunk(d, s), g)),
        out_shape=jax.ShapeDtypeStruct((2, st.batch, st.s_pad, SSD_D_INNER), BF16),
        scratch_shapes=[pltpu.VMEM((SSD_HPG, SSD_STATE, SSD_HEADDIM), F32)],
        compiler_params=_cparams(("parallel", "parallel", "parallel", "arbitrary")),
        name="ssd_scan",
    )(xc, xc, xc, dtr, tri, par)


def _attn_kernel(q_ref, k_ref, v_ref, lq_ref, lk_ref, g_ref, o_ref, qs_ref, vt_ref, m_ref, l_ref, acc_ref,
                 s_ref, *, s_lat, s_ctx, lambda_init):
    qi = pl.program_id(2)
    tq = q_ref.shape[1]
    tk = ATT_TK

    @pl.when(qi == 0)
    def _():
        def tr(c, carry):
            rows = v_ref[0, pl.ds(pl.multiple_of(c * tk, tk), tk), :]
            vt_ref[c] = rows.astype(F32).T.astype(BF16)
            return carry

        lax.fori_loop(0, vt_ref.shape[0], tr, 0)

    qt = q_ref[0].astype(F32).T
    row = lax.broadcasted_iota(jnp.int32, qt.shape, 0)
    qs_ref[:, 0:tq] = jnp.where(row < DA_HEAD_DIM, qt, 0.0).astype(BF16)
    qs_ref[:, tq:2 * tq] = jnp.where(row >= DA_HEAD_DIM, qt, 0.0).astype(BF16)
    m_ref[...] = jnp.full_like(m_ref, -jnp.inf)
    l_ref[...] = jnp.zeros_like(l_ref)
    acc_ref[...] = jnp.zeros_like(acc_ref)

    def scores(k):
        return jnp.dot(k, qs_ref[...], preferred_element_type=F32)

    def update(s, vt):
        m_prev = m_ref[...]
        m_new = jnp.maximum(m_prev, jnp.max(s, axis=0, keepdims=True))
        alpha = jnp.exp2(m_prev - m_new)
        p = jnp.exp2(s - m_new)
        l_ref[...] = alpha * l_ref[...] + jnp.sum(p, axis=0, keepdims=True)
        acc_ref[...] = alpha * acc_ref[...] + jnp.dot(vt, p.astype(BF16), preferred_element_type=F32)
        m_ref[...] = m_new

    n_lat = s_lat // tk

    def k_chunk(c):
        return k_ref[0, pl.ds(pl.multiple_of(c * tk, tk), tk), :]

    @pl.when(qi < s_lat // tq)
    def _():
        s_ref[...] = scores(k_chunk(0))

        def body(c, carry):
            s_cur = s_ref[...]
            s_nxt = scores(k_chunk(jnp.minimum(c + 1, n_lat - 1)))
            update(s_cur, vt_ref[c])
            s_ref[...] = s_nxt
            return carry

        lax.fori_loop(0, n_lat, body, 0)

    update(scores(k_ref[0, s_lat:s_lat + s_ctx, :]), vt_ref[n_lat, :, 0:s_ctx])

    lam = (jnp.exp(jnp.sum(lq_ref[0:1, :] * lk_ref[0:1, :], axis=-1, keepdims=True))
           - jnp.exp(jnp.sum(lq_ref[1:2, :] * lk_ref[1:2, :], axis=-1, keepdims=True)) + lambda_init)
    o_all = acc_ref[...] / l_ref[...]
    o = (o_all[:, 0:tq] - lam * o_all[:, tq:2 * tq]).T
    o_ref[0] = (_rms(o, g_ref[...]) * (1.0 - lambda_init)).astype(o_ref.dtype)


def _attn_call(st, qkv, lam_q, lam_k, subln_g, lambda_init):
    tq = ATT_TQ
    nq = st.s_pad // tq
    hw = 2 * DA_HEAD_DIM
    nh = DA_HEADS
    return pl.pallas_call(
        functools.partial(_attn_kernel, s_lat=st.s_lat, s_ctx=st.s_ctx, lambda_init=lambda_init),
        grid=(st.batch, nh, nq),
        in_specs=[pl.BlockSpec((1, tq, hw), lambda b, h, i: (b, i, h)),
                  pl.BlockSpec((1, st.s_pad, hw), lambda b, h, i: (b, 0, nh + h)),
                  pl.BlockSpec((1, st.s_pad, hw), lambda b, h, i: (b, 0, 2 * nh + h)),
                  pl.BlockSpec((2, DA_HEAD_DIM), lambda b, h, i: (0, 0)),
                  pl.BlockSpec((2, DA_HEAD_DIM), lambda b, h, i: (0, 0)),
                  pl.BlockSpec((1, hw), lambda b, h, i: (0, 0))],
        out_specs=pl.BlockSpec((1, tq, hw), lambda b, h, i: (b, i, h)),
        out_shape=jax.ShapeDtypeStruct((st.batch, st.s_pad, nh * hw), BF16),
        scratch_shapes=[pltpu.VMEM((hw, 2 * tq), BF16),
                        pltpu.VMEM((st.s_pad // ATT_TK, hw, ATT_TK), BF16),
                        pltpu.VMEM((1, 2 * tq), F32),
                        pltpu.VMEM((1, 2 * tq), F32),
                        pltpu.VMEM((hw, 2 * tq), F32),
                        pltpu.VMEM((ATT_TK, 2 * tq), F32)],
        compiler_params=_cparams(("parallel", "parallel", "arbitrary")),
        name="diff_attn",
    )(qkv, qkv, qkv, lam_q, lam_k, subln_g.reshape(1, hw))


def _extract_top(s, dst_ref, n):
    w = s
    for k in range(n):
        m = jnp.max(w, axis=0, keepdims=True)
        dst_ref[k:k + 1, :] = m
        w = jnp.where(w == m, -jnp.inf, w)


def _gate_kernel(s_ref, e1_ref, tau_ref, e2_ref, a_ref, b_ref):
    k = PEER_TOPK
    s1 = s_ref[0, 0, 0]
    s2 = s_ref[0, 0, 1]
    _extract_top(s1, a_ref, k)
    _extract_top(s2, b_ref, k)
    a = a_ref[...]
    b = b_ref[...]
    cand = jnp.concatenate(
        [a[0:1, :] + b] + [a[i:i + 1, :] + b[0:8, :] for i in range(1, 8)] + [a[8:16, :] + b[0:1, :]],
        axis=0)
    w = cand
    tau = None
    for it in range(k):
        tau = jnp.max(w, axis=0, keepdims=True)
        if it + 1 < k:
            w = jnp.where(w == tau, -jnp.inf, w)
    top = a[0:1, :] + b[0:1, :]
    z = jnp.sum(jnp.where(cand >= tau, jnp.exp(cand - top), 0.0), axis=0, keepdims=True)
    e1_ref[0, 0] = jnp.exp(s1 - a[0:1, :])
    e2_ref[0, 0] = jnp.exp(s2 - b[0:1, :]) / z
    tau_ref[0, 0] = tau - s1


def _gate_call(st, scores):
    nk = PEER_NKEYS
    shp = jax.ShapeDtypeStruct((st.batch, PEER_HEADS, nk, st.s_pad), F32)
    spec = pl.BlockSpec((1, 1, nk, TOK_TILE), lambda b, i, h: (b, h, 0, i))
    return pl.pallas_call(
        _gate_kernel,
        grid=(st.batch, st.n_tok, PEER_HEADS),
        in_specs=[pl.BlockSpec((1, 1, 2, nk, TOK_TILE), lambda b, i, h: (b, h, 0, 0, i))],
        out_specs=[spec, spec, spec],
        out_shape=[shp, shp, shp],
        scratch_shapes=[pltpu.VMEM((PEER_TOPK, TOK_TILE), F32), pltpu.VMEM((PEER_TOPK, TOK_TILE), F32)],
        compiler_params=_cparams(("parallel", "parallel", "parallel")),
        name="peer_gates",
    )(scores)


def _expert_kernel(hq_ref, u_ref, vt_ref, e1_ref, tau_ref, s2_ref, e2_ref, xr_ref, gate_ref, o_ref,
                   acc_ref, p_ref, *, gi):
    e = pl.program_id(2)

    @pl.when(e == 0)
    def _():
        acc_ref[...] = jnp.zeros_like(acc_ref)

    nk = PEER_NKEYS
    act = lax.dot_general(u_ref[...], hq_ref[0], NT_DIMS, preferred_element_type=F32)
    for a in range(PEER_EBLK // nk):
        g = None
        for h in range(PEER_HEADS):
            sel = jnp.where(s2_ref[0, h, 0] >= tau_ref[0, h, a:a + 1, :], e2_ref[0, h], 0.0)
            term = e1_ref[0, h, a:a + 1, :] * sel
            g = term if g is None else g + term
        x = act[a * nk:(a + 1) * nk, :]
        gelu = 0.5 * x * (1.0 + lax.erf(x * np.float32(math.sqrt(0.5))))
        p_ref[a * nk:(a + 1) * nk, :] = (g * gelu).astype(BF16)
    acc_ref[...] += jnp.dot(vt_ref[...], p_ref[...], preferred_element_type=F32)

    @pl.when(e == pl.num_programs(2) - 1)
    def _():
        o_ref[0] = xr_ref[0] + gate_ref[0][gi:gi + 1, :] * acc_ref[...].T


def _expert_call(st, hq, u, vt, e1, tau, scores, e2, x, modtab, gi):
    n_exp, d = u.shape
    nk = PEER_NKEYS
    rows = PEER_EBLK // nk
    tt = TOK_TILE
    return pl.pallas_call(
        functools.partial(_expert_kernel, gi=gi),
        grid=(st.batch, st.n_tok, n_exp // PEER_EBLK),
        in_specs=[pl.BlockSpec((1, tt, d), lambda b, i, e: (b, i, 0)),
                  pl.BlockSpec((PEER_EBLK, d), lambda b, i, e: (e, 0)),
                  pl.BlockSpec((d, PEER_EBLK), lambda b, i, e: (0, e)),
                  pl.BlockSpec((1, PEER_HEADS, rows, tt), lambda b, i, e: (b, 0, e, i)),
                  pl.BlockSpec((1, PEER_HEADS, rows, tt), lambda b, i, e: (b, 0, e, i)),
                  pl.BlockSpec((1, PEER_HEADS, 1, nk, tt), lambda b, i, e: (b, 0, 1, 0, i)),
                  pl.BlockSpec((1, PEER_HEADS, nk, tt), lambda b, i, e: (b, 0, 0, i)),
                  pl.BlockSpec((1, tt, d), lambda b, i, e: (b, i, 0)),
                  pl.BlockSpec((1, 6, d), lambda b, i, e: (st.mod_row(b, i), 0, 0))],
        out_specs=pl.BlockSpec((1, tt, d), lambda b, i, e: (b, i, 0)),
        out_shape=jax.ShapeDtypeStruct((st.batch, st.s_pad, d), F32),
        scratch_shapes=[pltpu.VMEM((d, tt), F32), pltpu.VMEM((PEER_EBLK, tt), BF16)],
        compiler_params=_cparams(("parallel", "parallel", "arbitrary")),
        name="peer_experts",
    )(hq, u, vt, e1, tau, scores, e2, x, modtab)


def _final_norm_kernel(x_ref, g_ref, o_ref):
    o_ref[0] = _rms(x_ref[0], g_ref[...])


def _final_norm(st, x, g):
    d = st.d
    return pl.pallas_call(
        _final_norm_kernel,
        grid=(st.batch, st.n_lat_tok),
        in_specs=[pl.BlockSpec((1, TOK_TILE, d), lambda b, i: (b, i, 0)),
                  pl.BlockSpec((1, d), lambda b, i: (0, 0))],
        out_specs=pl.BlockSpec((1, TOK_TILE, d), lambda b, i: (b, i, 0)),
        out_shape=jax.ShapeDtypeStruct((st.batch, st.s_lat, d), F32),
        compiler_params=_cparams(("parallel", "parallel")),
        name="final_norm",
    )(x, g.reshape(1, d))


def _rope_tables(st):
    rows = st.s_lat // GRID_W
    row = jnp.repeat(jnp.arange(rows, dtype=F32), GRID_W)
    col = jnp.tile(jnp.arange(GRID_W, dtype=F32), rows)
    half = DA_HEAD_DIM // 2
    freqs = ROPE_BASE ** (-jnp.arange(0, half, 2, dtype=F32) / half)
    ang_r = row[:, None] * freqs
    ang_c = col[:, None] * freqs
    ang = jnp.concatenate([ang_r, ang_r, ang_c, ang_c] * 2, axis=-1)
    extra = st.s_pad - st.s_lat
    cos = jnp.concatenate([jnp.cos(ang), jnp.ones((extra, LANE), F32)], axis=0)
    sin = jnp.concatenate([jnp.sin(ang), jnp.zeros((extra, LANE), F32)], axis=0)
    return cos, sin


def _qkv_weight(w):
    d = w.shape[0]
    qk = DA_HEADS * DA_HEAD_DIM * 2

    def regroup(t):
        return t.reshape(d, 2, DA_HEADS, DA_HEAD_DIM).transpose(0, 2, 1, 3).reshape(d, qk)

    return jnp.concatenate([regroup(w[:, :qk]), regroup(w[:, qk:2 * qk]), w[:, 2 * qk:]], axis=1).astype(BF16)


def _dt_weight(w_dt):
    d = w_dt.shape[0]
    t = w_dt.reshape(d, 2 * SSD_GROUPS, SSD_HPG)
    return jnp.pad(t, ((0, 0), (0, 0), (0, LANE - SSD_HPG))).reshape(d, 2 * SSD_GROUPS * LANE).astype(BF16)


def _scan_params(dt_bias, a_log, d_skip):
    def lanes(t):
        return jnp.pad(t.reshape(2 * SSD_GROUPS, SSD_HPG), ((0, 0), (0, LANE - SSD_HPG)))

    rows = jnp.stack([lanes(dt_bias), lanes(a_log), lanes(d_skip)], axis=1)
    return jnp.pad(rows, ((0, 0), (0, 5), (0, 0))).astype(F32)


def kernel(x, c, ctx, c_ctx, ada_w, ada_b, norm_mix_g, norm_ffn_g, ssd_in_w, ssd_conv_w, ssd_conv_b,
           ssd_dt_bias, ssd_a_log, ssd_d, ssd_norm_g, ssd_out_w, attn_qkv_w, attn_lambda_q, attn_lambda_k,
           attn_subln_g, attn_out_w, peer_q_w, peer_keys, peer_u, peer_v, final_norm_g):
    batch, s_lat, d = x.shape
    s_ctx = ctx.shape[1]
    depth = ada_w.shape[0]
    st = _Stream(batch, s_lat, s_ctx, d)

    xs = jnp.concatenate([x, ctx, jnp.zeros((batch, st.s_pad - st.s_real, d), F32)], axis=1)

    cc = jnp.zeros((8, d), F32).at[:batch].set(c).at[batch].set(c_ctx)
    mods = _ada_call(cc, ada_w, ada_b).reshape(depth, 8, 6, d)
    cos, sin = _rope_tables(st)
    zxw = SSD_D_INNER + SSD_CONV_DIM

    for i in range(depth):
        lat = mods[i, :batch]
        con = jnp.broadcast_to(mods[i, batch], (batch, 6, d))
        modtab = jnp.stack([con, lat], axis=1).reshape(batch * 2, 6, d)
        jm = i // 2
        if i % 2 == 0:
            w_in = ssd_in_w[jm]
            zx = _proj_mod(st, xs, norm_mix_g[i], modtab, w_in[:, :zxw].astype(BF16), 0, 1, 1024, BF16,
                           "ssd_in_proj")
            dtr = _proj_mod(st, xs, norm_mix_g[i], modtab, _dt_weight(w_in[:, zxw:]), 0, 1, 1024, F32,
                            "ssd_dt_proj")
            xc = _conv_call(st, zx, ssd_conv_w[jm], ssd_conv_b[jm])
            y2 = _scan_call(st, xc, dtr, _scan_params(ssd_dt_bias[jm], ssd_a_log[jm], ssd_d[jm]))
            xs = _proj_gate_resid(st, y2, zx, ssd_norm_g[jm], ssd_out_w[jm].astype(BF16), xs, modtab, 2)
        else:
            lambda_init = 0.8 - 0.6 * math.exp(-0.3 * i)
            qkv = _proj_qkv(st, xs, norm_mix_g[i], modtab, _qkv_weight(attn_qkv_w[jm]), cos, sin, 0, 1)
            o = _attn_call(st, qkv, attn_lambda_q[jm], attn_lambda_k[jm], attn_subln_g[jm], lambda_init)
            xs = _proj_plain_resid(st, o, attn_out_w[jm].astype(BF16), xs, modtab, 2)
        scores, hq = _proj_scores(st, xs, norm_ffn_g[i], modtab, peer_q_w[i].astype(BF16), peer_keys[i], 3, 4)
        e1, tau, e2 = _gate_call(st, scores)
        xs = _expert_call(st, hq, peer_u[i].astype(BF16), peer_v[i].T.astype(BF16), e1, tau, scores, e2,
                          xs, modtab, 5)
    return _final_norm(st, xs, final_norm_g)
```

```python
import functools
import math

import numpy as np
import jax
import jax.numpy as jnp
from jax import lax
from jax.experimental import pallas as pl
from jax.experimental.pallas import tpu as pltpu

F32 = jnp.float32
BF16 = jnp.bfloat16

EPS = 1e-6
GRID_W = 64
ROPE_BASE = 10000.0

SSD_HEADDIM = 64
SSD_GROUPS = 4
SSD_HPG = 8
SSD_HEADS = SSD_GROUPS * SSD_HPG
SSD_STATE = 128
SSD_CONV_W = 5
SSD_CHUNK = 128
SSD_D_INNER = SSD_HEADS * SSD_HEADDIM
SSD_GN = SSD_GROUPS * SSD_STATE
SSD_CONV_DIM = SSD_D_INNER + 2 * SSD_GN

DA_HEADS = 8
DA_HEAD_DIM = 64

PEER_HEADS = 8
PEER_NKEYS = 128
PEER_TOPK = 16
PEER_EBLK = 1024
PEER_RB = 64

LANE = 128
TOK_TILE = 512
CONV_TILE = 256
ATT_TQ = 512
ATT_TK = 1024
ATT_TV = 512
VMEM_LIMIT = 56 * 1024 * 1024

NT_DIMS = (((1,), (1,)), ((), ()))


def _cparams(sem):
    return pltpu.CompilerParams(dimension_semantics=sem, vmem_limit_bytes=VMEM_LIMIT)


def _rms(xf, g):
    return xf * lax.rsqrt(jnp.mean(xf * xf, axis=-1, keepdims=True) + EPS) * g


def _silu(x):
    return x * (1.0 / (1.0 + jnp.exp(-x)))


def _ada_kernel(c_ref, w_ref, b_ref, o_ref):
    c = c_ref[...]
    o_ref[0] = jnp.dot(_silu(c), w_ref[0], preferred_element_type=F32,
                       precision=lax.Precision.HIGHEST) + b_ref[0]


def _ada_call(cc, ada_w, ada_b):
    depth, d, n = ada_w.shape
    tn = 1536
    return pl.pallas_call(
        _ada_kernel,
        grid=(depth, n // tn),
        in_specs=[pl.BlockSpec((8, d), lambda l, j: (0, 0)),
                  pl.BlockSpec((1, d, tn), lambda l, j: (l, 0, j)),
                  pl.BlockSpec((1, 1, tn), lambda l, j: (l, 0, j))],
        out_specs=pl.BlockSpec((1, 8, tn), lambda l, j: (l, 0, j)),
        out_shape=jax.ShapeDtypeStruct((depth, 8, n), F32),
        compiler_params=_cparams(("parallel", "parallel")),
        name="ada_mod",
    )(cc, ada_w, ada_b.reshape(depth, 1, n))


def _mod_prologue(x_ref, g_ref, mod_ref, hn_ref, sh, sc):
    m = mod_ref[0]
    hn = _rms(x_ref[0], g_ref[...]) * (1.0 + m[sc:sc + 1, :]) + m[sh:sh + 1, :]
    hn_ref[...] = hn.astype(BF16)


def _proj_mod_kernel(x_ref, g_ref, mod_ref, w_ref, o_ref, hn_ref, *, sh, sc):
    @pl.when(pl.program_id(2) == 0)
    def _():
        _mod_prologue(x_ref, g_ref, mod_ref, hn_ref, sh, sc)

    o_ref[0] = jnp.dot(hn_ref[...], w_ref[...], preferred_element_type=F32).astype(o_ref.dtype)


def _rope_block(a, cos, sin, lane):
    fwd = pltpu.roll(a, LANE - 16, axis=1)
    bwd = pltpu.roll(a, 16, axis=1)
    rot = jnp.where(lane % 32 < 16, -fwd, bwd)
    return a * cos + rot * sin


def _proj_qkv_kernel(x_ref, g_ref, mod_ref, w_ref, cos_ref, sin_ref, o_ref, hn_ref, *, sh, sc):
    j = pl.program_id(2)

    @pl.when(j == 0)
    def _():
        _mod_prologue(x_ref, g_ref, mod_ref, hn_ref, sh, sc)

    acc = jnp.dot(hn_ref[...], w_ref[...], preferred_element_type=F32)
    tn = acc.shape[1]

    @pl.when(j < 2)
    def _():
        cos = cos_ref[...]
        sin = sin_ref[...]
        lane = lax.broadcasted_iota(jnp.int32, cos.shape, 1)
        scale = jnp.where(j == 0, DA_HEAD_DIM ** -0.5 * math.log2(math.e), 1.0).astype(F32)
        for cb in range(tn // LANE):
            a = acc[:, cb * LANE:(cb + 1) * LANE]
            o_ref[0, :, cb * LANE:(cb + 1) * LANE] = (
                _rope_block(a, cos, sin, lane) * scale).astype(o_ref.dtype)

    @pl.when(j >= 2)
    def _():
        o_ref[0] = acc.astype(o_ref.dtype)


def _proj_gate_resid_kernel(yf_ref, yb_ref, z_ref, g_ref, w_ref, xr_ref, gate_ref, o_ref, hn_ref,
                            *, gi):
    @pl.when(pl.program_id(2) == 0)
    def _():
        u = (yf_ref[0, 0].astype(F32) + yb_ref[0, 0].astype(F32)) * _silu(z_ref[0].astype(F32))
        hn_ref[...] = _rms(u, g_ref[...]).astype(BF16)

    acc = jnp.dot(hn_ref[...], w_ref[...], preferred_element_type=F32)
    o_ref[0] = xr_ref[0] + gate_ref[0][gi:gi + 1, :] * acc


def _proj_plain_resid_kernel(h_ref, w_ref, xr_ref, gate_ref, o_ref, *, gi):
    acc = jnp.dot(h_ref[0], w_ref[...], preferred_element_type=F32)
    o_ref[0] = xr_ref[0] + gate_ref[0][gi:gi + 1, :] * acc


def _proj_scores_kernel(x_ref, g_ref, mod_ref, w_ref, keys_ref, s_ref, hq_ref, hn_ref, *, sh, sc):
    @pl.when(pl.program_id(2) == 0)
    def _():
        _mod_prologue(x_ref, g_ref, mod_ref, hn_ref, sh, sc)
        hq_ref[0] = hn_ref[...]

    q = jnp.dot(hn_ref[...], w_ref[...], preferred_element_type=F32)
    half = PEER_NKEYS
    for c in range(2):
        s_ref[0, 0, c] = lax.dot_general(
            keys_ref[0, c], q[:, c * half:(c + 1) * half], NT_DIMS,
            preferred_element_type=F32, precision=lax.Precision.HIGHEST)


class _Stream:
    def __init__(self, batch, s_lat, s_ctx, d):
        assert s_lat % TOK_TILE == 0 and s_ctx % CONV_TILE == 0 and s_lat % GRID_W == 0
        self.batch, self.s_lat, self.s_ctx, self.d = batch, s_lat, s_ctx, d
        self.s_real = s_lat + s_ctx
        self.s_pad = -(-self.s_real // TOK_TILE) * TOK_TILE
        self.n_tok = self.s_pad // TOK_TILE
        self.n_lat_tok = s_lat // TOK_TILE

    def mod_row(self, b, i):
        return b * 2 + jnp.where(i < self.n_lat_tok, 1, 0)


def _mod_specs(st, d):
    return [pl.BlockSpec((1, TOK_TILE, d), lambda b, i, j: (b, i, 0)),
            pl.BlockSpec((1, d), lambda b, i, j: (0, 0)),
            pl.BlockSpec((1, 6, d), lambda b, i, j: (st.mod_row(b, i), 0, 0))]


def _proj_mod(st, x, g, modtab, w, sh, sc, tn, out_dtype, name):
    d, n = w.shape
    return pl.pallas_call(
        functools.partial(_proj_mod_kernel, sh=sh, sc=sc),
        grid=(st.batch, st.n_tok, n // tn),
        in_specs=_mod_specs(st, d) + [pl.BlockSpec((d, tn), lambda b, i, j: (0, j))],
        out_specs=pl.BlockSpec((1, TOK_TILE, tn), lambda b, i, j: (b, i, j)),
        out_shape=jax.ShapeDtypeStruct((st.batch, st.s_pad, n), out_dtype),
        scratch_shapes=[pltpu.VMEM((TOK_TILE, d), BF16)],
        compiler_params=_cparams(("parallel", "parallel", "arbitrary")),
        name=name,
    )(x, g.reshape(1, d), modtab, w)


def _proj_qkv(st, x, g, modtab, w, cos, sin, sh, sc):
    d, n = w.shape
    tn = 1024
    return pl.pallas_call(
        functools.partial(_proj_qkv_kernel, sh=sh, sc=sc),
        grid=(st.batch, st.n_tok, n // tn),
        in_specs=_mod_specs(st, d) + [
            pl.BlockSpec((d, tn), lambda b, i, j: (0, j)),
            pl.BlockSpec((TOK_TILE, LANE), lambda b, i, j: (i, 0)),
            pl.BlockSpec((TOK_TILE, LANE), lambda b, i, j: (i, 0))],
        out_specs=pl.BlockSpec((1, TOK_TILE, tn), lambda b, i, j: (b, i, j)),
        out_shape=jax.ShapeDtypeStruct((st.batch, st.s_pad, n), BF16),
        scratch_shapes=[pltpu.VMEM((TOK_TILE, d), BF16)],
        compiler_params=_cparams(("parallel", "parallel", "arbitrary")),
        name="attn_qkv_rope",
    )(x, g.reshape(1, d), modtab, w, cos, sin)


def _proj_gate_resid(st, y2, zx, g, w, x, modtab, gi):
    k, n = w.shape
    tn = n
    return pl.pallas_call(
        functools.partial(_proj_gate_resid_kernel, gi=gi),
        grid=(st.batch, st.n_tok, n // tn),
        in_specs=[pl.BlockSpec((1, 1, TOK_TILE, k), lambda b, i, j: (0, b, i, 0)),
                  pl.BlockSpec((1, 1, TOK_TILE, k), lambda b, i, j: (1, b, i, 0)),
                  pl.BlockSpec((1, TOK_TILE, k), lambda b, i, j: (b, i, 0)),
                  pl.BlockSpec((1, k), lambda b, i, j: (0, 0)),
                  pl.BlockSpec((k, tn), lambda b, i, j: (0, j)),
                  pl.BlockSpec((1, TOK_TILE, tn), lambda b, i, j: (b, i, j)),
                  pl.BlockSpec((1, 6, tn), lambda b, i, j: (st.mod_row(b, i), 0, j))],
        out_specs=pl.BlockSpec((1, TOK_TILE, tn), lambda b, i, j: (b, i, j)),
        out_shape=jax.ShapeDtypeStruct((st.batch, st.s_pad, n), F32),
        scratch_shapes=[pltpu.VMEM((TOK_TILE, k), BF16)],
        compiler_params=_cparams(("parallel", "parallel", "arbitrary")),
        name="ssd_out_proj",
    )(y2, y2, zx, g.reshape(1, k), w, x, modtab)


def _proj_plain_resid(st, h, w, x, modtab, gi):
    k, n = w.shape
    tn = n
    return pl.pallas_call(
        functools.partial(_proj_plain_resid_kernel, gi=gi),
        grid=(st.batch, st.n_tok, n // tn),
        in_specs=[pl.BlockSpec((1, TOK_TILE, k), lambda b, i, j: (b, i, 0)),
                  pl.BlockSpec((k, tn), lambda b, i, j: (0, j)),
                  pl.BlockSpec((1, TOK_TILE, tn), lambda b, i, j: (b, i, j)),
                  pl.BlockSpec((1, 6, tn), lambda b, i, j: (st.mod_row(b, i), 0, j))],
        out_specs=pl.BlockSpec((1, TOK_TILE, tn), lambda b, i, j: (b, i, j)),
        out_shape=jax.ShapeDtypeStruct((st.batch, st.s_pad, n), F32),
        compiler_params=_cparams(("parallel", "parallel", "arbitrary")),
        name="attn_out_proj",
    )(h, w, x, modtab)


def _proj_scores(st, x, g, modtab, w, keys, sh, sc):
    d, n = w.shape
    tn = 2 * PEER_NKEYS
    return pl.pallas_call(
        functools.partial(_proj_scores_kernel, sh=sh, sc=sc),
        grid=(st.batch, st.n_tok, PEER_HEADS),
        in_specs=_mod_specs(st, d) + [
            pl.BlockSpec((d, tn), lambda b, i, j: (0, j)),
            pl.BlockSpec((1, 2, PEER_NKEYS, PEER_NKEYS), lambda b, i, j: (j, 0, 0, 0))],
        out_specs=[pl.BlockSpec((1, 1, 2, PEER_NKEYS, TOK_TILE), lambda b, i, j: (b, j, 0, 0, i)),
                   pl.BlockSpec((1, TOK_TILE, d), lambda b, i, j: (b, i, 0))],
        out_shape=[jax.ShapeDtypeStruct((st.batch, PEER_HEADS, 2, PEER_NKEYS, st.s_pad), F32),
                   jax.ShapeDtypeStruct((st.batch, st.s_pad, d), BF16)],
        scratch_shapes=[pltpu.VMEM((TOK_TILE, d), BF16)],
        compiler_params=_cparams(("parallel", "parallel", "arbitrary")),
        name="peer_scores",
    )(x, g.reshape(1, d), modtab, w, keys)


def _conv_kernel(x_ref, prev_ref, next_ref, w_ref, b_ref, o_ref, buf_ref, *, seg_starts, n_tiles):
    i = pl.program_id(1)
    pad = SSD_CONV_W // 2
    first = functools.reduce(jnp.logical_or, [i == s for s in seg_starts])
    last = functools.reduce(jnp.logical_or, [i == s - 1 for s in seg_starts[1:] + (n_tiles,)])
    t = x_ref.shape[1]
    buf_ref[8:8 + t, :] = x_ref[0].astype(F32)
    buf_ref[0:8, :] = jnp.where(first, 0.0, prev_ref[0].astype(F32)[8:16, :])
    buf_ref[8 + t:16 + t, :] = jnp.where(last, 0.0, next_ref[0].astype(F32)[0:8, :])
    w = w_ref[...]
    acc = b_ref[...] + w[0:1, :] * buf_ref[8 - pad:8 - pad + t, :]
    for k in range(1, SSD_CONV_W):
        acc = acc + w[k:k + 1, :] * buf_ref[8 - pad + k:8 - pad + k + t, :]
    o_ref[0] = _silu(acc).astype(o_ref.dtype)


def _conv_call(st, zx, conv_w, conv_b):
    cw = 1024
    n_cblk = SSD_CONV_DIM // cw
    off = SSD_D_INNER // cw
    n_tiles = st.s_pad // CONV_TILE
    hb = CONV_TILE // 16
    n_hblk = st.s_pad // 16
    seg_starts = (0, st.s_lat // CONV_TILE, st.s_real // CONV_TILE)
    w8 = jnp.zeros((8, SSD_CONV_DIM), F32).at[:SSD_CONV_W].set(conv_w)
    return pl.pallas_call(
        functools.partial(_conv_kernel, seg_starts=seg_starts, n_tiles=n_tiles),
        grid=(st.batch, n_tiles, n_cblk),
        in_specs=[pl.BlockSpec((1, CONV_TILE, cw), lambda b, i, c: (b, i, off + c)),
                  pl.BlockSpec((1, 16, cw), lambda b, i, c: (b, jnp.maximum(i * hb - 1, 0), off + c)),
                  pl.BlockSpec((1, 16, cw),
                               lambda b, i, c: (b, jnp.minimum((i + 1) * hb, n_hblk - 1), off + c)),
                  pl.BlockSpec((8, cw), lambda b, i, c: (0, c)),
                  pl.BlockSpec((1, cw), lambda b, i, c: (0, c))],
        out_specs=pl.BlockSpec((1, CONV_TILE, cw), lambda b, i, c: (b, i, c)),
        out_shape=jax.ShapeDtypeStruct((st.batch, st.s_pad, SSD_CONV_DIM), BF16),
        scratch_shapes=[pltpu.VMEM((CONV_TILE + 16, cw), F32)],
        compiler_params=_cparams(("parallel", "parallel", "parallel")),
        name="ssd_conv",
    )(zx, zx, zx, w8, conv_b.reshape(1, SSD_CONV_DIM))


def _scan_kernel(xs_ref, bm_ref, cm_ref, dt_ref, tri_ref, par_ref, y_ref, h_ref):
    @pl.when(pl.program_id(2) == 0)
    def _():
        h_ref[...] = jnp.zeros_like(h_ref)

    par = par_ref[0]
    raw = dt_ref[0] + par[0:1, :]
    dt = jnp.maximum(raw, 0.0) + jnp.log1p(jnp.exp(-jnp.abs(raw)))
    a = -jnp.exp(par[1:2, :])
    dsk = par[2:3, :]
    dta = dt * a
    tri = tri_ref[0]
    cs = jnp.dot(tri, dta, preferred_element_type=F32, precision=lax.Precision.HIGHEST)
    tot = jnp.sum(dta, axis=0, keepdims=True)
    e_tot = jnp.exp(tot)
    e_cs = jnp.exp(cs)
    w_end = jnp.exp(tot - cs) * dt
    cs_t = cs.T
    dt_t = dt.T
    w_end_t = w_end.T
    mask = tri > 0.5

    p = SSD_HEADDIM
    n = SSD_STATE
    for g in range(SSD_GROUPS):
        bm = bm_ref[0, :, g * n:(g + 1) * n]
        cm = cm_ref[0, :, g * n:(g + 1) * n]
        bm_t = bm.astype(F32).T
        cb = lax.dot_general(cm, bm, NT_DIMS, preferred_element_type=F32)
        for r in range(SSD_HPG):
            hd = g * SSD_HPG + r
            seg = cs[:, hd:hd + 1] - cs_t[hd:hd + 1, :]
            decay = jnp.where(mask, jnp.exp(seg), 0.0)
            w = cb * decay * dt_t[hd:hd + 1, :]
            xh = xs_ref[0, :, hd * p:(hd + 1) * p]
            h_prev = h_ref[hd]
            y = jnp.dot(w.astype(BF16), xh, preferred_element_type=F32)
            y = y + jnp.dot(cm, h_prev.astype(BF16), preferred_element_type=F32) * e_cs[:, hd:hd + 1]
            y = y + dsk[:, hd:hd + 1] * xh.astype(F32)
            y_ref[0, 0, :, hd * p:(hd + 1) * p] = y.astype(y_ref.dtype)
            upd = jnp.dot((bm_t * w_end_t[hd:hd + 1, :]).astype(BF16), xh, preferred_element_type=F32)
            h_ref[hd] = e_tot[:, hd:hd + 1] * h_prev + upd


def _scan_call(st, xc, dtr, par):
    q = SSD_CHUNK
    n_lat, n_ctx = st.s_lat // q, st.s_ctx // q
    n_real = n_lat + n_ctx
    n_all = st.s_pad // q
    b_blk = SSD_D_INNER // SSD_GN
    c_blk = b_blk + 1

    def chunk(d, s):
        fwd = jnp.where(s < n_ctx, n_lat + s, s - n_ctx)
        bwd = n_real - 1 - s
        return jnp.where(s < n_real, jnp.where(d == 0, fwd, bwd), s)

    idx = np.arange(q)
    tri = jnp.asarray(np.stack([idx[:, None] >= idx[None, :], idx[:, None] <= idx[None, :]]), F32)
    return pl.pallas_call(
        _scan_kernel,
        grid=(st.batch, 2, n_all),
        in_specs=[pl.BlockSpec((1, q, SSD_D_INNER), lambda b, d, s: (b, chunk(d, s), 0)),
                  pl.BlockSpec((1, q, SSD_GN), lambda b, d, s: (b, chunk(d, s), b_blk)),
                  pl.BlockSpec((1, q, SSD_GN), lambda b, d, s: (b, chunk(d, s), c_blk)),
                  pl.BlockSpec((1, q, LANE), lambda b, d, s: (b, chunk(d, s), d)),
                  pl.BlockSpec((1, q, q), lambda b, d, s: (d, 0, 0)),
                  pl.BlockSpec((1, 8, LANE), lambda b, d, s: (d, 0, 0))],
        out_specs=pl.BlockSpec((1, 1, q, SSD_D_INNER), lambda b, d, s: (d, b, chunk(d, s), 0)),
        out_shape=jax.ShapeDtypeStruct((2, st.batch, st.s_pad, SSD_D_INNER), BF16),
        scratch_shapes=[pltpu.VMEM((SSD_HEADS, SSD_STATE, SSD_HEADDIM), F32)],
        compiler_params=_cparams(("parallel", "parallel", "arbitrary")),
        name="ssd_scan",
    )(xc, xc, xc, dtr, tri, par)


def _attn_kernel(q_ref, k_ref, v_ref, lq_ref, lk_ref, g_ref, o_ref, qs_ref, vt_ref, m_ref, l_ref, acc_ref,
                 s_ref, *, s_lat, s_ctx, lambda_init):
    qi = pl.program_id(2)
    tq = q_ref.shape[1]
    tk = ATT_TK
    tv = vt_ref.shape[2]
    n_sub = tk // tv

    @pl.when(qi == 0)
    def _():
        def tr(c, carry):
            rows = v_ref[0, pl.ds(pl.multiple_of(c * tv, tv), tv), :]
            vt_ref[c] = rows.astype(F32).T.astype(BF16)
            return carry

        lax.fori_loop(0, vt_ref.shape[0], tr, 0)

    qt = q_ref[0].astype(F32).T
    row = lax.broadcasted_iota(jnp.int32, qt.shape, 0)
    qs_ref[:, 0:tq] = jnp.where(row < DA_HEAD_DIM, qt, 0.0).astype(BF16)
    qs_ref[:, tq:2 * tq] = jnp.where(row >= DA_HEAD_DIM, qt, 0.0).astype(BF16)
    m_ref[...] = jnp.full_like(m_ref, -jnp.inf)
    l_ref[...] = jnp.zeros_like(l_ref)
    acc_ref[...] = jnp.zeros_like(acc_ref)

    def scores(k):
        return jnp.dot(k, qs_ref[...], preferred_element_type=F32)

    def update(s, vts):
        m_prev = m_ref[...]
        m_new = jnp.maximum(m_prev, jnp.max(s, axis=0, keepdims=True))
        alpha = jnp.exp2(m_prev - m_new)
        p = jnp.exp2(s - m_new)
        l_ref[...] = alpha * l_ref[...] + jnp.sum(p, axis=0, keepdims=True)
        pb = p.astype(BF16)
        rows = pb.shape[0] // len(vts)
        pv = None
        for i, vt in enumerate(vts):
            d = jnp.dot(vt, pb[i * rows:(i + 1) * rows, :], preferred_element_type=F32)
            pv = d if pv is None else pv + d
        acc_ref[...] = alpha * acc_ref[...] + pv
        m_ref[...] = m_new

    n_lat = s_lat // tk

    def k_chunk(c):
        return k_ref[0, pl.ds(pl.multiple_of(c * tk, tk), tk), :]

    @pl.when(qi < s_lat // tq)
    def _():
        s_ref[...] = scores(k_chunk(0))

        def body(c, carry):
            s_cur = s_ref[...]
            s_nxt = scores(k_chunk(jnp.minimum(c + 1, n_lat - 1)))
            update(s_cur, [vt_ref[c * n_sub + i] for i in range(n_sub)])
            s_ref[...] = s_nxt
            return carry

        lax.fori_loop(0, n_lat, body, 0)

    update(scores(k_ref[0, s_lat:s_lat + s_ctx, :]), [vt_ref[s_lat // tv, :, 0:s_ctx]])

    lam = (jnp.exp(jnp.sum(lq_ref[0:1, :] * lk_ref[0:1, :], axis=-1, keepdims=True))
           - jnp.exp(jnp.sum(lq_ref[1:2, :] * lk_ref[1:2, :], axis=-1, keepdims=True)) + lambda_init)
    o_all = acc_ref[...] / l_ref[...]
    o = (o_all[:, 0:tq] - lam * o_all[:, tq:2 * tq]).T
    o_ref[0] = (_rms(o, g_ref[...]) * (1.0 - lambda_init)).astype(o_ref.dtype)


def _attn_call(st, qkv, lam_q, lam_k, subln_g, lambda_init):
    tq = ATT_TQ
    nq = st.s_pad // tq
    hw = 2 * DA_HEAD_DIM
    nh = DA_HEADS
    return pl.pallas_call(
        functools.partial(_attn_kernel, s_lat=st.s_lat, s_ctx=st.s_ctx, lambda_init=lambda_init),
        grid=(st.batch, nh, nq),
        in_specs=[pl.BlockSpec((1, tq, hw), lambda b, h, i: (b, i, h)),
                  pl.BlockSpec((1, st.s_pad, hw), lambda b, h, i: (b, 0, nh + h)),
                  pl.BlockSpec((1, st.s_pad, hw), lambda b, h, i: (b, 0, 2 * nh + h)),
                  pl.BlockSpec((2, DA_HEAD_DIM), lambda b, h, i: (0, 0)),
                  pl.BlockSpec((2, DA_HEAD_DIM), lambda b, h, i: (0, 0)),
                  pl.BlockSpec((1, hw), lambda b, h, i: (0, 0))],
        out_specs=pl.BlockSpec((1, tq, hw), lambda b, h, i: (b, i, h)),
        out_shape=jax.ShapeDtypeStruct((st.batch, st.s_pad, nh * hw), BF16),
        scratch_shapes=[pltpu.VMEM((hw, 2 * tq), BF16),
                        pltpu.VMEM((st.s_pad // ATT_TV, hw, ATT_TV), BF16),
                        pltpu.VMEM((1, 2 * tq), F32),
                        pltpu.VMEM((1, 2 * tq), F32),
                        pltpu.VMEM((hw, 2 * tq), F32),
                        pltpu.VMEM((ATT_TK, 2 * tq), F32)],
        compiler_params=_cparams(("parallel", "parallel", "arbitrary")),
        name="diff_attn",
    )(qkv, qkv, qkv, lam_q, lam_k, subln_g.reshape(1, hw))


def _extract_top(s, dst_ref, n):
    w = s
    for k in range(n):
        m = jnp.max(w, axis=0, keepdims=True)
        dst_ref[k:k + 1, :] = m
        w = jnp.where(w == m, -jnp.inf, w)


def _gate_kernel(s_ref, e1_ref, tau_ref, e2_ref, a_ref, b_ref):
    k = PEER_TOPK
    s1 = s_ref[0, 0, 0]
    s2 = s_ref[0, 0, 1]
    _extract_top(s1, a_ref, k)
    _extract_top(s2, b_ref, k)
    a = a_ref[...]
    b = b_ref[...]
    cand = jnp.concatenate(
        [a[0:1, :] + b] + [a[i:i + 1, :] + b[0:8, :] for i in range(1, 8)] + [a[8:16, :] + b[0:1, :]],
        axis=0)
    w = cand
    tau = None
    for it in range(k):
        tau = jnp.max(w, axis=0, keepdims=True)
        if it + 1 < k:
            w = jnp.where(w == tau, -jnp.inf, w)
    top = a[0:1, :] + b[0:1, :]
    z = jnp.sum(jnp.where(cand >= tau, jnp.exp(cand - top), 0.0), axis=0, keepdims=True)
    e1_ref[0, 0] = jnp.exp(s1 - a[0:1, :])
    e2_ref[0, 0] = jnp.exp(s2 - b[0:1, :]) / z
    tau_ref[0, 0] = tau - s1


def _gate_call(st, scores):
    nk = PEER_NKEYS
    shp = jax.ShapeDtypeStruct((st.batch, PEER_HEADS, nk, st.s_pad), F32)
    spec = pl.BlockSpec((1, 1, nk, TOK_TILE), lambda b, i, h: (b, h, 0, i))
    return pl.pallas_call(
        _gate_kernel,
        grid=(st.batch, st.n_tok, PEER_HEADS),
        in_specs=[pl.BlockSpec((1, 1, 2, nk, TOK_TILE), lambda b, i, h: (b, h, 0, 0, i))],
        out_specs=[spec, spec, spec],
        out_shape=[shp, shp, shp],
        scratch_shapes=[pltpu.VMEM((PEER_TOPK, TOK_TILE), F32), pltpu.VMEM((PEER_TOPK, TOK_TILE), F32)],
        compiler_params=_cparams(("parallel", "parallel", "parallel")),
        name="peer_gates",
    )(scores)


def _expert_kernel(hq_ref, u0_ref, un_ref, vt_ref, e1_ref, tau_ref, s2_ref, e2_ref, xr_ref, gate_ref, o_ref,
                   acc_ref, act_ref, p_ref, *, gi):
    e = pl.program_id(2)
    nk = PEER_NKEYS
    tt = hq_ref.shape[1]
    half = tt // 2
    sqrt_half = np.float32(math.sqrt(0.5))

    def activations(u_ref, slot, c0):
        act_ref[slot, :, c0:c0 + half] = lax.dot_general(
            u_ref[...], hq_ref[0, c0:c0 + half, :], NT_DIMS, preferred_element_type=F32)

    @pl.when(e == 0)
    def _():
        acc_ref[...] = jnp.zeros_like(acc_ref)
        activations(u0_ref, 0, 0)
        activations(u0_ref, 0, half)

    def gate_tile(cur, a, t0):
        cols = slice(t0, t0 + LANE)
        e1 = [e1_ref[0, h, a:a + 1, cols] for h in range(PEER_HEADS)]
        tau = [tau_ref[0, h, a:a + 1, cols] for h in range(PEER_HEADS)]
        for r0 in range(0, nk, PEER_RB):
            rows = slice(r0, r0 + PEER_RB)
            g = None
            for h in range(PEER_HEADS):
                sel = jnp.where(s2_ref[0, h, 0, rows, cols] >= tau[h], e2_ref[0, h, rows, cols], 0.0)
                term = e1[h] * sel
                g = term if g is None else g + term
            x = act_ref[cur, a * nk + r0:a * nk + r0 + PEER_RB, cols]
            gelu = 0.5 * x * (1.0 + lax.erf(x * sqrt_half))
            p_ref[a * nk + r0:a * nk + r0 + PEER_RB, cols] = (g * gelu).astype(BF16)

    def stages(cur, nxt):
        rows_per = 2 * nk
        for c0 in (0, half):
            for r0 in range(0, PEER_EBLK, rows_per):
                act_ref[nxt, r0:r0 + rows_per, c0:c0 + half] = lax.dot_general(
                    un_ref[r0:r0 + rows_per, :], hq_ref[0, c0:c0 + half, :], NT_DIMS,
                    preferred_element_type=F32)
                for a in range(r0 // nk, (r0 + rows_per) // nk):
                    for t0 in range(c0, c0 + half, LANE):
                        gate_tile(cur, a, t0)
                acc_ref[:, c0:c0 + half] += jnp.dot(vt_ref[:, r0:r0 + rows_per],
                                                    p_ref[r0:r0 + rows_per, c0:c0 + half],
                                                    preferred_element_type=F32)

    for parity in range(2):
        pl.when(e % 2 == parity)(functools.partial(stages, parity, 1 - parity))

    @pl.when(e == pl.num_programs(2) - 1)
    def _():
        o_ref[0] = xr_ref[0] + gate_ref[0][gi:gi + 1, :] * acc_ref[...].T


def _expert_call(st, hq, u, vt, e1, tau, scores, e2, x, modtab, gi):
    n_exp, d = u.shape
    nk = PEER_NKEYS
    rows = PEER_EBLK // nk
    tt = TOK_TILE
    n_blk = n_exp // PEER_EBLK
    return pl.pallas_call(
        functools.partial(_expert_kernel, gi=gi),
        grid=(st.batch, st.n_tok, n_blk),
        in_specs=[pl.BlockSpec((1, tt, d), lambda b, i, e: (b, i, 0)),
                  pl.BlockSpec((PEER_EBLK, d), lambda b, i, e: (0, 0)),
                  pl.BlockSpec((PEER_EBLK, d), lambda b, i, e: (jnp.minimum(e + 1, n_blk - 1), 0)),
                  pl.BlockSpec((d, PEER_EBLK), lambda b, i, e: (0, e)),
                  pl.BlockSpec((1, PEER_HEADS, rows, tt), lambda b, i, e: (b, 0, e, i)),
                  pl.BlockSpec((1, PEER_HEADS, rows, tt), lambda b, i, e: (b, 0, e, i)),
                  pl.BlockSpec((1, PEER_HEADS, 1, nk, tt), lambda b, i, e: (b, 0, 1, 0, i)),
                  pl.BlockSpec((1, PEER_HEADS, nk, tt), lambda b, i, e: (b, 0, 0, i)),
                  pl.BlockSpec((1, tt, d), lambda b, i, e: (b, i, 0)),
                  pl.BlockSpec((1, 6, d), lambda b, i, e: (st.mod_row(b, i), 0, 0))],
        out_specs=pl.BlockSpec((1, tt, d), lambda b, i, e: (b, i, 0)),
        out_shape=jax.ShapeDtypeStruct((st.batch, st.s_pad, d), F32),
        scratch_shapes=[pltpu.VMEM((d, tt), F32),
                        pltpu.VMEM((2, PEER_EBLK, tt), F32),
                        pltpu.VMEM((PEER_EBLK, tt), BF16)],
        compiler_params=_cparams(("parallel", "parallel", "arbitrary")),
        name="peer_experts",
    )(hq, u, u, vt, e1, tau, scores, e2, x, modtab)


def _final_norm_kernel(x_ref, g_ref, o_ref):
    o_ref[0] = _rms(x_ref[0], g_ref[...])


def _final_norm(st, x, g):
    d = st.d
    return pl.pallas_call(
        _final_norm_kernel,
        grid=(st.batch, st.n_lat_tok),
        in_specs=[pl.BlockSpec((1, TOK_TILE, d), lambda b, i: (b, i, 0)),
                  pl.BlockSpec((1, d), lambda b, i: (0, 0))],
        out_specs=pl.BlockSpec((1, TOK_TILE, d), lambda b, i: (b, i, 0)),
        out_shape=jax.ShapeDtypeStruct((st.batch, st.s_lat, d), F32),
        compiler_params=_cparams(("parallel", "parallel")),
        name="final_norm",
    )(x, g.reshape(1, d))


def _rope_tables(st):
    rows = st.s_lat // GRID_W
    row = jnp.repeat(jnp.arange(rows, dtype=F32), GRID_W)
    col = jnp.tile(jnp.arange(GRID_W, dtype=F32), rows)
    half = DA_HEAD_DIM // 2
    freqs = ROPE_BASE ** (-jnp.arange(0, half, 2, dtype=F32) / half)
    ang_r = row[:, None] * freqs
    ang_c = col[:, None] * freqs
    ang = jnp.concatenate([ang_r, ang_r, ang_c, ang_c] * 2, axis=-1)
    extra = st.s_pad - st.s_lat
    cos = jnp.concatenate([jnp.cos(ang), jnp.ones((extra, LANE), F32)], axis=0)
    sin = jnp.concatenate([jnp.sin(ang), jnp.zeros((extra, LANE), F32)], axis=0)
    return cos, sin


def _qkv_weight(w):
    d = w.shape[0]
    qk = DA_HEADS * DA_HEAD_DIM * 2

    def regroup(t):
        return t.reshape(d, 2, DA_HEADS, DA_HEAD_DIM).transpose(0, 2, 1, 3).reshape(d, qk)

    return jnp.concatenate([regroup(w[:, :qk]), regroup(w[:, qk:2 * qk]), w[:, 2 * qk:]], axis=1).astype(BF16)


def _dt_weight(w_dt):
    d = w_dt.shape[0]
    t = w_dt.reshape(d, 2, SSD_HEADS)
    return jnp.pad(t, ((0, 0), (0, 0), (0, LANE - SSD_HEADS))).reshape(d, 2 * LANE).astype(BF16)


def _scan_params(dt_bias, a_log, d_skip):
    def lanes(t):
        return jnp.pad(t.reshape(2, SSD_HEADS), ((0, 0), (0, LANE - SSD_HEADS)))

    rows = jnp.stack([lanes(dt_bias), lanes(a_log), lanes(d_skip)], axis=1)
    return jnp.pad(rows, ((0, 0), (0, 5), (0, 0))).astype(F32)


def kernel(x, c, ctx, c_ctx, ada_w, ada_b, norm_mix_g, norm_ffn_g, ssd_in_w, ssd_conv_w, ssd_conv_b,
           ssd_dt_bias, ssd_a_log, ssd_d, ssd_norm_g, ssd_out_w, attn_qkv_w, attn_lambda_q, attn_lambda_k,
           attn_subln_g, attn_out_w, peer_q_w, peer_keys, peer_u, peer_v, final_norm_g):
    batch, s_lat, d = x.shape
    s_ctx = ctx.shape[1]
    depth = ada_w.shape[0]
    st = _Stream(batch, s_lat, s_ctx, d)

    xs = jnp.concatenate([x, ctx, jnp.zeros((batch, st.s_pad - st.s_real, d), F32)], axis=1)

    cc = jnp.zeros((8, d), F32).at[:batch].set(c).at[batch].set(c_ctx)
    mods = _ada_call(cc, ada_w, ada_b).reshape(depth, 8, 6, d)
    cos, sin = _rope_tables(st)
    zxw = SSD_D_INNER + SSD_CONV_DIM

    for i in range(depth):
        lat = mods[i, :batch]
        con = jnp.broadcast_to(mods[i, batch], (batch, 6, d))
        modtab = jnp.stack([con, lat], axis=1).reshape(batch * 2, 6, d)
        jm = i // 2
        if i % 2 == 0:
            w_in = ssd_in_w[jm]
            zx = _proj_mod(st, xs, norm_mix_g[i], modtab, w_in[:, :zxw].astype(BF16), 0, 1, 1024, BF16,
                           "ssd_in_proj")
            dtr = _proj_mod(st, xs, norm_mix_g[i], modtab, _dt_weight(w_in[:, zxw:]), 0, 1, 2 * LANE, F32,
                            "ssd_dt_proj")
            xc = _conv_call(st, zx, ssd_conv_w[jm], ssd_conv_b[jm])
            y2 = _scan_call(st, xc, dtr, _scan_params(ssd_dt_bias[jm], ssd_a_log[jm], ssd_d[jm]))
            xs = _proj_gate_resid(st, y2, zx, ssd_norm_g[jm], ssd_out_w[jm].astype(BF16), xs, modtab, 2)
        else:
            lambda_init = 0.8 - 0.6 * math.exp(-0.3 * i)
            qkv = _proj_qkv(st, xs, norm_mix_g[i], modtab, _qkv_weight(attn_qkv_w[jm]), cos, sin, 0, 1)
            o = _attn_call(st, qkv, attn_lambda_q[jm], attn_lambda_k[jm], attn_subln_g[jm], lambda_init)
            xs = _proj_plain_resid(st, o, attn_out_w[jm].astype(BF16), xs, modtab, 2)
        scores, hq = _proj_scores(st, xs, norm_ffn_g[i], modtab, peer_q_w[i].astype(BF16), peer_keys[i], 3, 4)
        e1, tau, e2 = _gate_call(st, scores)
        xs = _expert_call(st, hq, peer_u[i].astype(BF16), peer_v[i].T.astype(BF16), e1, tau, scores, e2,
                          xs, modtab, 5)
    return _final_norm(st, xs, final_norm_g)
```

```python
import functools
import math

import numpy as np
import jax
import jax.numpy as jnp
from jax import lax
from jax.experimental import pallas as pl
from jax.experimental.pallas import tpu as pltpu

F32 = jnp.float32
BF16 = jnp.bfloat16

EPS = 1e-6
GRID_W = 64
ROPE_BASE = 10000.0

SSD_HEADDIM = 64
SSD_GROUPS = 4
SSD_HPG = 8
SSD_HEADS = SSD_GROUPS * SSD_HPG
SSD_STATE = 128
SSD_CONV_W = 5
SSD_CHUNK = 128
SSD_D_INNER = SSD_HEADS * SSD_HEADDIM
SSD_GN = SSD_GROUPS * SSD_STATE
SSD_CONV_DIM = SSD_D_INNER + 2 * SSD_GN

DA_HEADS = 8
DA_HEAD_DIM = 64

PEER_HEADS = 8
PEER_NKEYS = 128
PEER_TOPK = 16
PEER_EBLK = 1024
PEER_RB = 64

LANE = 128
TOK_TILE = 512
CONV_TILE = 256
ATT_TQ = 512
ATT_TK = 1024
ATT_TV = 512
VMEM_LIMIT = 56 * 1024 * 1024

NT_DIMS = (((1,), (1,)), ((), ()))


def _cparams(sem):
    return pltpu.CompilerParams(dimension_semantics=sem, vmem_limit_bytes=VMEM_LIMIT)


def _rms(xf, g):
    return xf * lax.rsqrt(jnp.mean(xf * xf, axis=-1, keepdims=True) + EPS) * g


def _silu(x):
    return x * (1.0 / (1.0 + jnp.exp(-x)))


def _ada_kernel(c_ref, w_ref, b_ref, o_ref):
    c = c_ref[...]
    o_ref[0] = jnp.dot(_silu(c), w_ref[0], preferred_element_type=F32,
                       precision=lax.Precision.HIGHEST) + b_ref[0]


def _ada_call(cc, ada_w, ada_b):
    depth, d, n = ada_w.shape
    tn = 1536
    return pl.pallas_call(
        _ada_kernel,
        grid=(depth, n // tn),
        in_specs=[pl.BlockSpec((8, d), lambda l, j: (0, 0)),
                  pl.BlockSpec((1, d, tn), lambda l, j: (l, 0, j)),
                  pl.BlockSpec((1, 1, tn), lambda l, j: (l, 0, j))],
        out_specs=pl.BlockSpec((1, 8, tn), lambda l, j: (l, 0, j)),
        out_shape=jax.ShapeDtypeStruct((depth, 8, n), F32),
        compiler_params=_cparams(("parallel", "parallel")),
        name="ada_mod",
    )(cc, ada_w, ada_b.reshape(depth, 1, n))


def _mod_prologue(x_ref, g_ref, mod_ref, hn_ref, sh, sc):
    m = mod_ref[0]
    hn = _rms(x_ref[0], g_ref[...]) * (1.0 + m[sc:sc + 1, :]) + m[sh:sh + 1, :]
    hn_ref[...] = hn.astype(BF16)


def _proj_mod_kernel(x_ref, g_ref, mod_ref, w_ref, o_ref, hn_ref, *, sh, sc):
    @pl.when(pl.program_id(2) == 0)
    def _():
        _mod_prologue(x_ref, g_ref, mod_ref, hn_ref, sh, sc)

    o_ref[0, 0] = jnp.dot(hn_ref[...], w_ref[...], preferred_element_type=F32).astype(o_ref.dtype)


def _rope_block(a, cos, sin, lane):
    fwd = pltpu.roll(a, LANE - 16, axis=1)
    bwd = pltpu.roll(a, 16, axis=1)
    rot = jnp.where(lane % 32 < 16, -fwd, bwd)
    return a * cos + rot * sin


def _proj_qkv_kernel(x_ref, g_ref, mod_ref, w_ref, cos_ref, sin_ref, o_ref, hn_ref, *, sh, sc):
    j = pl.program_id(2)

    @pl.when(j == 0)
    def _():
        _mod_prologue(x_ref, g_ref, mod_ref, hn_ref, sh, sc)

    acc = jnp.dot(hn_ref[...], w_ref[...], preferred_element_type=F32)
    tn = acc.shape[1]

    @pl.when(j < 2)
    def _():
        cos = cos_ref[...]
        sin = sin_ref[...]
        lane = lax.broadcasted_iota(jnp.int32, cos.shape, 1)
        scale = jnp.where(j == 0, DA_HEAD_DIM ** -0.5 * math.log2(math.e), 1.0).astype(F32)
        for cb in range(tn // LANE):
            a = acc[:, cb * LANE:(cb + 1) * LANE]
            o_ref[0, cb] = (_rope_block(a, cos, sin, lane) * scale).astype(o_ref.dtype)

    @pl.when(j >= 2)
    def _():
        for cb in range(tn // LANE):
            o_ref[0, cb] = acc[:, cb * LANE:(cb + 1) * LANE].astype(o_ref.dtype)


def _proj_gate_resid_kernel(yf_ref, yb_ref, z_ref, g_ref, w_ref, xr_ref, gate_ref, o_ref, hn_ref,
                            *, gi):
    @pl.when(pl.program_id(2) == 0)
    def _():
        z = jnp.concatenate([z_ref[c, 0] for c in range(z_ref.shape[0])], axis=1).astype(F32)
        u = (yf_ref[0, 0].astype(F32) + yb_ref[0, 0].astype(F32)) * _silu(z)
        hn_ref[...] = _rms(u, g_ref[...]).astype(BF16)

    acc = jnp.dot(hn_ref[...], w_ref[...], preferred_element_type=F32)
    o_ref[0] = xr_ref[0] + gate_ref[0][gi:gi + 1, :] * acc


def _proj_heads_resid_kernel(h_ref, w_ref, xr_ref, gate_ref, o_ref, *, gi):
    h = jnp.concatenate([h_ref[0, hh] for hh in range(h_ref.shape[1])], axis=1)
    acc = jnp.dot(h, w_ref[...], preferred_element_type=F32)
    o_ref[0] = xr_ref[0] + gate_ref[0][gi:gi + 1, :] * acc


def _proj_scores_kernel(x_ref, g_ref, mod_ref, w_ref, keys_ref, s_ref, hq_ref, hn_ref, *, sh, sc):
    @pl.when(pl.program_id(2) == 0)
    def _():
        _mod_prologue(x_ref, g_ref, mod_ref, hn_ref, sh, sc)
        hq_ref[0] = hn_ref[...]

    q = jnp.dot(hn_ref[...], w_ref[...], preferred_element_type=F32)
    half = PEER_NKEYS
    for c in range(2):
        s_ref[0, 0, 0, c] = lax.dot_general(
            keys_ref[0, c], q[:, c * half:(c + 1) * half], NT_DIMS,
            preferred_element_type=F32, precision=lax.Precision.HIGHEST)


class _Stream:
    def __init__(self, batch, s_lat, s_ctx, d):
        assert s_lat % TOK_TILE == 0 and s_ctx % CONV_TILE == 0 and s_lat % GRID_W == 0
        self.batch, self.s_lat, self.s_ctx, self.d = batch, s_lat, s_ctx, d
        self.s_real = s_lat + s_ctx
        self.s_pad = -(-self.s_real // TOK_TILE) * TOK_TILE
        self.n_tok = self.s_pad // TOK_TILE
        self.n_lat_tok = s_lat // TOK_TILE

    def mod_row(self, b, i):
        return b * 2 + jnp.where(i < self.n_lat_tok, 1, 0)


def _mod_specs(st, d):
    return [pl.BlockSpec((1, TOK_TILE, d), lambda b, i, j: (b, i, 0)),
            pl.BlockSpec((1, d), lambda b, i, j: (0, 0)),
            pl.BlockSpec((1, 6, d), lambda b, i, j: (st.mod_row(b, i), 0, 0))]


def _proj_mod(st, x, g, modtab, w, sh, sc, tn, out_dtype, name):
    d, n = w.shape
    return pl.pallas_call(
        functools.partial(_proj_mod_kernel, sh=sh, sc=sc),
        grid=(st.batch, st.n_tok, n // tn),
        in_specs=_mod_specs(st, d) + [pl.BlockSpec((d, tn), lambda b, i, j: (0, j))],
        out_specs=pl.BlockSpec((1, 1, TOK_TILE, tn), lambda b, i, j: (j, b, i, 0)),
        out_shape=jax.ShapeDtypeStruct((n // tn, st.batch, st.s_pad, tn), out_dtype),
        scratch_shapes=[pltpu.VMEM((TOK_TILE, d), BF16)],
        compiler_params=_cparams(("parallel", "parallel", "arbitrary")),
        name=name,
    )(x, g.reshape(1, d), modtab, w)


def _proj_qkv(st, x, g, modtab, w, cos, sin, sh, sc):
    d, n = w.shape
    tn = 1024
    return pl.pallas_call(
        functools.partial(_proj_qkv_kernel, sh=sh, sc=sc),
        grid=(st.batch, st.n_tok, n // tn),
        in_specs=_mod_specs(st, d) + [
            pl.BlockSpec((d, tn), lambda b, i, j: (0, j)),
            pl.BlockSpec((TOK_TILE, LANE), lambda b, i, j: (i, 0)),
            pl.BlockSpec((TOK_TILE, LANE), lambda b, i, j: (i, 0))],
        out_specs=pl.BlockSpec((1, tn // LANE, TOK_TILE, LANE), lambda b, i, j: (b, j, i, 0)),
        out_shape=jax.ShapeDtypeStruct((st.batch, n // LANE, st.s_pad, LANE), BF16),
        scratch_shapes=[pltpu.VMEM((TOK_TILE, d), BF16)],
        compiler_params=_cparams(("parallel", "parallel", "arbitrary")),
        name="attn_qkv_rope",
    )(x, g.reshape(1, d), modtab, w, cos, sin)


def _proj_gate_resid(st, y2, zx, g, w, x, modtab, gi):
    k, n = w.shape
    tn = n
    return pl.pallas_call(
        functools.partial(_proj_gate_resid_kernel, gi=gi),
        grid=(st.batch, st.n_tok, n // tn),
        in_specs=[pl.BlockSpec((1, 1, TOK_TILE, k), lambda b, i, j: (0, b, i, 0)),
                  pl.BlockSpec((1, 1, TOK_TILE, k), lambda b, i, j: (1, b, i, 0)),
                  pl.BlockSpec((k // zx.shape[3], 1, TOK_TILE, zx.shape[3]), lambda b, i, j: (0, b, i, 0)),
                  pl.BlockSpec((1, k), lambda b, i, j: (0, 0)),
                  pl.BlockSpec((k, tn), lambda b, i, j: (0, j)),
                  pl.BlockSpec((1, TOK_TILE, tn), lambda b, i, j: (b, i, j)),
                  pl.BlockSpec((1, 6, tn), lambda b, i, j: (st.mod_row(b, i), 0, j))],
        out_specs=pl.BlockSpec((1, TOK_TILE, tn), lambda b, i, j: (b, i, j)),
        out_shape=jax.ShapeDtypeStruct((st.batch, st.s_pad, n), F32),
        scratch_shapes=[pltpu.VMEM((TOK_TILE, k), BF16)],
        compiler_params=_cparams(("parallel", "parallel", "arbitrary")),
        name="ssd_out_proj",
    )(y2, y2, zx, g.reshape(1, k), w, x, modtab)


def _proj_heads_resid(st, h, w, x, modtab, gi):
    k, n = w.shape
    tn = n
    return pl.pallas_call(
        functools.partial(_proj_heads_resid_kernel, gi=gi),
        grid=(st.batch, st.n_tok, n // tn),
        in_specs=[pl.BlockSpec((1, k // LANE, TOK_TILE, LANE), lambda b, i, j: (b, 0, i, 0)),
                  pl.BlockSpec((k, tn), lambda b, i, j: (0, j)),
                  pl.BlockSpec((1, TOK_TILE, tn), lambda b, i, j: (b, i, j)),
                  pl.BlockSpec((1, 6, tn), lambda b, i, j: (st.mod_row(b, i), 0, j))],
        out_specs=pl.BlockSpec((1, TOK_TILE, tn), lambda b, i, j: (b, i, j)),
        out_shape=jax.ShapeDtypeStruct((st.batch, st.s_pad, n), F32),
        compiler_params=_cparams(("parallel", "parallel", "arbitrary")),
        name="attn_out_proj",
    )(h, w, x, modtab)


def _proj_scores(st, x, g, modtab, w, keys, sh, sc):
    d, n = w.shape
    tn = 2 * PEER_NKEYS
    return pl.pallas_call(
        functools.partial(_proj_scores_kernel, sh=sh, sc=sc),
        grid=(st.batch, st.n_tok, PEER_HEADS),
        in_specs=_mod_specs(st, d) + [
            pl.BlockSpec((d, tn), lambda b, i, j: (0, j)),
            pl.BlockSpec((1, 2, PEER_NKEYS, PEER_NKEYS), lambda b, i, j: (j, 0, 0, 0))],
        out_specs=[pl.BlockSpec((1, 1, 1, 2, PEER_NKEYS, TOK_TILE), lambda b, i, j: (b, i, j, 0, 0, 0)),
                   pl.BlockSpec((1, TOK_TILE, d), lambda b, i, j: (b, i, 0))],
        out_shape=[jax.ShapeDtypeStruct((st.batch, st.n_tok, PEER_HEADS, 2, PEER_NKEYS, TOK_TILE), F32),
                   jax.ShapeDtypeStruct((st.batch, st.s_pad, d), BF16)],
        scratch_shapes=[pltpu.VMEM((TOK_TILE, d), BF16)],
        compiler_params=_cparams(("parallel", "parallel", "arbitrary")),
        name="peer_scores",
    )(x, g.reshape(1, d), modtab, w, keys)


def _conv_kernel(x_ref, prev_ref, next_ref, w_ref, b_ref, o_ref, buf_ref, *, seg_starts, n_tiles):
    i = pl.program_id(1)
    pad = SSD_CONV_W // 2
    first = functools.reduce(jnp.logical_or, [i == s for s in seg_starts])
    last = functools.reduce(jnp.logical_or, [i == s - 1 for s in seg_starts[1:] + (n_tiles,)])
    t = x_ref.shape[2]
    buf_ref[8:8 + t, :] = x_ref[0, 0].astype(F32)
    buf_ref[0:8, :] = jnp.where(first, 0.0, prev_ref[0, 0].astype(F32)[8:16, :])
    buf_ref[8 + t:16 + t, :] = jnp.where(last, 0.0, next_ref[0, 0].astype(F32)[0:8, :])
    w = w_ref[...]
    acc = b_ref[...] + w[0:1, :] * buf_ref[8 - pad:8 - pad + t, :]
    for k in range(1, SSD_CONV_W):
        acc = acc + w[k:k + 1, :] * buf_ref[8 - pad + k:8 - pad + k + t, :]
    o_ref[0, 0] = _silu(acc).astype(o_ref.dtype)


def _conv_call(st, zx, conv_w, conv_b):
    cw = 1024
    n_cblk = SSD_CONV_DIM // cw
    off = SSD_D_INNER // cw
    n_tiles = st.s_pad // CONV_TILE
    hb = CONV_TILE // 16
    n_hblk = st.s_pad // 16
    seg_starts = (0, st.s_lat // CONV_TILE, st.s_real // CONV_TILE)
    w8 = jnp.zeros((8, SSD_CONV_DIM), F32).at[:SSD_CONV_W].set(conv_w)
    return pl.pallas_call(
        functools.partial(_conv_kernel, seg_starts=seg_starts, n_tiles=n_tiles),
        grid=(st.batch, n_tiles, n_cblk),
        in_specs=[pl.BlockSpec((1, 1, CONV_TILE, cw), lambda b, i, c: (off + c, b, i, 0)),
                  pl.BlockSpec((1, 1, 16, cw), lambda b, i, c: (off + c, b, jnp.maximum(i * hb - 1, 0), 0)),
                  pl.BlockSpec((1, 1, 16, cw),
                               lambda b, i, c: (off + c, b, jnp.minimum((i + 1) * hb, n_hblk - 1), 0)),
                  pl.BlockSpec((8, cw), lambda b, i, c: (0, c)),
                  pl.BlockSpec((1, cw), lambda b, i, c: (0, c))],
        out_specs=pl.BlockSpec((1, 1, CONV_TILE, cw), lambda b, i, c: (c, b, i, 0)),
        out_shape=jax.ShapeDtypeStruct((n_cblk, st.batch, st.s_pad, cw), BF16),
        scratch_shapes=[pltpu.VMEM((CONV_TILE + 16, cw), F32)],
        compiler_params=_cparams(("parallel", "parallel", "parallel")),
        name="ssd_conv",
    )(zx, zx, zx, w8, conv_b.reshape(1, SSD_CONV_DIM))


def _scan_kernel(xs_ref, bc_ref, dt_ref, tri_ref, par_ref, y_ref, h_ref):
    @pl.when(pl.program_id(2) == 0)
    def _():
        h_ref[...] = jnp.zeros_like(h_ref)

    par = par_ref[0]
    raw = dt_ref[0, 0] + par[0:1, :]
    dt = jnp.maximum(raw, 0.0) + jnp.log1p(jnp.exp(-jnp.abs(raw)))
    a = -jnp.exp(par[1:2, :])
    dsk = par[2:3, :]
    dta = dt * a
    tri = tri_ref[0]
    cs = jnp.dot(tri, dta, preferred_element_type=F32, precision=lax.Precision.HIGHEST)
    tot = jnp.sum(dta, axis=0, keepdims=True)
    e_tot = jnp.exp(tot)
    e_cs = jnp.exp(cs)
    w_end = jnp.exp(tot - cs) * dt
    cs_t = cs.T
    dt_t = dt.T
    w_end_t = w_end.T
    mask = tri > 0.5

    p = SSD_HEADDIM
    n = SSD_STATE
    for g in range(SSD_GROUPS):
        bm = bc_ref[0, 0, :, g * n:(g + 1) * n]
        cm = bc_ref[0, 0, :, SSD_GN + g * n:SSD_GN + (g + 1) * n]
        bm_t = bm.astype(F32).T
        cb = lax.dot_general(cm, bm, NT_DIMS, preferred_element_type=F32)
        for r in range(SSD_HPG):
            hd = g * SSD_HPG + r
            seg = cs[:, hd:hd + 1] - cs_t[hd:hd + 1, :]
            decay = jnp.where(mask, jnp.exp(seg), 0.0)
            w = cb * decay * dt_t[hd:hd + 1, :]
            xb, xo = divmod(hd * p, xs_ref.shape[3])
            xh = xs_ref[xb, 0, :, xo:xo + p]
            h_prev = h_ref[hd]
            y = jnp.dot(w.astype(BF16), xh, preferred_element_type=F32)
            y = y + jnp.dot(cm, h_prev.astype(BF16), preferred_element_type=F32) * e_cs[:, hd:hd + 1]
            y = y + dsk[:, hd:hd + 1] * xh.astype(F32)
            y_ref[0, 0, :, hd * p:(hd + 1) * p] = y.astype(y_ref.dtype)
            upd = jnp.dot((bm_t * w_end_t[hd:hd + 1, :]).astype(BF16), xh, preferred_element_type=F32)
            h_ref[hd] = e_tot[:, hd:hd + 1] * h_prev + upd


def _scan_call(st, xc, dtr, par):
    q = SSD_CHUNK
    n_lat, n_ctx = st.s_lat // q, st.s_ctx // q
    n_real = n_lat + n_ctx
    n_all = st.s_pad // q
    cw = xc.shape[3]
    n_xblk = SSD_D_INNER // cw

    def chunk(d, s):
        fwd = jnp.where(s < n_ctx, n_lat + s, s - n_ctx)
        bwd = n_real - 1 - s
        return jnp.where(s < n_real, jnp.where(d == 0, fwd, bwd), s)

    idx = np.arange(q)
    tri = jnp.asarray(np.stack([idx[:, None] >= idx[None, :], idx[:, None] <= idx[None, :]]), F32)
    return pl.pallas_call(
        _scan_kernel,
        grid=(st.batch, 2, n_all),
        in_specs=[pl.BlockSpec((n_xblk, 1, q, cw), lambda b, d, s: (0, b, chunk(d, s), 0)),
                  pl.BlockSpec((1, 1, q, cw), lambda b, d, s: (n_xblk, b, chunk(d, s), 0)),
                  pl.BlockSpec((1, 1, q, LANE), lambda b, d, s: (d, b, chunk(d, s), 0)),
                  pl.BlockSpec((1, q, q), lambda b, d, s: (d, 0, 0)),
                  pl.BlockSpec((1, 8, LANE), lambda b, d, s: (d, 0, 0))],
        out_specs=pl.BlockSpec((1, 1, q, SSD_D_INNER), lambda b, d, s: (d, b, chunk(d, s), 0)),
        out_shape=jax.ShapeDtypeStruct((2, st.batch, st.s_pad, SSD_D_INNER), BF16),
        scratch_shapes=[pltpu.VMEM((SSD_HEADS, SSD_STATE, SSD_HEADDIM), F32)],
        compiler_params=_cparams(("parallel", "parallel", "arbitrary")),
        name="ssd_scan",
    )(xc, xc, dtr, tri, par)


def _attn_kernel(q_ref, k_ref, v_ref, lq_ref, lk_ref, g_ref, o_ref, qs_ref, vt_ref, m_ref, l_ref, acc_ref,
                 s_ref, *, s_lat, s_ctx, lambda_init):
    qi = pl.program_id(2)
    tq = q_ref.shape[2]
    tk = ATT_TK
    tv = vt_ref.shape[2]
    n_sub = tk // tv

    @pl.when(qi == 0)
    def _():
        def tr(c, carry):
            rows = v_ref[0, 0, pl.ds(pl.multiple_of(c * tv, tv), tv), :]
            vt_ref[c] = rows.astype(F32).T.astype(BF16)
            return carry

        lax.fori_loop(0, vt_ref.shape[0], tr, 0)

    qt = q_ref[0, 0].astype(F32).T
    row = lax.broadcasted_iota(jnp.int32, qt.shape, 0)
    qs_ref[:, 0:tq] = jnp.where(row < DA_HEAD_DIM, qt, 0.0).astype(BF16)
    qs_ref[:, tq:2 * tq] = jnp.where(row >= DA_HEAD_DIM, qt, 0.0).astype(BF16)
    m_ref[...] = jnp.full_like(m_ref, -jnp.inf)
    l_ref[...] = jnp.zeros_like(l_ref)
    acc_ref[...] = jnp.zeros_like(acc_ref)

    def scores(k):
        return jnp.dot(k, qs_ref[...], preferred_element_type=F32)

    def update(s, vts):
        m_prev = m_ref[...]
        m_new = jnp.maximum(m_prev, jnp.max(s, axis=0, keepdims=True))
        alpha = jnp.exp2(m_prev - m_new)
        p = jnp.exp2(s - m_new)
        l_ref[...] = alpha * l_ref[...] + jnp.sum(p, axis=0, keepdims=True)
        pb = p.astype(BF16)
        rows = pb.shape[0] // len(vts)
        pv = None
        for i, vt in enumerate(vts):
            d = jnp.dot(vt, pb[i * rows:(i + 1) * rows, :], preferred_element_type=F32)
            pv = d if pv is None else pv + d
        acc_ref[...] = alpha * acc_ref[...] + pv
        m_ref[...] = m_new

    n_lat = s_lat // tk

    def k_chunk(c):
        return k_ref[0, 0, pl.ds(pl.multiple_of(c * tk, tk), tk), :]

    @pl.when(qi < s_lat // tq)
    def _():
        s_ref[...] = scores(k_chunk(0))

        def body(c, carry):
            s_cur = s_ref[...]
            s_nxt = scores(k_chunk(jnp.minimum(c + 1, n_lat - 1)))
            update(s_cur, [vt_ref[c * n_sub + i] for i in range(n_sub)])
            s_ref[...] = s_nxt
            return carry

        lax.fori_loop(0, n_lat, body, 0)

    update(scores(k_ref[0, 0, s_lat:s_lat + s_ctx, :]), [vt_ref[s_lat // tv, :, 0:s_ctx]])

    lam = (jnp.exp(jnp.sum(lq_ref[0:1, :] * lk_ref[0:1, :], axis=-1, keepdims=True))
           - jnp.exp(jnp.sum(lq_ref[1:2, :] * lk_ref[1:2, :], axis=-1, keepdims=True)) + lambda_init)
    o_all = acc_ref[...] / l_ref[...]
    o = (o_all[:, 0:tq] - lam * o_all[:, tq:2 * tq]).T
    o_ref[0, 0] = (_rms(o, g_ref[...]) * (1.0 - lambda_init)).astype(o_ref.dtype)


def _attn_call(st, qkv, lam_q, lam_k, subln_g, lambda_init):
    tq = ATT_TQ
    nq = st.s_pad // tq
    hw = 2 * DA_HEAD_DIM
    nh = DA_HEADS
    return pl.pallas_call(
        functools.partial(_attn_kernel, s_lat=st.s_lat, s_ctx=st.s_ctx, lambda_init=lambda_init),
        grid=(st.batch, nh, nq),
        in_specs=[pl.BlockSpec((1, 1, tq, hw), lambda b, h, i: (b, h, i, 0)),
                  pl.BlockSpec((1, 1, st.s_pad, hw), lambda b, h, i: (b, nh + h, 0, 0)),
                  pl.BlockSpec((1, 1, st.s_pad, hw), lambda b, h, i: (b, 2 * nh + h, 0, 0)),
                  pl.BlockSpec((2, DA_HEAD_DIM), lambda b, h, i: (0, 0)),
                  pl.BlockSpec((2, DA_HEAD_DIM), lambda b, h, i: (0, 0)),
                  pl.BlockSpec((1, hw), lambda b, h, i: (0, 0))],
        out_specs=pl.BlockSpec((1, 1, tq, hw), lambda b, h, i: (b, h, i, 0)),
        out_shape=jax.ShapeDtypeStruct((st.batch, nh, st.s_pad, hw), BF16),
        scratch_shapes=[pltpu.VMEM((hw, 2 * tq), BF16),
                        pltpu.VMEM((st.s_pad // ATT_TV, hw, ATT_TV), BF16),
                        pltpu.VMEM((1, 2 * tq), F32),
                        pltpu.VMEM((1, 2 * tq), F32),
                        pltpu.VMEM((hw, 2 * tq), F32),
                        pltpu.VMEM((ATT_TK, 2 * tq), F32)],
        compiler_params=_cparams(("parallel", "parallel", "arbitrary")),
        name="diff_attn",
    )(qkv, qkv, qkv, lam_q, lam_k, subln_g.reshape(1, hw))


def _extract_top(s, dst_ref, n):
    w = s
    for k in range(n):
        m = jnp.max(w, axis=0, keepdims=True)
        dst_ref[k:k + 1, :] = m
        w = jnp.where(w == m, -jnp.inf, w)


def _gate_kernel(s_ref, e1_ref, tau_ref, e2_ref, a_ref, b_ref):
    k = PEER_TOPK
    s1 = s_ref[0, 0, 0, 0]
    s2 = s_ref[0, 0, 0, 1]
    _extract_top(s1, a_ref, k)
    _extract_top(s2, b_ref, k)
    a = a_ref[...]
    b = b_ref[...]
    cand = jnp.concatenate(
        [a[0:1, :] + b] + [a[i:i + 1, :] + b[0:8, :] for i in range(1, 8)] + [a[8:16, :] + b[0:1, :]],
        axis=0)
    w = cand
    tau = None
    for it in range(k):
        tau = jnp.max(w, axis=0, keepdims=True)
        if it + 1 < k:
            w = jnp.where(w == tau, -jnp.inf, w)
    top = a[0:1, :] + b[0:1, :]
    z = jnp.sum(jnp.where(cand >= tau, jnp.exp(cand - top), 0.0), axis=0, keepdims=True)
    blk_shape = e1_ref.shape[2:3] + e1_ref.shape[4:]
    e1_ref[0, 0, :, 0] = jnp.exp(s1 - a[0:1, :]).reshape(blk_shape)
    e2_ref[0, 0, 0] = jnp.exp(s2 - b[0:1, :]) / z
    tau_ref[0, 0, :, 0] = (tau - s1).reshape(blk_shape)


def _gate_call(st, scores):
    nk = PEER_NKEYS
    tt = TOK_TILE
    rows = PEER_EBLK // nk
    n_blk = nk // rows
    shp_blk = jax.ShapeDtypeStruct((st.batch, st.n_tok, n_blk, PEER_HEADS, rows, tt), F32)
    spec_blk = pl.BlockSpec((1, 1, n_blk, 1, rows, tt), lambda b, i, h: (b, i, 0, h, 0, 0))
    shp_e2 = jax.ShapeDtypeStruct((st.batch, st.n_tok, PEER_HEADS, nk, tt), F32)
    spec_e2 = pl.BlockSpec((1, 1, 1, nk, tt), lambda b, i, h: (b, i, h, 0, 0))
    return pl.pallas_call(
        _gate_kernel,
        grid=(st.batch, st.n_tok, PEER_HEADS),
        in_specs=[pl.BlockSpec((1, 1, 1, 2, nk, tt), lambda b, i, h: (b, i, h, 0, 0, 0))],
        out_specs=[spec_blk, spec_blk, spec_e2],
        out_shape=[shp_blk, shp_blk, shp_e2],
        scratch_shapes=[pltpu.VMEM((PEER_TOPK, TOK_TILE), F32), pltpu.VMEM((PEER_TOPK, TOK_TILE), F32)],
        compiler_params=_cparams(("parallel", "parallel", "parallel")),
        name="peer_gates",
    )(scores)


def _expert_kernel(hq_ref, u0_ref, un_ref, vt_ref, e1_ref, tau_ref, s2_ref, e2_ref, xr_ref, gate_ref, o_ref,
                   acc_ref, act_ref, p_ref, *, gi):
    e = pl.program_id(2)
    nk = PEER_NKEYS
    tt = hq_ref.shape[1]
    half = tt // 2
    sqrt_half = np.float32(math.sqrt(0.5))

    def activations(u_ref, slot, c0):
        act_ref[slot, :, c0:c0 + half] = lax.dot_general(
            u_ref[...], hq_ref[0, c0:c0 + half, :], NT_DIMS, preferred_element_type=F32)

    @pl.when(e == 0)
    def _():
        acc_ref[...] = jnp.zeros_like(acc_ref)
        activations(u0_ref, 0, 0)
        activations(u0_ref, 0, half)

    def gate_tile(cur, a, t0):
        cols = slice(t0, t0 + LANE)
        e1 = [e1_ref[0, 0, 0, h, a:a + 1, cols] for h in range(PEER_HEADS)]
        tau = [tau_ref[0, 0, 0, h, a:a + 1, cols] for h in range(PEER_HEADS)]
        for r0 in range(0, nk, PEER_RB):
            rows = slice(r0, r0 + PEER_RB)
            g = None
            for h in range(PEER_HEADS):
                sel = jnp.where(s2_ref[0, 0, h, 0, rows, cols] >= tau[h], e2_ref[0, 0, h, rows, cols], 0.0)
                term = e1[h] * sel
                g = term if g is None else g + term
            x = act_ref[cur, a * nk + r0:a * nk + r0 + PEER_RB, cols]
            gelu = 0.5 * x * (1.0 + lax.erf(x * sqrt_half))
            p_ref[a * nk + r0:a * nk + r0 + PEER_RB, cols] = (g * gelu).astype(BF16)

    def stages(cur, nxt):
        rows_per = 2 * nk
        act_rows = 2 * rows_per
        for c0 in (0, half):
            for r0 in range(0, PEER_EBLK, rows_per):
                if r0 % act_rows == 0:
                    act_ref[nxt, r0:r0 + act_rows, c0:c0 + half] = lax.dot_general(
                        un_ref[r0:r0 + act_rows, :], hq_ref[0, c0:c0 + half, :], NT_DIMS,
                        preferred_element_type=F32)
                for a in range(r0 // nk, (r0 + rows_per) // nk):
                    for t0 in range(c0, c0 + half, LANE):
                        gate_tile(cur, a, t0)
                acc_ref[:, c0:c0 + half] += jnp.dot(vt_ref[0, :, r0:r0 + rows_per],
                                                    p_ref[r0:r0 + rows_per, c0:c0 + half],
                                                    preferred_element_type=F32)

    for parity in range(2):
        pl.when(e % 2 == parity)(functools.partial(stages, parity, 1 - parity))

    @pl.when(e == pl.num_programs(2) - 1)
    def _():
        o_ref[0] = xr_ref[0] + gate_ref[0][gi:gi + 1, :] * acc_ref[...].T


def _expert_call(st, hq, u, vt, e1, tau, scores, e2, x, modtab, gi):
    n_exp, d = u.shape
    nk = PEER_NKEYS
    rows = PEER_EBLK // nk
    tt = TOK_TILE
    n_blk = n_exp // PEER_EBLK
    return pl.pallas_call(
        functools.partial(_expert_kernel, gi=gi),
        grid=(st.batch, st.n_tok, n_blk),
        in_specs=[pl.BlockSpec((1, tt, d), lambda b, i, e: (b, i, 0)),
                  pl.BlockSpec((PEER_EBLK, d), lambda b, i, e: (0, 0)),
                  pl.BlockSpec((PEER_EBLK, d), lambda b, i, e: (jnp.minimum(e + 1, n_blk - 1), 0)),
                  pl.BlockSpec((1, d, PEER_EBLK), lambda b, i, e: (e, 0, 0)),
                  pl.BlockSpec((1, 1, 1, PEER_HEADS, rows, tt), lambda b, i, e: (b, i, e, 0, 0, 0)),
                  pl.BlockSpec((1, 1, 1, PEER_HEADS, rows, tt), lambda b, i, e: (b, i, e, 0, 0, 0)),
                  pl.BlockSpec((1, 1, PEER_HEADS, 1, nk, tt), lambda b, i, e: (b, i, 0, 1, 0, 0)),
                  pl.BlockSpec((1, 1, PEER_HEADS, nk, tt), lambda b, i, e: (b, i, 0, 0, 0)),
                  pl.BlockSpec((1, tt, d), lambda b, i, e: (b, i, 0)),
                  pl.BlockSpec((1, 6, d), lambda b, i, e: (st.mod_row(b, i), 0, 0))],
        out_specs=pl.BlockSpec((1, tt, d), lambda b, i, e: (b, i, 0)),
        out_shape=jax.ShapeDtypeStruct((st.batch, st.s_pad, d), F32),
        scratch_shapes=[pltpu.VMEM((d, tt), F32),
                        pltpu.VMEM((2, PEER_EBLK, tt), F32),
                        pltpu.VMEM((PEER_EBLK, tt), BF16)],
        compiler_params=_cparams(("parallel", "parallel", "arbitrary")),
        name="peer_experts",
    )(hq, u, u, vt, e1, tau, scores, e2, x, modtab)


def _final_norm_kernel(x_ref, g_ref, o_ref):
    o_ref[0] = _rms(x_ref[0], g_ref[...])


def _final_norm(st, x, g):
    d = st.d
    return pl.pallas_call(
        _final_norm_kernel,
        grid=(st.batch, st.n_lat_tok),
        in_specs=[pl.BlockSpec((1, TOK_TILE, d), lambda b, i: (b, i, 0)),
                  pl.BlockSpec((1, d), lambda b, i: (0, 0))],
        out_specs=pl.BlockSpec((1, TOK_TILE, d), lambda b, i: (b, i, 0)),
        out_shape=jax.ShapeDtypeStruct((st.batch, st.s_lat, d), F32),
        compiler_params=_cparams(("parallel", "parallel")),
        name="final_norm",
    )(x, g.reshape(1, d))


def _rope_tables(st):
    rows = st.s_lat // GRID_W
    row = jnp.repeat(jnp.arange(rows, dtype=F32), GRID_W)
    col = jnp.tile(jnp.arange(GRID_W, dtype=F32), rows)
    half = DA_HEAD_DIM // 2
    freqs = ROPE_BASE ** (-jnp.arange(0, half, 2, dtype=F32) / half)
    ang_r = row[:, None] * freqs
    ang_c = col[:, None] * freqs
    ang = jnp.concatenate([ang_r, ang_r, ang_c, ang_c] * 2, axis=-1)
    extra = st.s_pad - st.s_lat
    cos = jnp.concatenate([jnp.cos(ang), jnp.ones((extra, LANE), F32)], axis=0)
    sin = jnp.concatenate([jnp.sin(ang), jnp.zeros((extra, LANE), F32)], axis=0)
    return cos, sin


def _qkv_weight(w):
    d = w.shape[0]
    qk = DA_HEADS * DA_HEAD_DIM * 2

    def regroup(t):
        return t.reshape(d, 2, DA_HEADS, DA_HEAD_DIM).transpose(0, 2, 1, 3).reshape(d, qk)

    return jnp.concatenate([regroup(w[:, :qk]), regroup(w[:, qk:2 * qk]), w[:, 2 * qk:]], axis=1).astype(BF16)


def _blocked_vt(v):
    n_exp, d = v.shape
    return v.reshape(n_exp // PEER_EBLK, PEER_EBLK, d).transpose(0, 2, 1).astype(BF16)


def _dt_weight(w_dt):
    d = w_dt.shape[0]
    t = w_dt.reshape(d, 2, SSD_HEADS)
    return jnp.pad(t, ((0, 0), (0, 0), (0, LANE - SSD_HEADS))).reshape(d, 2 * LANE).astype(BF16)


def _scan_params(dt_bias, a_log, d_skip):
    def lanes(t):
        return jnp.pad(t.reshape(2, SSD_HEADS), ((0, 0), (0, LANE - SSD_HEADS)))

    rows = jnp.stack([lanes(dt_bias), lanes(a_log), lanes(d_skip)], axis=1)
    return jnp.pad(rows, ((0, 0), (0, 5), (0, 0))).astype(F32)


def kernel(x, c, ctx, c_ctx, ada_w, ada_b, norm_mix_g, norm_ffn_g, ssd_in_w, ssd_conv_w, ssd_conv_b,
           ssd_dt_bias, ssd_a_log, ssd_d, ssd_norm_g, ssd_out_w, attn_qkv_w, attn_lambda_q, attn_lambda_k,
           attn_subln_g, attn_out_w, peer_q_w, peer_keys, peer_u, peer_v, final_norm_g):
    batch, s_lat, d = x.shape
    s_ctx = ctx.shape[1]
    depth = ada_w.shape[0]
    st = _Stream(batch, s_lat, s_ctx, d)

    xs = jnp.concatenate([x, ctx, jnp.zeros((batch, st.s_pad - st.s_real, d), F32)], axis=1)

    cc = jnp.zeros((8, d), F32).at[:batch].set(c).at[batch].set(c_ctx)
    mods = _ada_call(cc, ada_w, ada_b).reshape(depth, 8, 6, d)
    cos, sin = _rope_tables(st)
    zxw = SSD_D_INNER + SSD_CONV_DIM

    for i in range(depth):
        lat = mods[i, :batch]
        con = jnp.broadcast_to(mods[i, batch], (batch, 6, d))
        modtab = jnp.stack([con, lat], axis=1).reshape(batch * 2, 6, d)
        jm = i // 2
        if i % 2 == 0:
            w_in = ssd_in_w[jm]
            zx = _proj_mod(st, xs, norm_mix_g[i], modtab, w_in[:, :zxw].astype(BF16), 0, 1, 1024, BF16,
                           "ssd_in_proj")
            dtr = _proj_mod(st, xs, norm_mix_g[i], modtab, _dt_weight(w_in[:, zxw:]), 0, 1, LANE, F32,
                            "ssd_dt_proj")
            xc = _conv_call(st, zx, ssd_conv_w[jm], ssd_conv_b[jm])
            y2 = _scan_call(st, xc, dtr, _scan_params(ssd_dt_bias[jm], ssd_a_log[jm], ssd_d[jm]))
            xs = _proj_gate_resid(st, y2, zx, ssd_norm_g[jm], ssd_out_w[jm].astype(BF16), xs, modtab, 2)
        else:
            lambda_init = 0.8 - 0.6 * math.exp(-0.3 * i)
            qkv = _proj_qkv(st, xs, norm_mix_g[i], modtab, _qkv_weight(attn_qkv_w[jm]), cos, sin, 0, 1)
            o = _attn_call(st, qkv, attn_lambda_q[jm], attn_lambda_k[jm], attn_subln_g[jm], lambda_init)
            xs = _proj_heads_resid(st, o, attn_out_w[jm].astype(BF16), xs, modtab, 2)
        scores, hq = _proj_scores(st, xs, norm_ffn_g[i], modtab, peer_q_w[i].astype(BF16), peer_keys[i], 3, 4)
        e1, tau, e2 = _gate_call(st, scores)
        xs = _expert_call(st, hq, peer_u[i].astype(BF16), _blocked_vt(peer_v[i]), e1, tau, scores, e2,
                          xs, modtab, 5)
    return _final_norm(st, xs, final_norm_g)
```

```python
import functools
import math

import numpy as np
import jax
import jax.numpy as jnp
from jax import lax
from jax.experimental import pallas as pl
from jax.experimental.pallas import tpu as pltpu

F32 = jnp.float32
BF16 = jnp.bfloat16

EPS = 1e-6
GRID_W = 64
ROPE_BASE = 10000.0

SSD_HEADDIM = 64
SSD_GROUPS = 4
SSD_HPG = 8
SSD_HEADS = SSD_GROUPS * SSD_HPG
SSD_STATE = 128
SSD_CONV_W = 5
SSD_CHUNK = 128
SSD_D_INNER = SSD_HEADS * SSD_HEADDIM
SSD_GN = SSD_GROUPS * SSD_STATE
SSD_CONV_DIM = SSD_D_INNER + 2 * SSD_GN

DA_HEADS = 8
DA_HEAD_DIM = 64

PEER_HEADS = 8
PEER_NKEYS = 128
PEER_TOPK = 16
PEER_EBLK = 1024
PEER_RB = 64

LANE = 128
TOK_TILE = 512
CONV_TILE = 256
ATT_TQ = 512
ATT_TK = 1024
ATT_TV = 512
ATT_SUM_ROWS = 16
VMEM_LIMIT = 56 * 1024 * 1024

NT_DIMS = (((1,), (1,)), ((), ()))


def _cparams(sem):
    return pltpu.CompilerParams(dimension_semantics=sem, vmem_limit_bytes=VMEM_LIMIT)


def _rms(xf, g):
    return xf * lax.rsqrt(jnp.mean(xf * xf, axis=-1, keepdims=True) + EPS) * g


def _silu(x):
    return x * (1.0 / (1.0 + jnp.exp(-x)))


def _ada_kernel(c_ref, w_ref, b_ref, o_ref):
    c = c_ref[...]
    o_ref[0] = jnp.dot(_silu(c), w_ref[0], preferred_element_type=F32,
                       precision=lax.Precision.HIGHEST) + b_ref[0]


def _ada_call(cc, ada_w, ada_b):
    depth, d, n = ada_w.shape
    tn = 1536
    return pl.pallas_call(
        _ada_kernel,
        grid=(depth, n // tn),
        in_specs=[pl.BlockSpec((8, d), lambda l, j: (0, 0)),
                  pl.BlockSpec((1, d, tn), lambda l, j: (l, 0, j)),
                  pl.BlockSpec((1, 1, tn), lambda l, j: (l, 0, j))],
        out_specs=pl.BlockSpec((1, 8, tn), lambda l, j: (l, 0, j)),
        out_shape=jax.ShapeDtypeStruct((depth, 8, n), F32),
        compiler_params=_cparams(("parallel", "parallel")),
        name="ada_mod",
    )(cc, ada_w, ada_b.reshape(depth, 1, n))


def _mod_prologue(x_ref, g_ref, mod_ref, hn_ref, sh, sc):
    m = mod_ref[0]
    hn = _rms(x_ref[0], g_ref[...]) * (1.0 + m[sc:sc + 1, :]) + m[sh:sh + 1, :]
    hn_ref[...] = hn.astype(BF16)


def _proj_mod_kernel(x_ref, g_ref, mod_ref, w_ref, o_ref, hn_ref, *, sh, sc):
    @pl.when(pl.program_id(2) == 0)
    def _():
        _mod_prologue(x_ref, g_ref, mod_ref, hn_ref, sh, sc)

    o_ref[0, 0] = jnp.dot(hn_ref[...], w_ref[...], preferred_element_type=F32).astype(o_ref.dtype)


def _rope_block(a, cos, sin, lane):
    fwd = pltpu.roll(a, LANE - 16, axis=1)
    bwd = pltpu.roll(a, 16, axis=1)
    rot = jnp.where(lane % 32 < 16, -fwd, bwd)
    return a * cos + rot * sin


def _proj_qkv_kernel(x_ref, g_ref, mod_ref, w_ref, cos_ref, sin_ref, o_ref, hn_ref, *, sh, sc):
    j = pl.program_id(2)

    @pl.when(j == 0)
    def _():
        _mod_prologue(x_ref, g_ref, mod_ref, hn_ref, sh, sc)

    acc = jnp.dot(hn_ref[...], w_ref[...], preferred_element_type=F32)
    tn = acc.shape[1]

    @pl.when(j < 2)
    def _():
        cos = cos_ref[...]
        sin = sin_ref[...]
        lane = lax.broadcasted_iota(jnp.int32, cos.shape, 1)
        scale = jnp.where(j == 0, DA_HEAD_DIM ** -0.5 * math.log2(math.e), 1.0).astype(F32)
        for cb in range(tn // LANE):
            a = acc[:, cb * LANE:(cb + 1) * LANE]
            o_ref[0, cb] = (_rope_block(a, cos, sin, lane) * scale).astype(o_ref.dtype)

    @pl.when(j >= 2)
    def _():
        for cb in range(tn // LANE):
            o_ref[0, cb] = acc[:, cb * LANE:(cb + 1) * LANE].astype(o_ref.dtype)


def _proj_gate_resid_kernel(yf_ref, yb_ref, z_ref, g_ref, w_ref, xr_ref, gate_ref, o_ref, hn_ref,
                            *, gi):
    @pl.when(pl.program_id(2) == 0)
    def _():
        z = jnp.concatenate([z_ref[c, 0] for c in range(z_ref.shape[0])], axis=1).astype(F32)
        u = (yf_ref[0, 0].astype(F32) + yb_ref[0, 0].astype(F32)) * _silu(z)
        hn_ref[...] = _rms(u, g_ref[...]).astype(BF16)

    acc = jnp.dot(hn_ref[...], w_ref[...], preferred_element_type=F32)
    o_ref[0] = xr_ref[0] + gate_ref[0][gi:gi + 1, :] * acc


def _proj_heads_resid_kernel(h_ref, w_ref, xr_ref, gate_ref, o_ref, *, gi):
    h = jnp.concatenate([h_ref[0, hh] for hh in range(h_ref.shape[1])], axis=1)
    acc = jnp.dot(h, w_ref[...], preferred_element_type=F32)
    o_ref[0] = xr_ref[0] + gate_ref[0][gi:gi + 1, :] * acc


def _proj_scores_kernel(x_ref, g_ref, mod_ref, w_ref, keys_ref, s_ref, hq_ref, hn_ref, *, sh, sc):
    @pl.when(pl.program_id(2) == 0)
    def _():
        _mod_prologue(x_ref, g_ref, mod_ref, hn_ref, sh, sc)
        hq_ref[0] = hn_ref[...]

    q = jnp.dot(hn_ref[...], w_ref[...], preferred_element_type=F32)
    half = PEER_NKEYS
    for c in range(2):
        s_ref[0, 0, 0, c] = lax.dot_general(
            keys_ref[0, c], q[:, c * half:(c + 1) * half], NT_DIMS,
            preferred_element_type=F32, precision=lax.Precision.HIGHEST)


class _Stream:
    def __init__(self, batch, s_lat, s_ctx, d):
        assert s_lat % TOK_TILE == 0 and s_ctx % CONV_TILE == 0 and s_lat % GRID_W == 0
        self.batch, self.s_lat, self.s_ctx, self.d = batch, s_lat, s_ctx, d
        self.s_real = s_lat + s_ctx
        self.s_pad = -(-self.s_real // TOK_TILE) * TOK_TILE
        self.n_tok = self.s_pad // TOK_TILE
        self.n_lat_tok = s_lat // TOK_TILE

    def mod_row(self, b, i):
        return b * 2 + jnp.where(i < self.n_lat_tok, 1, 0)


def _mod_specs(st, d):
    return [pl.BlockSpec((1, TOK_TILE, d), lambda b, i, j: (b, i, 0)),
            pl.BlockSpec((1, d), lambda b, i, j: (0, 0)),
            pl.BlockSpec((1, 6, d), lambda b, i, j: (st.mod_row(b, i), 0, 0))]


def _proj_mod(st, x, g, modtab, w, sh, sc, tn, out_dtype, name):
    d, n = w.shape
    return pl.pallas_call(
        functools.partial(_proj_mod_kernel, sh=sh, sc=sc),
        grid=(st.batch, st.n_tok, n // tn),
        in_specs=_mod_specs(st, d) + [pl.BlockSpec((d, tn), lambda b, i, j: (0, j))],
        out_specs=pl.BlockSpec((1, 1, TOK_TILE, tn), lambda b, i, j: (j, b, i, 0)),
        out_shape=jax.ShapeDtypeStruct((n // tn, st.batch, st.s_pad, tn), out_dtype),
        scratch_shapes=[pltpu.VMEM((TOK_TILE, d), BF16)],
        compiler_params=_cparams(("parallel", "parallel", "arbitrary")),
        name=name,
    )(x, g.reshape(1, d), modtab, w)


def _proj_qkv(st, x, g, modtab, w, cos, sin, sh, sc):
    d, n = w.shape
    tn = 1024
    return pl.pallas_call(
        functools.partial(_proj_qkv_kernel, sh=sh, sc=sc),
        grid=(st.batch, st.n_tok, n // tn),
        in_specs=_mod_specs(st, d) + [
            pl.BlockSpec((d, tn), lambda b, i, j: (0, j)),
            pl.BlockSpec((TOK_TILE, LANE), lambda b, i, j: (i, 0)),
            pl.BlockSpec((TOK_TILE, LANE), lambda b, i, j: (i, 0))],
        out_specs=pl.BlockSpec((1, tn // LANE, TOK_TILE, LANE), lambda b, i, j: (b, j, i, 0)),
        out_shape=jax.ShapeDtypeStruct((st.batch, n // LANE, st.s_pad, LANE), BF16),
        scratch_shapes=[pltpu.VMEM((TOK_TILE, d), BF16)],
        compiler_params=_cparams(("parallel", "parallel", "arbitrary")),
        name="attn_qkv_rope",
    )(x, g.reshape(1, d), modtab, w, cos, sin)


def _proj_gate_resid(st, y2, zx, g, w, x, modtab, gi):
    k, n = w.shape
    tn = n
    return pl.pallas_call(
        functools.partial(_proj_gate_resid_kernel, gi=gi),
        grid=(st.batch, st.n_tok, n // tn),
        in_specs=[pl.BlockSpec((1, 1, TOK_TILE, k), lambda b, i, j: (0, b, i, 0)),
                  pl.BlockSpec((1, 1, TOK_TILE, k), lambda b, i, j: (1, b, i, 0)),
                  pl.BlockSpec((k // zx.shape[3], 1, TOK_TILE, zx.shape[3]), lambda b, i, j: (0, b, i, 0)),
                  pl.BlockSpec((1, k), lambda b, i, j: (0, 0)),
                  pl.BlockSpec((k, tn), lambda b, i, j: (0, j)),
                  pl.BlockSpec((1, TOK_TILE, tn), lambda b, i, j: (b, i, j)),
                  pl.BlockSpec((1, 6, tn), lambda b, i, j: (st.mod_row(b, i), 0, j))],
        out_specs=pl.BlockSpec((1, TOK_TILE, tn), lambda b, i, j: (b, i, j)),
        out_shape=jax.ShapeDtypeStruct((st.batch, st.s_pad, n), F32),
        scratch_shapes=[pltpu.VMEM((TOK_TILE, k), BF16)],
        compiler_params=_cparams(("parallel", "parallel", "arbitrary")),
        name="ssd_out_proj",
    )(y2, y2, zx, g.reshape(1, k), w, x, modtab)


def _proj_heads_resid(st, h, w, x, modtab, gi):
    k, n = w.shape
    tn = n
    return pl.pallas_call(
        functools.partial(_proj_heads_resid_kernel, gi=gi),
        grid=(st.batch, st.n_tok, n // tn),
        in_specs=[pl.BlockSpec((1, k // LANE, TOK_TILE, LANE), lambda b, i, j: (b, 0, i, 0)),
                  pl.BlockSpec((k, tn), lambda b, i, j: (0, j)),
                  pl.BlockSpec((1, TOK_TILE, tn), lambda b, i, j: (b, i, j)),
                  pl.BlockSpec((1, 6, tn), lambda b, i, j: (st.mod_row(b, i), 0, j))],
        out_specs=pl.BlockSpec((1, TOK_TILE, tn), lambda b, i, j: (b, i, j)),
        out_shape=jax.ShapeDtypeStruct((st.batch, st.s_pad, n), F32),
        compiler_params=_cparams(("parallel", "parallel", "arbitrary")),
        name="attn_out_proj",
    )(h, w, x, modtab)


def _proj_scores(st, x, g, modtab, w, keys, sh, sc):
    d, n = w.shape
    tn = 2 * PEER_NKEYS
    return pl.pallas_call(
        functools.partial(_proj_scores_kernel, sh=sh, sc=sc),
        grid=(st.batch, st.n_tok, PEER_HEADS),
        in_specs=_mod_specs(st, d) + [
            pl.BlockSpec((d, tn), lambda b, i, j: (0, j)),
            pl.BlockSpec((1, 2, PEER_NKEYS, PEER_NKEYS), lambda b, i, j: (j, 0, 0, 0))],
        out_specs=[pl.BlockSpec((1, 1, 1, 2, PEER_NKEYS, TOK_TILE), lambda b, i, j: (b, i, j, 0, 0, 0)),
                   pl.BlockSpec((1, TOK_TILE, d), lambda b, i, j: (b, i, 0))],
        out_shape=[jax.ShapeDtypeStruct((st.batch, st.n_tok, PEER_HEADS, 2, PEER_NKEYS, TOK_TILE), F32),
                   jax.ShapeDtypeStruct((st.batch, st.s_pad, d), BF16)],
        scratch_shapes=[pltpu.VMEM((TOK_TILE, d), BF16)],
        compiler_params=_cparams(("parallel", "parallel", "arbitrary")),
        name="peer_scores",
    )(x, g.reshape(1, d), modtab, w, keys)


def _conv_kernel(x_ref, prev_ref, next_ref, w_ref, b_ref, o_ref, buf_ref, *, seg_starts, n_tiles):
    i = pl.program_id(1)
    pad = SSD_CONV_W // 2
    first = functools.reduce(jnp.logical_or, [i == s for s in seg_starts])
    last = functools.reduce(jnp.logical_or, [i == s - 1 for s in seg_starts[1:] + (n_tiles,)])
    t = x_ref.shape[2]
    buf_ref[8:8 + t, :] = x_ref[0, 0].astype(F32)
    buf_ref[0:8, :] = jnp.where(first, 0.0, prev_ref[0, 0].astype(F32)[8:16, :])
    buf_ref[8 + t:16 + t, :] = jnp.where(last, 0.0, next_ref[0, 0].astype(F32)[0:8, :])
    w = w_ref[...]
    acc = b_ref[...] + w[0:1, :] * buf_ref[8 - pad:8 - pad + t, :]
    for k in range(1, SSD_CONV_W):
        acc = acc + w[k:k + 1, :] * buf_ref[8 - pad + k:8 - pad + k + t, :]
    o_ref[0, 0] = _silu(acc).astype(o_ref.dtype)


def _conv_call(st, zx, conv_w, conv_b):
    cw = 1024
    n_cblk = SSD_CONV_DIM // cw
    off = SSD_D_INNER // cw
    n_tiles = st.s_pad // CONV_TILE
    hb = CONV_TILE // 16
    n_hblk = st.s_pad // 16
    seg_starts = (0, st.s_lat // CONV_TILE, st.s_real // CONV_TILE)
    w8 = jnp.zeros((8, SSD_CONV_DIM), F32).at[:SSD_CONV_W].set(conv_w)
    return pl.pallas_call(
        functools.partial(_conv_kernel, seg_starts=seg_starts, n_tiles=n_tiles),
        grid=(st.batch, n_tiles, n_cblk),
        in_specs=[pl.BlockSpec((1, 1, CONV_TILE, cw), lambda b, i, c: (off + c, b, i, 0)),
                  pl.BlockSpec((1, 1, 16, cw), lambda b, i, c: (off + c, b, jnp.maximum(i * hb - 1, 0), 0)),
                  pl.BlockSpec((1, 1, 16, cw),
                               lambda b, i, c: (off + c, b, jnp.minimum((i + 1) * hb, n_hblk - 1), 0)),
                  pl.BlockSpec((8, cw), lambda b, i, c: (0, c)),
                  pl.BlockSpec((1, cw), lambda b, i, c: (0, c))],
        out_specs=pl.BlockSpec((1, 1, CONV_TILE, cw), lambda b, i, c: (c, b, i, 0)),
        out_shape=jax.ShapeDtypeStruct((n_cblk, st.batch, st.s_pad, cw), BF16),
        scratch_shapes=[pltpu.VMEM((CONV_TILE + 16, cw), F32)],
        compiler_params=_cparams(("parallel", "parallel", "parallel")),
        name="ssd_conv",
    )(zx, zx, zx, w8, conv_b.reshape(1, SSD_CONV_DIM))


def _scan_kernel(xs_ref, bc_ref, dt_ref, tri_ref, par_ref, y_ref, h_ref):
    @pl.when(pl.program_id(2) == 0)
    def _():
        h_ref[...] = jnp.zeros_like(h_ref)

    par = par_ref[0]
    raw = dt_ref[0, 0] + par[0:1, :]
    dt = jnp.maximum(raw, 0.0) + jnp.log1p(jnp.exp(-jnp.abs(raw)))
    a = -jnp.exp(par[1:2, :])
    dsk = par[2:3, :]
    dta = dt * a
    tri = tri_ref[0]
    cs = jnp.dot(tri, dta, preferred_element_type=F32, precision=lax.Precision.HIGHEST)
    tot = jnp.sum(dta, axis=0, keepdims=True)
    e_tot = jnp.exp(tot)
    e_cs = jnp.exp(cs)
    w_end = jnp.exp(tot - cs) * dt
    cs_t = cs.T
    dt_t = dt.T
    w_end_t = w_end.T
    mask = tri > 0.5

    p = SSD_HEADDIM
    n = SSD_STATE
    for g in range(SSD_GROUPS):
        bm = bc_ref[0, 0, :, g * n:(g + 1) * n]
        cm = bc_ref[0, 0, :, SSD_GN + g * n:SSD_GN + (g + 1) * n]
        bm_t = bm.astype(F32).T
        cb = lax.dot_general(cm, bm, NT_DIMS, preferred_element_type=F32)
        for r in range(SSD_HPG):
            hd = g * SSD_HPG + r
            seg = cs[:, hd:hd + 1] - cs_t[hd:hd + 1, :]
            decay = jnp.where(mask, jnp.exp(seg), 0.0)
            w = cb * decay * dt_t[hd:hd + 1, :]
            xb, xo = divmod(hd * p, xs_ref.shape[3])
            xh = xs_ref[xb, 0, :, xo:xo + p]
            h_prev = h_ref[hd]
            y = jnp.dot(w.astype(BF16), xh, preferred_element_type=F32)
            y = y + jnp.dot(cm, h_prev.astype(BF16), preferred_element_type=F32) * e_cs[:, hd:hd + 1]
            y = y + dsk[:, hd:hd + 1] * xh.astype(F32)
            y_ref[0, 0, :, hd * p:(hd + 1) * p] = y.astype(y_ref.dtype)
            upd = jnp.dot((bm_t * w_end_t[hd:hd + 1, :]).astype(BF16), xh, preferred_element_type=F32)
            h_ref[hd] = e_tot[:, hd:hd + 1] * h_prev + upd


def _scan_call(st, xc, dtr, par):
    q = SSD_CHUNK
    n_lat, n_ctx = st.s_lat // q, st.s_ctx // q
    n_real = n_lat + n_ctx
    n_all = st.s_pad // q
    cw = xc.shape[3]
    n_xblk = SSD_D_INNER // cw

    def chunk(d, s):
        fwd = jnp.where(s < n_ctx, n_lat + s, s - n_ctx)
        bwd = n_real - 1 - s
        return jnp.where(s < n_real, jnp.where(d == 0, fwd, bwd), s)

    idx = np.arange(q)
    tri = jnp.asarray(np.stack([idx[:, None] >= idx[None, :], idx[:, None] <= idx[None, :]]), F32)
    return pl.pallas_call(
        _scan_kernel,
        grid=(st.batch, 2, n_all),
        in_specs=[pl.BlockSpec((n_xblk, 1, q, cw), lambda b, d, s: (0, b, chunk(d, s), 0)),
                  pl.BlockSpec((1, 1, q, cw), lambda b, d, s: (n_xblk, b, chunk(d, s), 0)),
                  pl.BlockSpec((1, 1, q, LANE), lambda b, d, s: (d, b, chunk(d, s), 0)),
                  pl.BlockSpec((1, q, q), lambda b, d, s: (d, 0, 0)),
                  pl.BlockSpec((1, 8, LANE), lambda b, d, s: (d, 0, 0))],
        out_specs=pl.BlockSpec((1, 1, q, SSD_D_INNER), lambda b, d, s: (d, b, chunk(d, s), 0)),
        out_shape=jax.ShapeDtypeStruct((2, st.batch, st.s_pad, SSD_D_INNER), BF16),
        scratch_shapes=[pltpu.VMEM((SSD_HEADS, SSD_STATE, SSD_HEADDIM), F32)],
        compiler_params=_cparams(("parallel", "parallel", "arbitrary")),
        name="ssd_scan",
    )(xc, xc, dtr, tri, par)


def _attn_kernel(q_ref, k_ref, v_ref, lq_ref, lk_ref, g_ref, o_ref, qs_ref, vt_ref, m_ref, acc_ref,
                 s_ref, *, s_lat, s_ctx, lambda_init):
    qi = pl.program_id(2)
    tq = q_ref.shape[2]
    tk = ATT_TK
    tv = vt_ref.shape[2]
    n_sub = tk // tv
    hw = q_ref.shape[3]

    @pl.when(qi == 0)
    def _():
        def tr(c, carry):
            rows = v_ref[0, 0, pl.ds(pl.multiple_of(c * tv, tv), tv), :]
            vt_ref[c, 0:hw, :] = rows.astype(F32).T.astype(BF16)
            sub = lax.broadcasted_iota(jnp.int32, (vt_ref.shape[1] - hw, tv), 0)
            vt_ref[c, hw:, :] = jnp.where(sub == 0, 1.0, 0.0).astype(BF16)
            return carry

        lax.fori_loop(0, vt_ref.shape[0], tr, 0)

    qt = q_ref[0, 0].astype(F32).T
    row = lax.broadcasted_iota(jnp.int32, qt.shape, 0)
    qs_ref[:, 0:tq] = jnp.where(row < DA_HEAD_DIM, qt, 0.0).astype(BF16)
    qs_ref[:, tq:2 * tq] = jnp.where(row >= DA_HEAD_DIM, qt, 0.0).astype(BF16)
    m_ref[...] = jnp.full_like(m_ref, -jnp.inf)
    acc_ref[...] = jnp.zeros_like(acc_ref)

    def scores(k):
        return jnp.dot(k, qs_ref[...], preferred_element_type=F32)

    def update(read_s, vts):
        m_prev = m_ref[...]
        m_new = jnp.maximum(m_prev, jnp.max(read_s(), axis=0, keepdims=True))
        alpha = jnp.exp2(m_prev - m_new)
        pb = jnp.exp2(read_s() - m_new).astype(BF16)
        rows = pb.shape[0] // len(vts)
        pv = None
        for i, vt in enumerate(vts):
            d = jnp.dot(vt, pb[i * rows:(i + 1) * rows, :], preferred_element_type=F32)
            pv = d if pv is None else pv + d
        acc_ref[...] = alpha * acc_ref[...] + pv
        m_ref[...] = m_new

    n_lat = s_lat // tk

    def k_chunk(c):
        return k_ref[0, 0, pl.ds(pl.multiple_of(c * tk, tk), tk), :]

    @pl.when(qi < s_lat // tq)
    def _():
        s_ref[0] = scores(k_chunk(0))

        def half_step(c, slot):
            s_ref[1 - slot] = scores(k_chunk(jnp.minimum(c + 1, n_lat - 1)))
            update(lambda: s_ref[slot], [vt_ref[c * n_sub + i] for i in range(n_sub)])

        def body(cp, carry):
            half_step(2 * cp, 0)
            half_step(2 * cp + 1, 1)
            return carry

        lax.fori_loop(0, n_lat // 2, body, 0)
        if n_lat % 2:
            half_step(n_lat - 1, 0)

    s_ctx_keys = scores(k_ref[0, 0, s_lat:s_lat + s_ctx, :])
    update(lambda: s_ctx_keys, [vt_ref[s_lat // tv, :, 0:s_ctx]])

    lam = (jnp.exp(jnp.sum(lq_ref[0:1, :] * lk_ref[0:1, :], axis=-1, keepdims=True))
           - jnp.exp(jnp.sum(lq_ref[1:2, :] * lk_ref[1:2, :], axis=-1, keepdims=True)) + lambda_init)
    o_all = acc_ref[0:hw, :] / acc_ref[hw:hw + 1, :]
    o = (o_all[:, 0:tq] - lam * o_all[:, tq:2 * tq]).T
    o_ref[0, 0] = (_rms(o, g_ref[...]) * (1.0 - lambda_init)).astype(o_ref.dtype)


def _attn_call(st, qkv, lam_q, lam_k, subln_g, lambda_init):
    tq = ATT_TQ
    nq = st.s_pad // tq
    hw = 2 * DA_HEAD_DIM
    nh = DA_HEADS
    return pl.pallas_call(
        functools.partial(_attn_kernel, s_lat=st.s_lat, s_ctx=st.s_ctx, lambda_init=lambda_init),
        grid=(st.batch, nh, nq),
        in_specs=[pl.BlockSpec((1, 1, tq, hw), lambda b, h, i: (b, h, i, 0)),
                  pl.BlockSpec((1, 1, st.s_pad, hw), lambda b, h, i: (b, nh + h, 0, 0)),
                  pl.BlockSpec((1, 1, st.s_pad, hw), lambda b, h, i: (b, 2 * nh + h, 0, 0)),
                  pl.BlockSpec((2, DA_HEAD_DIM), lambda b, h, i: (0, 0)),
                  pl.BlockSpec((2, DA_HEAD_DIM), lambda b, h, i: (0, 0)),
                  pl.BlockSpec((1, hw), lambda b, h, i: (0, 0))],
        out_specs=pl.BlockSpec((1, 1, tq, hw), lambda b, h, i: (b, h, i, 0)),
        out_shape=jax.ShapeDtypeStruct((st.batch, nh, st.s_pad, hw), BF16),
        scratch_shapes=[pltpu.VMEM((hw, 2 * tq), BF16),
                        pltpu.VMEM((st.s_pad // ATT_TV, hw + ATT_SUM_ROWS, ATT_TV), BF16),
                        pltpu.VMEM((1, 2 * tq), F32),
                        pltpu.VMEM((hw + ATT_SUM_ROWS, 2 * tq), F32),
                        pltpu.VMEM((2, ATT_TK, 2 * tq), F32)],
        compiler_params=_cparams(("parallel", "parallel", "arbitrary")),
        name="diff_attn",
    )(qkv, qkv, qkv, lam_q, lam_k, subln_g.reshape(1, hw))


def _extract_top(s, dst_ref, n):
    w = s
    for k in range(n):
        m = jnp.max(w, axis=0, keepdims=True)
        dst_ref[k:k + 1, :] = m
        w = jnp.where(w == m, -jnp.inf, w)


def _gate_kernel(s_ref, e1_ref, tau_ref, e2_ref, a_ref, b_ref):
    k = PEER_TOPK
    s1 = s_ref[0, 0, 0, 0]
    s2 = s_ref[0, 0, 0, 1]
    _extract_top(s1, a_ref, k)
    _extract_top(s2, b_ref, k)
    a = a_ref[...]
    b = b_ref[...]
    cand = jnp.concatenate(
        [a[0:1, :] + b] + [a[i:i + 1, :] + b[0:8, :] for i in range(1, 8)] + [a[8:16, :] + b[0:1, :]],
        axis=0)
    w = cand
    tau = None
    for it in range(k):
        tau = jnp.max(w, axis=0, keepdims=True)
        if it + 1 < k:
            w = jnp.where(w == tau, -jnp.inf, w)
    top = a[0:1, :] + b[0:1, :]
    z = jnp.sum(jnp.where(cand >= tau, jnp.exp(cand - top), 0.0), axis=0, keepdims=True)
    blk_shape = e1_ref.shape[2:3] + e1_ref.shape[4:]
    e1_ref[0, 0, :, 0] = jnp.exp(s1 - a[0:1, :]).reshape(blk_shape)
    e2_ref[0, 0, 0] = jnp.exp(s2 - b[0:1, :]) / z
    tau_ref[0, 0, :, 0] = (tau - s1).reshape(blk_shape)


def _gate_call(st, scores):
    nk = PEER_NKEYS
    tt = TOK_TILE
    rows = PEER_EBLK // nk
    n_blk = nk // rows
    shp_blk = jax.ShapeDtypeStruct((st.batch, st.n_tok, n_blk, PEER_HEADS, rows, tt), F32)
    spec_blk = pl.BlockSpec((1, 1, n_blk, 1, rows, tt), lambda b, i, h: (b, i, 0, h, 0, 0))
    shp_e2 = jax.ShapeDtypeStruct((st.batch, st.n_tok, PEER_HEADS, nk, tt), F32)
    spec_e2 = pl.BlockSpec((1, 1, 1, nk, tt), lambda b, i, h: (b, i, h, 0, 0))
    return pl.pallas_call(
        _gate_kernel,
        grid=(st.batch, st.n_tok, PEER_HEADS),
        in_specs=[pl.BlockSpec((1, 1, 1, 2, nk, tt), lambda b, i, h: (b, i, h, 0, 0, 0))],
        out_specs=[spec_blk, spec_blk, spec_e2],
        out_shape=[shp_blk, shp_blk, shp_e2],
        scratch_shapes=[pltpu.VMEM((PEER_TOPK, TOK_TILE), F32), pltpu.VMEM((PEER_TOPK, TOK_TILE), F32)],
        compiler_params=_cparams(("parallel", "parallel", "parallel")),
        name="peer_gates",
    )(scores)


def _expert_kernel(hq_ref, u0_ref, un_ref, vt_ref, e1_ref, tau_ref, s2_ref, e2_ref, xr_ref, gate_ref, o_ref,
                   acc_ref, act_ref, p_ref, *, gi):
    e = pl.program_id(2)
    nk = PEER_NKEYS
    tt = hq_ref.shape[1]
    half = tt // 2
    sqrt_half = np.float32(math.sqrt(0.5))

    def activations(u_ref, slot, c0):
        act_ref[slot, :, c0:c0 + half] = lax.dot_general(
            u_ref[...], hq_ref[0, c0:c0 + half, :], NT_DIMS, preferred_element_type=F32)

    @pl.when(e == 0)
    def _():
        acc_ref[...] = jnp.zeros_like(acc_ref)
        activations(u0_ref, 0, 0)
        activations(u0_ref, 0, half)

    def gate_tile(cur, a_list, t0):
        cols = slice(t0, t0 + LANE)
        e1 = [[e1_ref[0, 0, 0, h, a:a + 1, cols] for h in range(PEER_HEADS)] for a in a_list]
        tau = [[tau_ref[0, 0, 0, h, a:a + 1, cols] for h in range(PEER_HEADS)] for a in a_list]
        for r0 in range(0, nk, PEER_RB):
            rows = slice(r0, r0 + PEER_RB)
            g = [None] * len(a_list)
            for h in range(PEER_HEADS):
                s2 = s2_ref[0, 0, h, 0, rows, cols]
                e2 = e2_ref[0, 0, h, rows, cols]
                for ai in range(len(a_list)):
                    term = e1[ai][h] * jnp.where(s2 >= tau[ai][h], e2, 0.0)
                    g[ai] = term if g[ai] is None else g[ai] + term
            for ai, a in enumerate(a_list):
                x = act_ref[cur, a * nk + r0:a * nk + r0 + PEER_RB, cols]
                gelu = 0.5 * x * (1.0 + lax.erf(x * sqrt_half))
                p_ref[a * nk + r0:a * nk + r0 + PEER_RB, cols] = (g[ai] * gelu).astype(BF16)

    def stages(cur, nxt):
        rows_per = 2 * nk
        act_rows = 2 * rows_per
        for c0 in (0, half):
            for r0 in range(0, PEER_EBLK, rows_per):
                if r0 % act_rows == 0:
                    act_ref[nxt, r0:r0 + act_rows, c0:c0 + half] = lax.dot_general(
                        un_ref[r0:r0 + act_rows, :], hq_ref[0, c0:c0 + half, :], NT_DIMS,
                        preferred_element_type=F32)
                a_list = list(range(r0 // nk, (r0 + rows_per) // nk))
                for t0 in range(c0, c0 + half, LANE):
                    gate_tile(cur, a_list, t0)
                acc_ref[:, c0:c0 + half] += jnp.dot(vt_ref[0, :, r0:r0 + rows_per],
                                                    p_ref[r0:r0 + rows_per, c0:c0 + half],
                                                    preferred_element_type=F32)

    for parity in range(2):
        pl.when(e % 2 == parity)(functools.partial(stages, parity, 1 - parity))

    @pl.when(e == pl.num_programs(2) - 1)
    def _():
        o_ref[0] = xr_ref[0] + gate_ref[0][gi:gi + 1, :] * acc_ref[...].T


def _expert_call(st, hq, u, vt, e1, tau, scores, e2, x, modtab, gi):
    n_exp, d = u.shape
    nk = PEER_NKEYS
    rows = PEER_EBLK // nk
    tt = TOK_TILE
    n_blk = n_exp // PEER_EBLK
    return pl.pallas_call(
        functools.partial(_expert_kernel, gi=gi),
        grid=(st.batch, st.n_tok, n_blk),
        in_specs=[pl.BlockSpec((1, tt, d), lambda b, i, e: (b, i, 0)),
                  pl.BlockSpec((PEER_EBLK, d), lambda b, i, e: (0, 0)),
                  pl.BlockSpec((PEER_EBLK, d), lambda b, i, e: (jnp.minimum(e + 1, n_blk - 1), 0)),
                  pl.BlockSpec((1, d, PEER_EBLK), lambda b, i, e: (e, 0, 0)),
                  pl.BlockSpec((1, 1, 1, PEER_HEADS, rows, tt), lambda b, i, e: (b, i, e, 0, 0, 0)),
                  pl.BlockSpec((1, 1, 1, PEER_HEADS, rows, tt), lambda b, i, e: (b, i, e, 0, 0, 0)),
                  pl.BlockSpec((1, 1, PEER_HEADS, 1, nk, tt), lambda b, i, e: (b, i, 0, 1, 0, 0)),
                  pl.BlockSpec((1, 1, PEER_HEADS, nk, tt), lambda b, i, e: (b, i, 0, 0, 0)),
                  pl.BlockSpec((1, tt, d), lambda b, i, e: (b, i, 0)),
                  pl.BlockSpec((1, 6, d), lambda b, i, e: (st.mod_row(b, i), 0, 0))],
        out_specs=pl.BlockSpec((1, tt, d), lambda b, i, e: (b, i, 0)),
        out_shape=jax.ShapeDtypeStruct((st.batch, st.s_pad, d), F32),
        scratch_shapes=[pltpu.VMEM((d, tt), F32),
                        pltpu.VMEM((2, PEER_EBLK, tt), F32),
                        pltpu.VMEM((PEER_EBLK, tt), BF16)],
        compiler_params=_cparams(("parallel", "parallel", "arbitrary")),
        name="peer_experts",
    )(hq, u, u, vt, e1, tau, scores, e2, x, modtab)


def _final_norm_kernel(x_ref, g_ref, o_ref):
    o_ref[0] = _rms(x_ref[0], g_ref[...])


def _final_norm(st, x, g):
    d = st.d
    return pl.pallas_call(
        _final_norm_kernel,
        grid=(st.batch, st.n_lat_tok),
        in_specs=[pl.BlockSpec((1, TOK_TILE, d), lambda b, i: (b, i, 0)),
                  pl.BlockSpec((1, d), lambda b, i: (0, 0))],
        out_specs=pl.BlockSpec((1, TOK_TILE, d), lambda b, i: (b, i, 0)),
        out_shape=jax.ShapeDtypeStruct((st.batch, st.s_lat, d), F32),
        compiler_params=_cparams(("parallel", "parallel")),
        name="final_norm",
    )(x, g.reshape(1, d))


def _rope_tables(st):
    rows = st.s_lat // GRID_W
    row = jnp.repeat(jnp.arange(rows, dtype=F32), GRID_W)
    col = jnp.tile(jnp.arange(GRID_W, dtype=F32), rows)
    half = DA_HEAD_DIM // 2
    freqs = ROPE_BASE ** (-jnp.arange(0, half, 2, dtype=F32) / half)
    ang_r = row[:, None] * freqs
    ang_c = col[:, None] * freqs
    ang = jnp.concatenate([ang_r, ang_r, ang_c, ang_c] * 2, axis=-1)
    extra = st.s_pad - st.s_lat
    cos = jnp.concatenate([jnp.cos(ang), jnp.ones((extra, LANE), F32)], axis=0)
    sin = jnp.concatenate([jnp.sin(ang), jnp.zeros((extra, LANE), F32)], axis=0)
    return cos, sin


def _qkv_weight(w):
    d = w.shape[0]
    qk = DA_HEADS * DA_HEAD_DIM * 2

    def regroup(t):
        return t.reshape(d, 2, DA_HEADS, DA_HEAD_DIM).transpose(0, 2, 1, 3).reshape(d, qk)

    return jnp.concatenate([regroup(w[:, :qk]), regroup(w[:, qk:2 * qk]), w[:, 2 * qk:]], axis=1).astype(BF16)


def _blocked_vt(v):
    n_exp, d = v.shape
    return v.reshape(n_exp // PEER_EBLK, PEER_EBLK, d).transpose(0, 2, 1).astype(BF16)


def _dt_weight(w_dt):
    d = w_dt.shape[0]
    t = w_dt.reshape(d, 2, SSD_HEADS)
    return jnp.pad(t, ((0, 0), (0, 0), (0, LANE - SSD_HEADS))).reshape(d, 2 * LANE).astype(BF16)


def _scan_params(dt_bias, a_log, d_skip):
    def lanes(t):
        return jnp.pad(t.reshape(2, SSD_HEADS), ((0, 0), (0, LANE - SSD_HEADS)))

    rows = jnp.stack([lanes(dt_bias), lanes(a_log), lanes(d_skip)], axis=1)
    return jnp.pad(rows, ((0, 0), (0, 5), (0, 0))).astype(F32)


def kernel(x, c, ctx, c_ctx, ada_w, ada_b, norm_mix_g, norm_ffn_g, ssd_in_w, ssd_conv_w, ssd_conv_b,
           ssd_dt_bias, ssd_a_log, ssd_d, ssd_norm_g, ssd_out_w, attn_qkv_w, attn_lambda_q, attn_lambda_k,
           attn_subln_g, attn_out_w, peer_q_w, peer_keys, peer_u, peer_v, final_norm_g):
    batch, s_lat, d = x.shape
    s_ctx = ctx.shape[1]
    depth = ada_w.shape[0]
    st = _Stream(batch, s_lat, s_ctx, d)

    xs = jnp.concatenate([x, ctx, jnp.zeros((batch, st.s_pad - st.s_real, d), F32)], axis=1)

    cc = jnp.zeros((8, d), F32).at[:batch].set(c).at[batch].set(c_ctx)
    mods = _ada_call(cc, ada_w, ada_b).reshape(depth, 8, 6, d)
    cos, sin = _rope_tables(st)
    zxw = SSD_D_INNER + SSD_CONV_DIM

    for i in range(depth):
        lat = mods[i, :batch]
        con = jnp.broadcast_to(mods[i, batch], (batch, 6, d))
        modtab = jnp.stack([con, lat], axis=1).reshape(batch * 2, 6, d)
        jm = i // 2
        if i % 2 == 0:
            w_in = ssd_in_w[jm]
            zx = _proj_mod(st, xs, norm_mix_g[i], modtab, w_in[:, :zxw].astype(BF16), 0, 1, 1024, BF16,
                           "ssd_in_proj")
            dtr = _proj_mod(st, xs, norm_mix_g[i], modtab, _dt_weight(w_in[:, zxw:]), 0, 1, LANE, F32,
                            "ssd_dt_proj")
            xc = _conv_call(st, zx, ssd_conv_w[jm], ssd_conv_b[jm])
            y2 = _scan_call(st, xc, dtr, _scan_params(ssd_dt_bias[jm], ssd_a_log[jm], ssd_d[jm]))
            xs = _proj_gate_resid(st, y2, zx, ssd_norm_g[jm], ssd_out_w[jm].astype(BF16), xs, modtab, 2)
        else:
            lambda_init = 0.8 - 0.6 * math.exp(-0.3 * i)
            qkv = _proj_qkv(st, xs, norm_mix_g[i], modtab, _qkv_weight(attn_qkv_w[jm]), cos, sin, 0, 1)
            o = _attn_call(st, qkv, attn_lambda_q[jm], attn_lambda_k[jm], attn_subln_g[jm], lambda_init)
            xs = _proj_heads_resid(st, o, attn_out_w[jm].astype(BF16), xs, modtab, 2)
        scores, hq = _proj_scores(st, xs, norm_ffn_g[i], modtab, peer_q_w[i].astype(BF16), peer_keys[i], 3, 4)
        e1, tau, e2 = _gate_call(st, scores)
        xs = _expert_call(st, hq, peer_u[i].astype(BF16), _blocked_vt(peer_v[i]), e1, tau, scores, e2,
                          xs, modtab, 5)
    return _final_norm(st, xs, final_norm_g)
```

```python
import functools
import math

import numpy as np
import jax
import jax.numpy as jnp
from jax import lax
from jax.experimental import pallas as pl
from jax.experimental.pallas import tpu as pltpu

F32 = jnp.float32
BF16 = jnp.bfloat16

EPS = 1e-6
GRID_W = 64
ROPE_BASE = 10000.0

SSD_HEADDIM = 64
SSD_GROUPS = 4
SSD_HPG = 8
SSD_HEADS = SSD_GROUPS * SSD_HPG
SSD_STATE = 128
SSD_CONV_W = 5
SSD_CHUNK = 128
SSD_D_INNER = SSD_HEADS * SSD_HEADDIM
SSD_GN = SSD_GROUPS * SSD_STATE
SSD_CONV_DIM = SSD_D_INNER + 2 * SSD_GN

DA_HEADS = 8
DA_HEAD_DIM = 64

PEER_HEADS = 8
PEER_NKEYS = 128
PEER_TOPK = 16
PEER_EBLK = 1024
PEER_RB = 64

LANE = 128
TOK_TILE = 512
CONV_TILE = 256
ATT_TQ = 512
ATT_TK = 1024
ATT_TV = 512
ATT_SUM_ROWS = 16
VMEM_LIMIT = 56 * 1024 * 1024

NT_DIMS = (((1,), (1,)), ((), ()))


def _cparams(sem):
    return pltpu.CompilerParams(dimension_semantics=sem, vmem_limit_bytes=VMEM_LIMIT)


def _rms(xf, g):
    return xf * lax.rsqrt(jnp.mean(xf * xf, axis=-1, keepdims=True) + EPS) * g


def _silu(x):
    return x * (1.0 / (1.0 + jnp.exp(-x)))


def _ada_kernel(c_ref, w_ref, b_ref, o_ref):
    c = c_ref[...]
    o_ref[0] = jnp.dot(_silu(c), w_ref[0], preferred_element_type=F32,
                       precision=lax.Precision.HIGHEST) + b_ref[0]


def _ada_call(cc, ada_w, ada_b):
    depth, d, n = ada_w.shape
    tn = 1536
    return pl.pallas_call(
        _ada_kernel,
        grid=(depth, n // tn),
        in_specs=[pl.BlockSpec((8, d), lambda l, j: (0, 0)),
                  pl.BlockSpec((1, d, tn), lambda l, j: (l, 0, j)),
                  pl.BlockSpec((1, 1, tn), lambda l, j: (l, 0, j))],
        out_specs=pl.BlockSpec((1, 8, tn), lambda l, j: (l, 0, j)),
        out_shape=jax.ShapeDtypeStruct((depth, 8, n), F32),
        compiler_params=_cparams(("parallel", "parallel")),
        name="ada_mod",
    )(cc, ada_w, ada_b.reshape(depth, 1, n))


def _mod_prologue(x_ref, g_ref, mod_ref, hn_ref, sh, sc):
    m = mod_ref[0]
    hn = _rms(x_ref[0], g_ref[...]) * (1.0 + m[sc:sc + 1, :]) + m[sh:sh + 1, :]
    hn_ref[...] = hn.astype(BF16)


def _proj_mod_kernel(x_ref, g_ref, mod_ref, w_ref, o_ref, hn_ref, *, sh, sc):
    @pl.when(pl.program_id(2) == 0)
    def _():
        _mod_prologue(x_ref, g_ref, mod_ref, hn_ref, sh, sc)

    o_ref[0, 0] = jnp.dot(hn_ref[...], w_ref[...], preferred_element_type=F32).astype(o_ref.dtype)


def _rope_block(a, cos, sin, lane):
    fwd = pltpu.roll(a, LANE - 16, axis=1)
    bwd = pltpu.roll(a, 16, axis=1)
    rot = jnp.where(lane % 32 < 16, -fwd, bwd)
    return a * cos + rot * sin


def _proj_qkv_kernel(x_ref, g_ref, mod_ref, w_ref, cos_ref, sin_ref, o_ref, hn_ref, *, sh, sc):
    j = pl.program_id(2)

    @pl.when(j == 0)
    def _():
        _mod_prologue(x_ref, g_ref, mod_ref, hn_ref, sh, sc)

    acc = jnp.dot(hn_ref[...], w_ref[...], preferred_element_type=F32)
    tn = acc.shape[1]

    @pl.when(j < 2)
    def _():
        cos = cos_ref[...]
        sin = sin_ref[...]
        lane = lax.broadcasted_iota(jnp.int32, cos.shape, 1)
        scale = jnp.where(j == 0, DA_HEAD_DIM ** -0.5 * math.log2(math.e), 1.0).astype(F32)
        for cb in range(tn // LANE):
            a = acc[:, cb * LANE:(cb + 1) * LANE]
            o_ref[0, cb] = (_rope_block(a, cos, sin, lane) * scale).astype(o_ref.dtype)

    @pl.when(j >= 2)
    def _():
        for cb in range(tn // LANE):
            o_ref[0, cb] = acc[:, cb * LANE:(cb + 1) * LANE].astype(o_ref.dtype)


def _proj_gate_resid_kernel(yf_ref, yb_ref, z_ref, g_ref, w_ref, xr_ref, gate_ref, o_ref, hn_ref,
                            *, gi):
    @pl.when(pl.program_id(2) == 0)
    def _():
        z = jnp.concatenate([z_ref[c, 0] for c in range(z_ref.shape[0])], axis=1).astype(F32)
        u = (yf_ref[0, 0].astype(F32) + yb_ref[0, 0].astype(F32)) * _silu(z)
        hn_ref[...] = _rms(u, g_ref[...]).astype(BF16)

    acc = jnp.dot(hn_ref[...], w_ref[...], preferred_element_type=F32)
    o_ref[0] = xr_ref[0] + gate_ref[0][gi:gi + 1, :] * acc


def _proj_heads_resid_kernel(h_ref, w_ref, xr_ref, gate_ref, o_ref, *, gi):
    h = jnp.concatenate([h_ref[0, hh] for hh in range(h_ref.shape[1])], axis=1)
    acc = jnp.dot(h, w_ref[...], preferred_element_type=F32)
    o_ref[0] = xr_ref[0] + gate_ref[0][gi:gi + 1, :] * acc


def _proj_scores_kernel(x_ref, g_ref, mod_ref, w_ref, keys_ref, s_ref, hq_ref, hn_ref, *, sh, sc):
    @pl.when(pl.program_id(2) == 0)
    def _():
        _mod_prologue(x_ref, g_ref, mod_ref, hn_ref, sh, sc)
        hq_ref[0] = hn_ref[...]

    q = jnp.dot(hn_ref[...], w_ref[...], preferred_element_type=F32)
    half = PEER_NKEYS
    for c in range(2):
        s_ref[0, 0, 0, c] = lax.dot_general(
            keys_ref[0, c], q[:, c * half:(c + 1) * half], NT_DIMS,
            preferred_element_type=F32, precision=lax.Precision.HIGHEST)


class _Stream:
    def __init__(self, batch, s_lat, s_ctx, d):
        assert s_lat % TOK_TILE == 0 and s_ctx % CONV_TILE == 0 and s_lat % GRID_W == 0
        self.batch, self.s_lat, self.s_ctx, self.d = batch, s_lat, s_ctx, d
        self.s_real = s_lat + s_ctx
        self.s_pad = -(-self.s_real // TOK_TILE) * TOK_TILE
        self.n_tok = self.s_pad // TOK_TILE
        self.n_lat_tok = s_lat // TOK_TILE

    def mod_row(self, b, i):
        return b * 2 + jnp.where(i < self.n_lat_tok, 1, 0)


def _mod_specs(st, d):
    return [pl.BlockSpec((1, TOK_TILE, d), lambda b, i, j: (b, i, 0)),
            pl.BlockSpec((1, d), lambda b, i, j: (0, 0)),
            pl.BlockSpec((1, 6, d), lambda b, i, j: (st.mod_row(b, i), 0, 0))]


def _proj_mod(st, x, g, modtab, w, sh, sc, tn, out_dtype, name):
    d, n = w.shape
    return pl.pallas_call(
        functools.partial(_proj_mod_kernel, sh=sh, sc=sc),
        grid=(st.batch, st.n_tok, n // tn),
        in_specs=_mod_specs(st, d) + [pl.BlockSpec((d, tn), lambda b, i, j: (0, j))],
        out_specs=pl.BlockSpec((1, 1, TOK_TILE, tn), lambda b, i, j: (j, b, i, 0)),
        out_shape=jax.ShapeDtypeStruct((n // tn, st.batch, st.s_pad, tn), out_dtype),
        scratch_shapes=[pltpu.VMEM((TOK_TILE, d), BF16)],
        compiler_params=_cparams(("parallel", "parallel", "arbitrary")),
        name=name,
    )(x, g.reshape(1, d), modtab, w)


def _proj_qkv(st, x, g, modtab, w, cos, sin, sh, sc):
    d, n = w.shape
    tn = 1024
    return pl.pallas_call(
        functools.partial(_proj_qkv_kernel, sh=sh, sc=sc),
        grid=(st.batch, st.n_tok, n // tn),
        in_specs=_mod_specs(st, d) + [
            pl.BlockSpec((d, tn), lambda b, i, j: (0, j)),
            pl.BlockSpec((TOK_TILE, LANE), lambda b, i, j: (i, 0)),
            pl.BlockSpec((TOK_TILE, LANE), lambda b, i, j: (i, 0))],
        out_specs=pl.BlockSpec((1, tn // LANE, TOK_TILE, LANE), lambda b, i, j: (b, j, i, 0)),
        out_shape=jax.ShapeDtypeStruct((st.batch, n // LANE, st.s_pad, LANE), BF16),
        scratch_shapes=[pltpu.VMEM((TOK_TILE, d), BF16)],
        compiler_params=_cparams(("parallel", "parallel", "arbitrary")),
        name="attn_qkv_rope",
    )(x, g.reshape(1, d), modtab, w, cos, sin)


def _proj_gate_resid(st, y2, zx, g, w, x, modtab, gi):
    k, n = w.shape
    tn = n
    return pl.pallas_call(
        functools.partial(_proj_gate_resid_kernel, gi=gi),
        grid=(st.batch, st.n_tok, n // tn),
        in_specs=[pl.BlockSpec((1, 1, TOK_TILE, k), lambda b, i, j: (0, b, i, 0)),
                  pl.BlockSpec((1, 1, TOK_TILE, k), lambda b, i, j: (1, b, i, 0)),
                  pl.BlockSpec((k // zx.shape[3], 1, TOK_TILE, zx.shape[3]), lambda b, i, j: (0, b, i, 0)),
                  pl.BlockSpec((1, k), lambda b, i, j: (0, 0)),
                  pl.BlockSpec((k, tn), lambda b, i, j: (0, j)),
                  pl.BlockSpec((1, TOK_TILE, tn), lambda b, i, j: (b, i, j)),
                  pl.BlockSpec((1, 6, tn), lambda b, i, j: (st.mod_row(b, i), 0, j))],
        out_specs=pl.BlockSpec((1, TOK_TILE, tn), lambda b, i, j: (b, i, j)),
        out_shape=jax.ShapeDtypeStruct((st.batch, st.s_pad, n), F32),
        scratch_shapes=[pltpu.VMEM((TOK_TILE, k), BF16)],
        compiler_params=_cparams(("parallel", "parallel", "arbitrary")),
        name="ssd_out_proj",
    )(y2, y2, zx, g.reshape(1, k), w, x, modtab)


def _proj_heads_resid(st, h, w, x, modtab, gi):
    k, n = w.shape
    tn = n
    return pl.pallas_call(
        functools.partial(_proj_heads_resid_kernel, gi=gi),
        grid=(st.batch, st.n_tok, n // tn),
        in_specs=[pl.BlockSpec((1, k // LANE, TOK_TILE, LANE), lambda b, i, j: (b, 0, i, 0)),
                  pl.BlockSpec((k, tn), lambda b, i, j: (0, j)),
                  pl.BlockSpec((1, TOK_TILE, tn), lambda b, i, j: (b, i, j)),
                  pl.BlockSpec((1, 6, tn), lambda b, i, j: (st.mod_row(b, i), 0, j))],
        out_specs=pl.BlockSpec((1, TOK_TILE, tn), lambda b, i, j: (b, i, j)),
        out_shape=jax.ShapeDtypeStruct((st.batch, st.s_pad, n), F32),
        compiler_params=_cparams(("parallel", "parallel", "arbitrary")),
        name="attn_out_proj",
    )(h, w, x, modtab)


def _proj_scores(st, x, g, modtab, w, keys, sh, sc):
    d, n = w.shape
    tn = 2 * PEER_NKEYS
    return pl.pallas_call(
        functools.partial(_proj_scores_kernel, sh=sh, sc=sc),
        grid=(st.batch, st.n_tok, PEER_HEADS),
        in_specs=_mod_specs(st, d) + [
            pl.BlockSpec((d, tn), lambda b, i, j: (0, j)),
            pl.BlockSpec((1, 2, PEER_NKEYS, PEER_NKEYS), lambda b, i, j: (j, 0, 0, 0))],
        out_specs=[pl.BlockSpec((1, 1, 1, 2, PEER_NKEYS, TOK_TILE), lambda b, i, j: (b, i, j, 0, 0, 0)),
                   pl.BlockSpec((1, TOK_TILE, d), lambda b, i, j: (b, i, 0))],
        out_shape=[jax.ShapeDtypeStruct((st.batch, st.n_tok, PEER_HEADS, 2, PEER_NKEYS, TOK_TILE), F32),
                   jax.ShapeDtypeStruct((st.batch, st.s_pad, d), BF16)],
        scratch_shapes=[pltpu.VMEM((TOK_TILE, d), BF16)],
        compiler_params=_cparams(("parallel", "parallel", "arbitrary")),
        name="peer_scores",
    )(x, g.reshape(1, d), modtab, w, keys)


def _conv_kernel(x_ref, prev_ref, next_ref, w_ref, b_ref, o_ref, buf_ref, *, seg_starts, n_tiles):
    i = pl.program_id(1)
    pad = SSD_CONV_W // 2
    first = functools.reduce(jnp.logical_or, [i == s for s in seg_starts])
    last = functools.reduce(jnp.logical_or, [i == s - 1 for s in seg_starts[1:] + (n_tiles,)])
    t = x_ref.shape[2]
    buf_ref[8:8 + t, :] = x_ref[0, 0].astype(F32)
    buf_ref[0:8, :] = jnp.where(first, 0.0, prev_ref[0, 0].astype(F32)[8:16, :])
    buf_ref[8 + t:16 + t, :] = jnp.where(last, 0.0, next_ref[0, 0].astype(F32)[0:8, :])
    w = w_ref[...]
    acc = b_ref[...] + w[0:1, :] * buf_ref[8 - pad:8 - pad + t, :]
    for k in range(1, SSD_CONV_W):
        acc = acc + w[k:k + 1, :] * buf_ref[8 - pad + k:8 - pad + k + t, :]
    o_ref[0, 0] = _silu(acc).astype(o_ref.dtype)


def _conv_call(st, zx, conv_w, conv_b):
    cw = 1024
    n_cblk = SSD_CONV_DIM // cw
    off = SSD_D_INNER // cw
    n_tiles = st.s_pad // CONV_TILE
    hb = CONV_TILE // 16
    n_hblk = st.s_pad // 16
    seg_starts = (0, st.s_lat // CONV_TILE, st.s_real // CONV_TILE)
    w8 = jnp.zeros((8, SSD_CONV_DIM), F32).at[:SSD_CONV_W].set(conv_w)
    return pl.pallas_call(
        functools.partial(_conv_kernel, seg_starts=seg_starts, n_tiles=n_tiles),
        grid=(st.batch, n_tiles, n_cblk),
        in_specs=[pl.BlockSpec((1, 1, CONV_TILE, cw), lambda b, i, c: (off + c, b, i, 0)),
                  pl.BlockSpec((1, 1, 16, cw), lambda b, i, c: (off + c, b, jnp.maximum(i * hb - 1, 0), 0)),
                  pl.BlockSpec((1, 1, 16, cw),
                               lambda b, i, c: (off + c, b, jnp.minimum((i + 1) * hb, n_hblk - 1), 0)),
                  pl.BlockSpec((8, cw), lambda b, i, c: (0, c)),
                  pl.BlockSpec((1, cw), lambda b, i, c: (0, c))],
        out_specs=pl.BlockSpec((1, 1, CONV_TILE, cw), lambda b, i, c: (c, b, i, 0)),
        out_shape=jax.ShapeDtypeStruct((n_cblk, st.batch, st.s_pad, cw), BF16),
        scratch_shapes=[pltpu.VMEM((CONV_TILE + 16, cw), F32)],
        compiler_params=_cparams(("parallel", "parallel", "parallel")),
        name="ssd_conv",
    )(zx, zx, zx, w8, conv_b.reshape(1, SSD_CONV_DIM))


def _scan_kernel(xs_ref, bc_ref, dt_ref, tri_ref, par_ref, dlane_ref, y_ref, h_ref):
    @pl.when(pl.program_id(2) == 0)
    def _():
        h_ref[...] = jnp.zeros_like(h_ref)

    par = par_ref[0]
    raw = dt_ref[0, 0] + par[0:1, :]
    dt = jnp.maximum(raw, 0.0) + jnp.log1p(jnp.exp(-jnp.abs(raw)))
    a = -jnp.exp(par[1:2, :])
    dta = dt * a
    tri = tri_ref[0]
    cs = jnp.dot(tri, dta, preferred_element_type=F32, precision=lax.Precision.HIGHEST)
    tot = jnp.sum(dta, axis=0, keepdims=True)
    e_tot = jnp.exp(tot)
    w_end = jnp.exp(tot - cs) * dt
    cs_t = cs.T
    dt_t = dt.T
    w_end_t = w_end.T
    mask = tri > 0.5

    e_cs_t = jnp.exp(cs_t)
    n = SSD_STATE
    q = tri.shape[0]
    lane = lax.broadcasted_iota(jnp.int32, (q, LANE), 1)
    left = lane < SSD_HEADDIM
    eye = lax.broadcasted_iota(jnp.int32, (q, q), 0) == lax.broadcasted_iota(jnp.int32, (q, q), 1)
    lane1 = lax.broadcasted_iota(jnp.int32, (1, LANE), 1) < SSD_HEADDIM
    for g in range(SSD_GROUPS):
        bm = bc_ref[0, 0, :, g * n:(g + 1) * n]
        cm = bc_ref[0, 0, :, SSD_GN + g * n:SSD_GN + (g + 1) * n]
        bm_t = bm.astype(F32).T
        cb = lax.dot_general(cm, bm, NT_DIMS, preferred_element_type=F32)
        for r in range(0, SSD_HPG, 2):
            hds = (g * SSD_HPG + r, g * SSD_HPG + r + 1)
            pair = hds[0] // 2
            xb, xo = divmod(pair * LANE, xs_ref.shape[3])
            xp = xs_ref[xb, 0, :, xo:xo + LANE]
            zero = jnp.zeros_like(xp)
            x_blk = jnp.concatenate([jnp.where(left, xp, zero), jnp.where(left, zero, xp)], axis=0)
            h_prev = h_ref[pair]
            ch = jnp.dot(cm, h_prev.astype(BF16), preferred_element_type=F32)
            ch_blk = jnp.concatenate([jnp.where(left, ch, 0.0), jnp.where(left, 0.0, ch)], axis=0)
            w_parts, d_parts, u_parts = [], [], []
            for hd in hds:
                seg = cs[:, hd:hd + 1] - cs_t[hd:hd + 1, :]
                decay = jnp.where(mask, jnp.exp(seg), 0.0)
                w_parts.append(cb * decay * dt_t[hd:hd + 1, :])
                d_parts.append(jnp.where(eye, e_cs_t[hd:hd + 1, :], 0.0))
                u_parts.append(bm_t * w_end_t[hd:hd + 1, :])
            lhs = jnp.concatenate(w_parts + d_parts, axis=1).astype(BF16)
            rhs = jnp.concatenate([x_blk, ch_blk.astype(BF16)], axis=0)
            y = jnp.dot(lhs, rhs, preferred_element_type=F32)
            y = y + dlane_ref[0, :, pair * LANE:(pair + 1) * LANE] * xp.astype(F32)
            y_ref[0, 0, :, pair * LANE:(pair + 1) * LANE] = y.astype(y_ref.dtype)
            upd = jnp.dot(jnp.concatenate(u_parts, axis=1).astype(BF16), x_blk, preferred_element_type=F32)
            e_tot_pair = jnp.where(lane1, e_tot[:, hds[0]:hds[0] + 1], e_tot[:, hds[1]:hds[1] + 1])
            h_ref[pair] = e_tot_pair * h_prev + upd


def _scan_call(st, xc, dtr, par, dlane):
    q = SSD_CHUNK
    n_lat, n_ctx = st.s_lat // q, st.s_ctx // q
    n_real = n_lat + n_ctx
    n_all = st.s_pad // q
    cw = xc.shape[3]
    n_xblk = SSD_D_INNER // cw

    def chunk(d, s):
        fwd = jnp.where(s < n_ctx, n_lat + s, s - n_ctx)
        bwd = n_real - 1 - s
        return jnp.where(s < n_real, jnp.where(d == 0, fwd, bwd), s)

    idx = np.arange(q)
    tri = jnp.asarray(np.stack([idx[:, None] >= idx[None, :], idx[:, None] <= idx[None, :]]), F32)
    return pl.pallas_call(
        _scan_kernel,
        grid=(st.batch, 2, n_all),
        in_specs=[pl.BlockSpec((n_xblk, 1, q, cw), lambda b, d, s: (0, b, chunk(d, s), 0)),
                  pl.BlockSpec((1, 1, q, cw), lambda b, d, s: (n_xblk, b, chunk(d, s), 0)),
                  pl.BlockSpec((1, 1, q, LANE), lambda b, d, s: (d, b, chunk(d, s), 0)),
                  pl.BlockSpec((1, q, q), lambda b, d, s: (d, 0, 0)),
                  pl.BlockSpec((1, 8, LANE), lambda b, d, s: (d, 0, 0)),
                  pl.BlockSpec((1, 1, SSD_D_INNER), lambda b, d, s: (d, 0, 0))],
        out_specs=pl.BlockSpec((1, 1, q, SSD_D_INNER), lambda b, d, s: (d, b, chunk(d, s), 0)),
        out_shape=jax.ShapeDtypeStruct((2, st.batch, st.s_pad, SSD_D_INNER), BF16),
        scratch_shapes=[pltpu.VMEM((SSD_HEADS // 2, SSD_STATE, 2 * SSD_HEADDIM), F32)],
        compiler_params=_cparams(("parallel", "parallel", "arbitrary")),
        name="ssd_scan",
    )(xc, xc, dtr, tri, par, dlane)


def _attn_kernel(q_ref, k_ref, v_ref, lq_ref, lk_ref, g_ref, o_ref, qs_ref, vt_ref, m_ref, acc_ref,
                 s_ref, *, s_lat, s_ctx, lambda_init):
    qi = pl.program_id(2)
    tq = q_ref.shape[2]
    tk = ATT_TK
    tv = vt_ref.shape[2]
    n_sub = tk // tv
    hw = q_ref.shape[3]

    @pl.when(qi == 0)
    def _():
        def tr(c, carry):
            rows = v_ref[0, 0, pl.ds(pl.multiple_of(c * tv, tv), tv), :]
            vt_ref[c, 0:hw, :] = rows.astype(F32).T.astype(BF16)
            sub = lax.broadcasted_iota(jnp.int32, (vt_ref.shape[1] - hw, tv), 0)
            vt_ref[c, hw:, :] = jnp.where(sub == 0, 1.0, 0.0).astype(BF16)
            return carry

        lax.fori_loop(0, vt_ref.shape[0], tr, 0)

    qt = q_ref[0, 0].astype(F32).T
    row = lax.broadcasted_iota(jnp.int32, qt.shape, 0)
    qs_ref[:, 0:tq] = jnp.where(row < DA_HEAD_DIM, qt, 0.0).astype(BF16)
    qs_ref[:, tq:2 * tq] = jnp.where(row >= DA_HEAD_DIM, qt, 0.0).astype(BF16)
    m_ref[...] = jnp.full_like(m_ref, -jnp.inf)
    acc_ref[...] = jnp.zeros_like(acc_ref)

    def scores(k):
        return jnp.dot(k, qs_ref[...], preferred_element_type=F32)

    def update(read_s, vts):
        m_prev = m_ref[...]
        m_new = jnp.maximum(m_prev, jnp.max(read_s(), axis=0, keepdims=True))
        alpha = jnp.exp2(m_prev - m_new)
        pb = jnp.exp2(read_s() - m_new).astype(BF16)
        rows = pb.shape[0] // len(vts)
        pv = None
        for i, vt in enumerate(vts):
            d = jnp.dot(vt, pb[i * rows:(i + 1) * rows, :], preferred_element_type=F32)
            pv = d if pv is None else pv + d
        acc_ref[...] = alpha * acc_ref[...] + pv
        m_ref[...] = m_new

    n_lat = s_lat // tk

    def k_chunk(c):
        return k_ref[0, 0, pl.ds(pl.multiple_of(c * tk, tk), tk), :]

    @pl.when(qi < s_lat // tq)
    def _():
        s_ref[0] = scores(k_chunk(0))

        def half_step(c, slot):
            s_ref[1 - slot] = scores(k_chunk(jnp.minimum(c + 1, n_lat - 1)))
            update(lambda: s_ref[slot], [vt_ref[c * n_sub + i] for i in range(n_sub)])

        def body(cp, carry):
            half_step(2 * cp, 0)
            half_step(2 * cp + 1, 1)
            return carry

        lax.fori_loop(0, n_lat // 2, body, 0)
        if n_lat % 2:
            half_step(n_lat - 1, 0)

    s_ctx_keys = scores(k_ref[0, 0, s_lat:s_lat + s_ctx, :])
    update(lambda: s_ctx_keys, [vt_ref[s_lat // tv, :, 0:s_ctx]])

    lam = (jnp.exp(jnp.sum(lq_ref[0:1, :] * lk_ref[0:1, :], axis=-1, keepdims=True))
           - jnp.exp(jnp.sum(lq_ref[1:2, :] * lk_ref[1:2, :], axis=-1, keepdims=True)) + lambda_init)
    o_all = acc_ref[0:hw, :] / acc_ref[hw:hw + 1, :]
    o = (o_all[:, 0:tq] - lam * o_all[:, tq:2 * tq]).T
    o_ref[0, 0] = (_rms(o, g_ref[...]) * (1.0 - lambda_init)).astype(o_ref.dtype)


def _attn_call(st, qkv, lam_q, lam_k, subln_g, lambda_init):
    tq = ATT_TQ
    nq = st.s_pad // tq
    hw = 2 * DA_HEAD_DIM
    nh = DA_HEADS
    return pl.pallas_call(
        functools.partial(_attn_kernel, s_lat=st.s_lat, s_ctx=st.s_ctx, lambda_init=lambda_init),
        grid=(st.batch, nh, nq),
        in_specs=[pl.BlockSpec((1, 1, tq, hw), lambda b, h, i: (b, h, i, 0)),
                  pl.BlockSpec((1, 1, st.s_pad, hw), lambda b, h, i: (b, nh + h, 0, 0)),
                  pl.BlockSpec((1, 1, st.s_pad, hw), lambda b, h, i: (b, 2 * nh + h, 0, 0)),
                  pl.BlockSpec((2, DA_HEAD_DIM), lambda b, h, i: (0, 0)),
                  pl.BlockSpec((2, DA_HEAD_DIM), lambda b, h, i: (0, 0)),
                  pl.BlockSpec((1, hw), lambda b, h, i: (0, 0))],
        out_specs=pl.BlockSpec((1, 1, tq, hw), lambda b, h, i: (b, h, i, 0)),
        out_shape=jax.ShapeDtypeStruct((st.batch, nh, st.s_pad, hw), BF16),
        scratch_shapes=[pltpu.VMEM((hw, 2 * tq), BF16),
                        pltpu.VMEM((st.s_pad // ATT_TV, hw + ATT_SUM_ROWS, ATT_TV), BF16),
                        pltpu.VMEM((1, 2 * tq), F32),
                        pltpu.VMEM((hw + ATT_SUM_ROWS, 2 * tq), F32),
                        pltpu.VMEM((2, ATT_TK, 2 * tq), F32)],
        compiler_params=_cparams(("parallel", "parallel", "arbitrary")),
        name="diff_attn",
    )(qkv, qkv, qkv, lam_q, lam_k, subln_g.reshape(1, hw))


def _extract_top(s, dst_ref, n):
    w = s
    for k in range(n):
        m = jnp.max(w, axis=0, keepdims=True)
        dst_ref[k:k + 1, :] = m
        w = jnp.where(w == m, -jnp.inf, w)


def _gate_kernel(s_ref, e1_ref, tau_ref, e2_ref, a_ref, b_ref):
    k = PEER_TOPK
    s1 = s_ref[0, 0, 0, 0]
    s2 = s_ref[0, 0, 0, 1]
    _extract_top(s1, a_ref, k)
    _extract_top(s2, b_ref, k)
    a = a_ref[...]
    b = b_ref[...]
    cand = jnp.concatenate(
        [a[0:1, :] + b] + [a[i:i + 1, :] + b[0:8, :] for i in range(1, 8)] + [a[8:16, :] + b[0:1, :]],
        axis=0)
    w = cand
    tau = None
    for it in range(k):
        tau = jnp.max(w, axis=0, keepdims=True)
        if it + 1 < k:
            w = jnp.where(w == tau, -jnp.inf, w)
    top = a[0:1, :] + b[0:1, :]
    z = jnp.sum(jnp.where(cand >= tau, jnp.exp(cand - top), 0.0), axis=0, keepdims=True)
    blk_shape = e1_ref.shape[2:3] + e1_ref.shape[4:]
    e1_ref[0, 0, :, 0] = jnp.exp(s1 - a[0:1, :]).reshape(blk_shape)
    e2_ref[0, 0, 0] = jnp.exp(s2 - b[0:1, :]) / z
    tau_ref[0, 0, :, 0] = (tau - s1).reshape(blk_shape)


def _gate_call(st, scores):
    nk = PEER_NKEYS
    tt = TOK_TILE
    rows = PEER_EBLK // nk
    n_blk = nk // rows
    shp_blk = jax.ShapeDtypeStruct((st.batch, st.n_tok, n_blk, PEER_HEADS, rows, tt), F32)
    spec_blk = pl.BlockSpec((1, 1, n_blk, 1, rows, tt), lambda b, i, h: (b, i, 0, h, 0, 0))
    shp_e2 = jax.ShapeDtypeStruct((st.batch, st.n_tok, PEER_HEADS, nk, tt), F32)
    spec_e2 = pl.BlockSpec((1, 1, 1, nk, tt), lambda b, i, h: (b, i, h, 0, 0))
    return pl.pallas_call(
        _gate_kernel,
        grid=(st.batch, st.n_tok, PEER_HEADS),
        in_specs=[pl.BlockSpec((1, 1, 1, 2, nk, tt), lambda b, i, h: (b, i, h, 0, 0, 0))],
        out_specs=[spec_blk, spec_blk, spec_e2],
        out_shape=[shp_blk, shp_blk, shp_e2],
        scratch_shapes=[pltpu.VMEM((PEER_TOPK, TOK_TILE), F32), pltpu.VMEM((PEER_TOPK, TOK_TILE), F32)],
        compiler_params=_cparams(("parallel", "parallel", "parallel")),
        name="peer_gates",
    )(scores)


def _expert_kernel(hq_ref, u0_ref, un_ref, vt_ref, e1_ref, tau_ref, s2_ref, e2_ref, xr_ref, gate_ref, o_ref,
                   acc_ref, act_ref, p_ref, hqt_ref, *, gi):
    e = pl.program_id(2)
    nk = PEER_NKEYS
    tt = hq_ref.shape[1]
    half = tt // 2
    sqrt_half = np.float32(math.sqrt(0.5))

    @pl.when(e == 0)
    def _():
        acc_ref[...] = jnp.zeros_like(acc_ref)
        hqt_ref[...] = hq_ref[0].astype(F32).T.astype(BF16)
        act_ref[0] = jnp.dot(u0_ref[...], hqt_ref[...], preferred_element_type=F32)

    def gate_tile(cur, a_list, t0):
        cols = slice(t0, t0 + LANE)
        e1 = [[e1_ref[0, 0, 0, h, a:a + 1, cols] for h in range(PEER_HEADS)] for a in a_list]
        tau = [[tau_ref[0, 0, 0, h, a:a + 1, cols] for h in range(PEER_HEADS)] for a in a_list]
        for r0 in range(0, nk, PEER_RB):
            rows = slice(r0, r0 + PEER_RB)
            g = [None] * len(a_list)
            for h in range(PEER_HEADS):
                s2 = s2_ref[0, 0, h, 0, rows, cols]
                e2 = e2_ref[0, 0, h, rows, cols]
                for ai in range(len(a_list)):
                    term = e1[ai][h] * jnp.where(s2 >= tau[ai][h], e2, 0.0)
                    g[ai] = term if g[ai] is None else g[ai] + term
            for ai, a in enumerate(a_list):
                x = act_ref[cur, a * nk + r0:a * nk + r0 + PEER_RB, cols]
                gelu = 0.5 * x * (1.0 + lax.erf(x * sqrt_half))
                p_ref[a * nk + r0:a * nk + r0 + PEER_RB, cols] = (g[ai] * gelu).astype(BF16)

    def stages(cur, nxt):
        rows_per = 2 * nk
        act_rows = 2 * rows_per
        for c0 in (0, half):
            for r0 in range(0, PEER_EBLK, rows_per):
                if r0 % act_rows == 0:
                    act_ref[nxt, r0:r0 + act_rows, c0:c0 + half] = jnp.dot(
                        un_ref[r0:r0 + act_rows, :], hqt_ref[:, c0:c0 + half],
                        preferred_element_type=F32)
                a_list = list(range(r0 // nk, (r0 + rows_per) // nk))
                for t0 in range(c0, c0 + half, LANE):
                    gate_tile(cur, a_list, t0)
                acc_ref[:, c0:c0 + half] += jnp.dot(vt_ref[0, :, r0:r0 + rows_per],
                                                    p_ref[r0:r0 + rows_per, c0:c0 + half],
                                                    preferred_element_type=F32)

    for parity in range(2):
        pl.when(e % 2 == parity)(functools.partial(stages, parity, 1 - parity))

    @pl.when(e == pl.num_programs(2) - 1)
    def _():
        o_ref[0] = xr_ref[0] + gate_ref[0][gi:gi + 1, :] * acc_ref[...].T


def _expert_call(st, hq, u, vt, e1, tau, scores, e2, x, modtab, gi):
    n_exp, d = u.shape
    nk = PEER_NKEYS
    rows = PEER_EBLK // nk
    tt = TOK_TILE
    n_blk = n_exp // PEER_EBLK
    return pl.pallas_call(
        functools.partial(_expert_kernel, gi=gi),
        grid=(st.batch, st.n_tok, n_blk),
        in_specs=[pl.BlockSpec((1, tt, d), lambda b, i, e: (b, i, 0)),
                  pl.BlockSpec((PEER_EBLK, d), lambda b, i, e: (0, 0)),
                  pl.BlockSpec((PEER_EBLK, d), lambda b, i, e: (jnp.minimum(e + 1, n_blk - 1), 0)),
                  pl.BlockSpec((1, d, PEER_EBLK), lambda b, i, e: (e, 0, 0)),
                  pl.BlockSpec((1, 1, 1, PEER_HEADS, rows, tt), lambda b, i, e: (b, i, e, 0, 0, 0)),
                  pl.BlockSpec((1, 1, 1, PEER_HEADS, rows, tt), lambda b, i, e: (b, i, e, 0, 0, 0)),
                  pl.BlockSpec((1, 1, PEER_HEADS, 1, nk, tt), lambda b, i, e: (b, i, 0, 1, 0, 0)),
                  pl.BlockSpec((1, 1, PEER_HEADS, nk, tt), lambda b, i, e: (b, i, 0, 0, 0)),
                  pl.BlockSpec((1, tt, d), lambda b, i, e: (b, i, 0)),
                  pl.BlockSpec((1, 6, d), lambda b, i, e: (st.mod_row(b, i), 0, 0))],
        out_specs=pl.BlockSpec((1, tt, d), lambda b, i, e: (b, i, 0)),
        out_shape=jax.ShapeDtypeStruct((st.batch, st.s_pad, d), F32),
        scratch_shapes=[pltpu.VMEM((d, tt), F32),
                        pltpu.VMEM((2, PEER_EBLK, tt), F32),
                        pltpu.VMEM((PEER_EBLK, tt), BF16),
                        pltpu.VMEM((d, tt), BF16)],
        compiler_params=_cparams(("parallel", "parallel", "arbitrary")),
        name="peer_experts",
    )(hq, u, u, vt, e1, tau, scores, e2, x, modtab)


def _final_norm_kernel(x_ref, g_ref, o_ref):
    o_ref[0] = _rms(x_ref[0], g_ref[...])


def _final_norm(st, x, g):
    d = st.d
    return pl.pallas_call(
        _final_norm_kernel,
        grid=(st.batch, st.n_lat_tok),
        in_specs=[pl.BlockSpec((1, TOK_TILE, d), lambda b, i: (b, i, 0)),
                  pl.BlockSpec((1, d), lambda b, i: (0, 0))],
        out_specs=pl.BlockSpec((1, TOK_TILE, d), lambda b, i: (b, i, 0)),
        out_shape=jax.ShapeDtypeStruct((st.batch, st.s_lat, d), F32),
        compiler_params=_cparams(("parallel", "parallel")),
        name="final_norm",
    )(x, g.reshape(1, d))


def _rope_tables(st):
    rows = st.s_lat // GRID_W
    row = jnp.repeat(jnp.arange(rows, dtype=F32), GRID_W)
    col = jnp.tile(jnp.arange(GRID_W, dtype=F32), rows)
    half = DA_HEAD_DIM // 2
    freqs = ROPE_BASE ** (-jnp.arange(0, half, 2, dtype=F32) / half)
    ang_r = row[:, None] * freqs
    ang_c = col[:, None] * freqs
    ang = jnp.concatenate([ang_r, ang_r, ang_c, ang_c] * 2, axis=-1)
    extra = st.s_pad - st.s_lat
    cos = jnp.concatenate([jnp.cos(ang), jnp.ones((extra, LANE), F32)], axis=0)
    sin = jnp.concatenate([jnp.sin(ang), jnp.zeros((extra, LANE), F32)], axis=0)
    return cos, sin


def _qkv_weight(w):
    d = w.shape[0]
    qk = DA_HEADS * DA_HEAD_DIM * 2

    def regroup(t):
        return t.reshape(d, 2, DA_HEADS, DA_HEAD_DIM).transpose(0, 2, 1, 3).reshape(d, qk)

    return jnp.concatenate([regroup(w[:, :qk]), regroup(w[:, qk:2 * qk]), w[:, 2 * qk:]], axis=1).astype(BF16)


def _blocked_vt(v):
    n_exp, d = v.shape
    return v.reshape(n_exp // PEER_EBLK, PEER_EBLK, d).transpose(0, 2, 1).astype(BF16)


def _dt_weight(w_dt):
    d = w_dt.shape[0]
    t = w_dt.reshape(d, 2, SSD_HEADS)
    return jnp.pad(t, ((0, 0), (0, 0), (0, LANE - SSD_HEADS))).reshape(d, 2 * LANE).astype(BF16)


def _scan_params(dt_bias, a_log, d_skip):
    def lanes(t):
        return jnp.pad(t.reshape(2, SSD_HEADS), ((0, 0), (0, LANE - SSD_HEADS)))

    rows = jnp.stack([lanes(dt_bias), lanes(a_log)], axis=1)
    dlane = jnp.repeat(d_skip.reshape(2, SSD_HEADS), SSD_HEADDIM, axis=1)[:, None, :]
    return jnp.pad(rows, ((0, 0), (0, 6), (0, 0))).astype(F32), dlane.astype(F32)


def kernel(x, c, ctx, c_ctx, ada_w, ada_b, norm_mix_g, norm_ffn_g, ssd_in_w, ssd_conv_w, ssd_conv_b,
           ssd_dt_bias, ssd_a_log, ssd_d, ssd_norm_g, ssd_out_w, attn_qkv_w, attn_lambda_q, attn_lambda_k,
           attn_subln_g, attn_out_w, peer_q_w, peer_keys, peer_u, peer_v, final_norm_g):
    batch, s_lat, d = x.shape
    s_ctx = ctx.shape[1]
    depth = ada_w.shape[0]
    st = _Stream(batch, s_lat, s_ctx, d)

    xs = jnp.concatenate([x, ctx, jnp.zeros((batch, st.s_pad - st.s_real, d), F32)], axis=1)

    cc = jnp.zeros((8, d), F32).at[:batch].set(c).at[batch].set(c_ctx)
    mods = _ada_call(cc, ada_w, ada_b).reshape(depth, 8, 6, d)
    cos, sin = _rope_tables(st)
    zxw = SSD_D_INNER + SSD_CONV_DIM

    for i in range(depth):
        lat = mods[i, :batch]
        con = jnp.broadcast_to(mods[i, batch], (batch, 6, d))
        modtab = jnp.stack([con, lat], axis=1).reshape(batch * 2, 6, d)
        jm = i // 2
        if i % 2 == 0:
            w_in = ssd_in_w[jm]
            zx = _proj_mod(st, xs, norm_mix_g[i], modtab, w_in[:, :zxw].astype(BF16), 0, 1, 1024, BF16,
                           "ssd_in_proj")
            dtr = _proj_mod(st, xs, norm_mix_g[i], modtab, _dt_weight(w_in[:, zxw:]), 0, 1, LANE, F32,
                            "ssd_dt_proj")
            xc = _conv_call(st, zx, ssd_conv_w[jm], ssd_conv_b[jm])
            y2 = _scan_call(st, xc, dtr, *_scan_params(ssd_dt_bias[jm], ssd_a_log[jm], ssd_d[jm]))
            xs = _proj_gate_resid(st, y2, zx, ssd_norm_g[jm], ssd_out_w[jm].astype(BF16), xs, modtab, 2)
        else:
            lambda_init = 0.8 - 0.6 * math.exp(-0.3 * i)
            qkv = _proj_qkv(st, xs, norm_mix_g[i], modtab, _qkv_weight(attn_qkv_w[jm]), cos, sin, 0, 1)
            o = _attn_call(st, qkv, attn_lambda_q[jm], attn_lambda_k[jm], attn_subln_g[jm], lambda_init)
            xs = _proj_heads_resid(st, o, attn_out_w[jm].astype(BF16), xs, modtab, 2)
        scores, hq = _proj_scores(st, xs, norm_ffn_g[i], modtab, peer_q_w[i].astype(BF16), peer_keys[i], 3, 4)
        e1, tau, e2 = _gate_call(st, scores)
        xs = _expert_call(st, hq, peer_u[i].astype(BF16), _blocked_vt(peer_v[i]), e1, tau, scores, e2,
                          xs, modtab, 5)
    return _final_norm(st, xs, final_norm_g)
```

```python
import functools
import math

import numpy as np
import jax
import jax.numpy as jnp
from jax import lax
from jax.experimental import pallas as pl
from jax.experimental.pallas import tpu as pltpu

F32 = jnp.float32
BF16 = jnp.bfloat16

EPS = 1e-6
GRID_W = 64
ROPE_BASE = 10000.0

SSD_HEADDIM = 64
SSD_GROUPS = 4
SSD_HPG = 8
SSD_HEADS = SSD_GROUPS * SSD_HPG
SSD_STATE = 128
SSD_CONV_W = 5
SSD_CHUNK = 128
SSD_D_INNER = SSD_HEADS * SSD_HEADDIM
SSD_GN = SSD_GROUPS * SSD_STATE
SSD_CONV_DIM = SSD_D_INNER + 2 * SSD_GN

DA_HEADS = 8
DA_HEAD_DIM = 64

PEER_HEADS = 8
PEER_NKEYS = 128
PEER_TOPK = 16
PEER_EBLK = 1024
PEER_RB = 64

LANE = 128
TOK_TILE = 512
CONV_TILE = 256
ATT_TQ = 512
ATT_TK = 1024
ATT_TV = 512
ATT_SUM_ROWS = 16
VMEM_LIMIT = 56 * 1024 * 1024

NT_DIMS = (((1,), (1,)), ((), ()))


def _cparams(sem):
    return pltpu.CompilerParams(dimension_semantics=sem, vmem_limit_bytes=VMEM_LIMIT)


def _rms(xf, g):
    return xf * lax.rsqrt(jnp.mean(xf * xf, axis=-1, keepdims=True) + EPS) * g


def _silu(x):
    return x * (1.0 / (1.0 + jnp.exp(-x)))


def _ada_kernel(c_ref, w_ref, b_ref, o_ref):
    c = c_ref[...]
    o_ref[0] = jnp.dot(_silu(c), w_ref[0], preferred_element_type=F32,
                       precision=lax.Precision.HIGHEST) + b_ref[0]


def _ada_call(cc, ada_w, ada_b):
    depth, d, n = ada_w.shape
    tn = 1536
    return pl.pallas_call(
        _ada_kernel,
        grid=(depth, n // tn),
        in_specs=[pl.BlockSpec((8, d), lambda l, j: (0, 0)),
                  pl.BlockSpec((1, d, tn), lambda l, j: (l, 0, j)),
                  pl.BlockSpec((1, 1, tn), lambda l, j: (l, 0, j))],
        out_specs=pl.BlockSpec((1, 8, tn), lambda l, j: (l, 0, j)),
        out_shape=jax.ShapeDtypeStruct((depth, 8, n), F32),
        compiler_params=_cparams(("parallel", "parallel")),
        name="ada_mod",
    )(cc, ada_w, ada_b.reshape(depth, 1, n))


def _mod_prologue(x_ref, g_ref, mod_ref, hn_ref, sh, sc):
    m = mod_ref[0]
    hn = _rms(x_ref[0], g_ref[...]) * (1.0 + m[sc:sc + 1, :]) + m[sh:sh + 1, :]
    hn_ref[...] = hn.astype(BF16)


def _proj_mod_kernel(x_ref, g_ref, mod_ref, w_ref, o_ref, hn_ref, *, sh, sc):
    @pl.when(pl.program_id(2) == 0)
    def _():
        _mod_prologue(x_ref, g_ref, mod_ref, hn_ref, sh, sc)

    o_ref[0, 0] = jnp.dot(hn_ref[...], w_ref[...], preferred_element_type=F32).astype(o_ref.dtype)


def _rope_block(a, cos, sin, lane):
    fwd = pltpu.roll(a, LANE - 16, axis=1)
    bwd = pltpu.roll(a, 16, axis=1)
    rot = jnp.where(lane % 32 < 16, -fwd, bwd)
    return a * cos + rot * sin


def _proj_qkv_kernel(x_ref, g_ref, mod_ref, w_ref, cos_ref, sin_ref, o_ref, hn_ref, *, sh, sc):
    j = pl.program_id(2)

    @pl.when(j == 0)
    def _():
        _mod_prologue(x_ref, g_ref, mod_ref, hn_ref, sh, sc)

    acc = jnp.dot(hn_ref[...], w_ref[...], preferred_element_type=F32)
    tn = acc.shape[1]

    @pl.when(j < 2)
    def _():
        cos = cos_ref[...]
        sin = sin_ref[...]
        lane = lax.broadcasted_iota(jnp.int32, cos.shape, 1)
        scale = jnp.where(j == 0, DA_HEAD_DIM ** -0.5 * math.log2(math.e), 1.0).astype(F32)
        for cb in range(tn // LANE):
            a = acc[:, cb * LANE:(cb + 1) * LANE]
            o_ref[0, cb] = (_rope_block(a, cos, sin, lane) * scale).astype(o_ref.dtype)

    @pl.when(j >= 2)
    def _():
        for cb in range(tn // LANE):
            o_ref[0, cb] = acc[:, cb * LANE:(cb + 1) * LANE].astype(o_ref.dtype)


def _proj_gate_resid_kernel(yf_ref, yb_ref, z_ref, g_ref, w_ref, xr_ref, gate_ref, o_ref, hn_ref,
                            *, gi):
    @pl.when(pl.program_id(2) == 0)
    def _():
        z = jnp.concatenate([z_ref[c, 0] for c in range(z_ref.shape[0])], axis=1).astype(F32)
        u = (yf_ref[0, 0].astype(F32) + yb_ref[0, 0].astype(F32)) * _silu(z)
        hn_ref[...] = _rms(u, g_ref[...]).astype(BF16)

    acc = jnp.dot(hn_ref[...], w_ref[...], preferred_element_type=F32)
    o_ref[0] = xr_ref[0] + gate_ref[0][gi:gi + 1, :] * acc


def _proj_heads_resid_kernel(h_ref, w_ref, xr_ref, gate_ref, o_ref, *, gi):
    h = jnp.concatenate([h_ref[0, hh] for hh in range(h_ref.shape[1])], axis=1)
    acc = jnp.dot(h, w_ref[...], preferred_element_type=F32)
    o_ref[0] = xr_ref[0] + gate_ref[0][gi:gi + 1, :] * acc


def _proj_scores_kernel(x_ref, g_ref, mod_ref, w_ref, keys_ref, s_ref, hq_ref, hn_ref, *, sh, sc):
    @pl.when(pl.program_id(2) == 0)
    def _():
        _mod_prologue(x_ref, g_ref, mod_ref, hn_ref, sh, sc)
        hq_ref[0] = hn_ref[...]

    q = jnp.dot(hn_ref[...], w_ref[...], preferred_element_type=F32)
    half = PEER_NKEYS
    for c in range(2):
        s_hc = lax.dot_general(
            keys_ref[0, c], q[:, c * half:(c + 1) * half], NT_DIMS,
            preferred_element_type=F32, precision=lax.Precision.HIGHEST)
        for lb in range(s_hc.shape[1] // LANE):
            s_ref[0, 0, 0, c, lb] = s_hc[:, lb * LANE:(lb + 1) * LANE]


class _Stream:
    def __init__(self, batch, s_lat, s_ctx, d):
        assert s_lat % TOK_TILE == 0 and s_ctx % CONV_TILE == 0 and s_lat % GRID_W == 0
        self.batch, self.s_lat, self.s_ctx, self.d = batch, s_lat, s_ctx, d
        self.s_real = s_lat + s_ctx
        self.s_pad = -(-self.s_real // TOK_TILE) * TOK_TILE
        self.n_tok = self.s_pad // TOK_TILE
        self.n_lat_tok = s_lat // TOK_TILE

    def mod_row(self, b, i):
        return b * 2 + jnp.where(i < self.n_lat_tok, 1, 0)


def _mod_specs(st, d):
    return [pl.BlockSpec((1, TOK_TILE, d), lambda b, i, j: (b, i, 0)),
            pl.BlockSpec((1, d), lambda b, i, j: (0, 0)),
            pl.BlockSpec((1, 6, d), lambda b, i, j: (st.mod_row(b, i), 0, 0))]


def _proj_mod(st, x, g, modtab, w, sh, sc, tn, out_dtype, name):
    d, n = w.shape
    return pl.pallas_call(
        functools.partial(_proj_mod_kernel, sh=sh, sc=sc),
        grid=(st.batch, st.n_tok, n // tn),
        in_specs=_mod_specs(st, d) + [pl.BlockSpec((d, tn), lambda b, i, j: (0, j))],
        out_specs=pl.BlockSpec((1, 1, TOK_TILE, tn), lambda b, i, j: (j, b, i, 0)),
        out_shape=jax.ShapeDtypeStruct((n // tn, st.batch, st.s_pad, tn), out_dtype),
        scratch_shapes=[pltpu.VMEM((TOK_TILE, d), BF16)],
        compiler_params=_cparams(("parallel", "parallel", "arbitrary")),
        name=name,
    )(x, g.reshape(1, d), modtab, w)


def _proj_qkv(st, x, g, modtab, w, cos, sin, sh, sc):
    d, n = w.shape
    tn = 1024
    return pl.pallas_call(
        functools.partial(_proj_qkv_kernel, sh=sh, sc=sc),
        grid=(st.batch, st.n_tok, n // tn),
        in_specs=_mod_specs(st, d) + [
            pl.BlockSpec((d, tn), lambda b, i, j: (0, j)),
            pl.BlockSpec((TOK_TILE, LANE), lambda b, i, j: (i, 0)),
            pl.BlockSpec((TOK_TILE, LANE), lambda b, i, j: (i, 0))],
        out_specs=pl.BlockSpec((1, tn // LANE, TOK_TILE, LANE), lambda b, i, j: (b, j, i, 0)),
        out_shape=jax.ShapeDtypeStruct((st.batch, n // LANE, st.s_pad, LANE), BF16),
        scratch_shapes=[pltpu.VMEM((TOK_TILE, d), BF16)],
        compiler_params=_cparams(("parallel", "parallel", "arbitrary")),
        name="attn_qkv_rope",
    )(x, g.reshape(1, d), modtab, w, cos, sin)


def _proj_gate_resid(st, y2, zx, g, w, x, modtab, gi):
    k, n = w.shape
    tn = n
    return pl.pallas_call(
        functools.partial(_proj_gate_resid_kernel, gi=gi),
        grid=(st.batch, st.n_tok, n // tn),
        in_specs=[pl.BlockSpec((1, 1, TOK_TILE, k), lambda b, i, j: (0, b, i, 0)),
                  pl.BlockSpec((1, 1, TOK_TILE, k), lambda b, i, j: (1, b, i, 0)),
                  pl.BlockSpec((k // zx.shape[3], 1, TOK_TILE, zx.shape[3]), lambda b, i, j: (0, b, i, 0)),
                  pl.BlockSpec((1, k), lambda b, i, j: (0, 0)),
                  pl.BlockSpec((k, tn), lambda b, i, j: (0, j)),
                  pl.BlockSpec((1, TOK_TILE, tn), lambda b, i, j: (b, i, j)),
                  pl.BlockSpec((1, 6, tn), lambda b, i, j: (st.mod_row(b, i), 0, j))],
        out_specs=pl.BlockSpec((1, TOK_TILE, tn), lambda b, i, j: (b, i, j)),
        out_shape=jax.ShapeDtypeStruct((st.batch, st.s_pad, n), F32),
        scratch_shapes=[pltpu.VMEM((TOK_TILE, k), BF16)],
        compiler_params=_cparams(("parallel", "parallel", "arbitrary")),
        name="ssd_out_proj",
    )(y2, y2, zx, g.reshape(1, k), w, x, modtab)


def _proj_heads_resid(st, h, w, x, modtab, gi):
    k, n = w.shape
    tn = n
    return pl.pallas_call(
        functools.partial(_proj_heads_resid_kernel, gi=gi),
        grid=(st.batch, st.n_tok, n // tn),
        in_specs=[pl.BlockSpec((1, k // LANE, TOK_TILE, LANE), lambda b, i, j: (b, 0, i, 0)),
                  pl.BlockSpec((k, tn), lambda b, i, j: (0, j)),
                  pl.BlockSpec((1, TOK_TILE, tn), lambda b, i, j: (b, i, j)),
                  pl.BlockSpec((1, 6, tn), lambda b, i, j: (st.mod_row(b, i), 0, j))],
        out_specs=pl.BlockSpec((1, TOK_TILE, tn), lambda b, i, j: (b, i, j)),
        out_shape=jax.ShapeDtypeStruct((st.batch, st.s_pad, n), F32),
        compiler_params=_cparams(("parallel", "parallel", "arbitrary")),
        name="attn_out_proj",
    )(h, w, x, modtab)


def _proj_scores(st, x, g, modtab, w, keys, sh, sc):
    d, n = w.shape
    tn = 2 * PEER_NKEYS
    return pl.pallas_call(
        functools.partial(_proj_scores_kernel, sh=sh, sc=sc),
        grid=(st.batch, st.n_tok, PEER_HEADS),
        in_specs=_mod_specs(st, d) + [
            pl.BlockSpec((d, tn), lambda b, i, j: (0, j)),
            pl.BlockSpec((1, 2, PEER_NKEYS, PEER_NKEYS), lambda b, i, j: (j, 0, 0, 0))],
        out_specs=[pl.BlockSpec((1, 1, 1, 2, TOK_TILE // LANE, PEER_NKEYS, LANE),
                                lambda b, i, j: (b, i, j, 0, 0, 0, 0)),
                   pl.BlockSpec((1, TOK_TILE, d), lambda b, i, j: (b, i, 0))],
        out_shape=[jax.ShapeDtypeStruct(
            (st.batch, st.n_tok, PEER_HEADS, 2, TOK_TILE // LANE, PEER_NKEYS, LANE), F32),
                   jax.ShapeDtypeStruct((st.batch, st.s_pad, d), BF16)],
        scratch_shapes=[pltpu.VMEM((TOK_TILE, d), BF16)],
        compiler_params=_cparams(("parallel", "parallel", "arbitrary")),
        name="peer_scores",
    )(x, g.reshape(1, d), modtab, w, keys)


def _conv_kernel(x_ref, prev_ref, next_ref, w_ref, b_ref, o_ref, buf_ref, *, seg_starts, n_tiles):
    i = pl.program_id(1)
    pad = SSD_CONV_W // 2
    first = functools.reduce(jnp.logical_or, [i == s for s in seg_starts])
    last = functools.reduce(jnp.logical_or, [i == s - 1 for s in seg_starts[1:] + (n_tiles,)])
    t = x_ref.shape[2]
    buf_ref[8:8 + t, :] = x_ref[0, 0].astype(F32)
    buf_ref[0:8, :] = jnp.where(first, 0.0, prev_ref[0, 0].astype(F32)[8:16, :])
    buf_ref[8 + t:16 + t, :] = jnp.where(last, 0.0, next_ref[0, 0].astype(F32)[0:8, :])
    w = w_ref[...]
    acc = b_ref[...] + w[0:1, :] * buf_ref[8 - pad:8 - pad + t, :]
    for k in range(1, SSD_CONV_W):
        acc = acc + w[k:k + 1, :] * buf_ref[8 - pad + k:8 - pad + k + t, :]
    o_ref[0, 0] = _silu(acc).astype(o_ref.dtype)


def _conv_call(st, zx, conv_w, conv_b):
    cw = 1024
    n_cblk = SSD_CONV_DIM // cw
    off = SSD_D_INNER // cw
    n_tiles = st.s_pad // CONV_TILE
    hb = CONV_TILE // 16
    n_hblk = st.s_pad // 16
    seg_starts = (0, st.s_lat // CONV_TILE, st.s_real // CONV_TILE)
    w8 = jnp.zeros((8, SSD_CONV_DIM), F32).at[:SSD_CONV_W].set(conv_w)
    return pl.pallas_call(
        functools.partial(_conv_kernel, seg_starts=seg_starts, n_tiles=n_tiles),
        grid=(st.batch, n_tiles, n_cblk),
        in_specs=[pl.BlockSpec((1, 1, CONV_TILE, cw), lambda b, i, c: (off + c, b, i, 0)),
                  pl.BlockSpec((1, 1, 16, cw), lambda b, i, c: (off + c, b, jnp.maximum(i * hb - 1, 0), 0)),
                  pl.BlockSpec((1, 1, 16, cw),
                               lambda b, i, c: (off + c, b, jnp.minimum((i + 1) * hb, n_hblk - 1), 0)),
                  pl.BlockSpec((8, cw), lambda b, i, c: (0, c)),
                  pl.BlockSpec((1, cw), lambda b, i, c: (0, c))],
        out_specs=pl.BlockSpec((1, 1, CONV_TILE, cw), lambda b, i, c: (c, b, i, 0)),
        out_shape=jax.ShapeDtypeStruct((n_cblk, st.batch, st.s_pad, cw), BF16),
        scratch_shapes=[pltpu.VMEM((CONV_TILE + 16, cw), F32)],
        compiler_params=_cparams(("parallel", "parallel", "parallel")),
        name="ssd_conv",
    )(zx, zx, zx, w8, conv_b.reshape(1, SSD_CONV_DIM))


def _scan_kernel(xs_ref, bc_ref, dt_ref, tri_ref, par_ref, dlane_ref, y_ref, h_ref):
    @pl.when(pl.program_id(2) == 0)
    def _():
        h_ref[...] = jnp.zeros_like(h_ref)

    par = par_ref[0]
    raw = dt_ref[0, 0] + par[0:1, :]
    dt = jnp.maximum(raw, 0.0) + jnp.log1p(jnp.exp(-jnp.abs(raw)))
    a = -jnp.exp(par[1:2, :])
    dta = dt * a
    tri = tri_ref[0]
    cs = jnp.dot(tri, dta, preferred_element_type=F32, precision=lax.Precision.HIGHEST)
    tot = jnp.sum(dta, axis=0, keepdims=True)
    e_tot = jnp.exp(tot)
    w_end = jnp.exp(tot - cs) * dt
    cs_t = cs.T
    dt_t = dt.T
    w_end_t = w_end.T
    mask = tri > 0.5

    e_cs_t = jnp.exp(cs_t)
    n = SSD_STATE
    q = tri.shape[0]
    lane = lax.broadcasted_iota(jnp.int32, (q, LANE), 1)
    left = lane < SSD_HEADDIM
    eye = lax.broadcasted_iota(jnp.int32, (q, q), 0) == lax.broadcasted_iota(jnp.int32, (q, q), 1)
    lane1 = lax.broadcasted_iota(jnp.int32, (1, LANE), 1) < SSD_HEADDIM
    for g in range(SSD_GROUPS):
        bm = bc_ref[0, 0, :, g * n:(g + 1) * n]
        cm = bc_ref[0, 0, :, SSD_GN + g * n:SSD_GN + (g + 1) * n]
        bm_t = bm.astype(F32).T
        cb = lax.dot_general(cm, bm, NT_DIMS, preferred_element_type=F32)
        for r in range(0, SSD_HPG, 2):
            hds = (g * SSD_HPG + r, g * SSD_HPG + r + 1)
            pair = hds[0] // 2
            xb, xo = divmod(pair * LANE, xs_ref.shape[3])
            xp = xs_ref[xb, 0, :, xo:xo + LANE]
            zero = jnp.zeros_like(xp)
            x_blk = jnp.concatenate([jnp.where(left, xp, zero), jnp.where(left, zero, xp)], axis=0)
            h_prev = h_ref[pair]
            ch = jnp.dot(cm, h_prev.astype(BF16), preferred_element_type=F32)
            ch_blk = jnp.concatenate([jnp.where(left, ch, 0.0), jnp.where(left, 0.0, ch)], axis=0)
            w_parts, d_parts, u_parts = [], [], []
            for hd in hds:
                seg = cs[:, hd:hd + 1] - cs_t[hd:hd + 1, :]
                decay = jnp.where(mask, jnp.exp(seg), 0.0)
                w_parts.append(cb * decay * dt_t[hd:hd + 1, :])
                d_parts.append(jnp.where(eye, e_cs_t[hd:hd + 1, :], 0.0))
                u_parts.append(bm_t * w_end_t[hd:hd + 1, :])
            lhs = jnp.concatenate(w_parts + d_parts, axis=1).astype(BF16)
            rhs = jnp.concatenate([x_blk, ch_blk.astype(BF16)], axis=0)
            y = jnp.dot(lhs, rhs, preferred_element_type=F32)
            y = y + dlane_ref[0, :, pair * LANE:(pair + 1) * LANE] * xp.astype(F32)
            y_ref[0, 0, :, pair * LANE:(pair + 1) * LANE] = y.astype(y_ref.dtype)
            upd = jnp.dot(jnp.concatenate(u_parts, axis=1).astype(BF16), x_blk, preferred_element_type=F32)
            e_tot_pair = jnp.where(lane1, e_tot[:, hds[0]:hds[0] + 1], e_tot[:, hds[1]:hds[1] + 1])
            h_ref[pair] = e_tot_pair * h_prev + upd


def _scan_call(st, xc, dtr, par, dlane):
    q = SSD_CHUNK
    n_lat, n_ctx = st.s_lat // q, st.s_ctx // q
    n_real = n_lat + n_ctx
    n_all = st.s_pad // q
    cw = xc.shape[3]
    n_xblk = SSD_D_INNER // cw

    def chunk(d, s):
        fwd = jnp.where(s < n_ctx, n_lat + s, s - n_ctx)
        bwd = n_real - 1 - s
        return jnp.where(s < n_real, jnp.where(d == 0, fwd, bwd), s)

    idx = np.arange(q)
    tri = jnp.asarray(np.stack([idx[:, None] >= idx[None, :], idx[:, None] <= idx[None, :]]), F32)
    return pl.pallas_call(
        _scan_kernel,
        grid=(st.batch, 2, n_all),
        in_specs=[pl.BlockSpec((n_xblk, 1, q, cw), lambda b, d, s: (0, b, chunk(d, s), 0)),
                  pl.BlockSpec((1, 1, q, cw), lambda b, d, s: (n_xblk, b, chunk(d, s), 0)),
                  pl.BlockSpec((1, 1, q, LANE), lambda b, d, s: (d, b, chunk(d, s), 0)),
                  pl.BlockSpec((1, q, q), lambda b, d, s: (d, 0, 0)),
                  pl.BlockSpec((1, 8, LANE), lambda b, d, s: (d, 0, 0)),
                  pl.BlockSpec((1, 1, SSD_D_INNER), lambda b, d, s: (d, 0, 0))],
        out_specs=pl.BlockSpec((1, 1, q, SSD_D_INNER), lambda b, d, s: (d, b, chunk(d, s), 0)),
        out_shape=jax.ShapeDtypeStruct((2, st.batch, st.s_pad, SSD_D_INNER), BF16),
        scratch_shapes=[pltpu.VMEM((SSD_HEADS // 2, SSD_STATE, 2 * SSD_HEADDIM), F32)],
        compiler_params=_cparams(("parallel", "parallel", "arbitrary")),
        name="ssd_scan",
    )(xc, xc, dtr, tri, par, dlane)


def _attn_kernel(q_ref, k_ref, v_ref, lq_ref, lk_ref, g_ref, o_ref, qs_ref, vt_ref, m_ref, acc_ref,
                 s_ref, *, s_lat, s_ctx, lambda_init):
    qi = pl.program_id(2)
    tq = q_ref.shape[2]
    tk = ATT_TK
    tv = vt_ref.shape[2]
    n_sub = tk // tv
    hw = q_ref.shape[3]

    @pl.when(qi == 0)
    def _():
        def tr(c, carry):
            rows = v_ref[0, 0, pl.ds(pl.multiple_of(c * tv, tv), tv), :]
            vt_ref[c, 0:hw, :] = rows.astype(F32).T.astype(BF16)
            sub = lax.broadcasted_iota(jnp.int32, (vt_ref.shape[1] - hw, tv), 0)
            vt_ref[c, hw:, :] = jnp.where(sub == 0, 1.0, 0.0).astype(BF16)
            return carry

        lax.fori_loop(0, vt_ref.shape[0], tr, 0)

    qt = q_ref[0, 0].astype(F32).T
    row = lax.broadcasted_iota(jnp.int32, qt.shape, 0)
    qs_ref[:, 0:tq] = jnp.where(row < DA_HEAD_DIM, qt, 0.0).astype(BF16)
    qs_ref[:, tq:2 * tq] = jnp.where(row >= DA_HEAD_DIM, qt, 0.0).astype(BF16)
    m_ref[...] = jnp.full_like(m_ref, -jnp.inf)
    acc_ref[...] = jnp.zeros_like(acc_ref)

    def scores(k):
        return jnp.dot(k, qs_ref[...], preferred_element_type=F32)

    def update(read_s, vts):
        m_prev = m_ref[...]
        m_new = jnp.maximum(m_prev, jnp.max(read_s(), axis=0, keepdims=True))
        alpha = jnp.exp2(m_prev - m_new)
        pb = jnp.exp2(read_s() - m_new).astype(BF16)
        rows = pb.shape[0] // len(vts)
        pv = None
        for i, vt in enumerate(vts):
            d = jnp.dot(vt, pb[i * rows:(i + 1) * rows, :], preferred_element_type=F32)
            pv = d if pv is None else pv + d
        acc_ref[...] = alpha * acc_ref[...] + pv
        m_ref[...] = m_new

    n_lat = s_lat // tk

    def k_chunk(c):
        return k_ref[0, 0, pl.ds(pl.multiple_of(c * tk, tk), tk), :]

    n_cb = s_ref.shape[1]

    def put_scores(slot, s):
        for cb in range(n_cb):
            s_ref[slot, cb] = s[:, cb * LANE:(cb + 1) * LANE]

    def get_scores(slot):
        return jnp.concatenate([s_ref[slot, cb] for cb in range(n_cb)], axis=1)

    @pl.when(qi < s_lat // tq)
    def _():
        put_scores(0, scores(k_chunk(0)))

        def half_step(c, slot):
            put_scores(1 - slot, scores(k_chunk(jnp.minimum(c + 1, n_lat - 1))))
            update(lambda: get_scores(slot), [vt_ref[c * n_sub + i] for i in range(n_sub)])

        def body(cp, carry):
            half_step(2 * cp, 0)
            half_step(2 * cp + 1, 1)
            return carry

        lax.fori_loop(0, n_lat // 2, body, 0)
        if n_lat % 2:
            half_step(n_lat - 1, 0)

    s_ctx_keys = scores(k_ref[0, 0, s_lat:s_lat + s_ctx, :])
    update(lambda: s_ctx_keys, [vt_ref[s_lat // tv, :, 0:s_ctx]])

    lam = (jnp.exp(jnp.sum(lq_ref[0:1, :] * lk_ref[0:1, :], axis=-1, keepdims=True))
           - jnp.exp(jnp.sum(lq_ref[1:2, :] * lk_ref[1:2, :], axis=-1, keepdims=True)) + lambda_init)
    o_all = acc_ref[0:hw, :] / acc_ref[hw:hw + 1, :]
    o = (o_all[:, 0:tq] - lam * o_all[:, tq:2 * tq]).T
    o_ref[0, 0] = (_rms(o, g_ref[...]) * (1.0 - lambda_init)).astype(o_ref.dtype)


def _attn_call(st, qkv, lam_q, lam_k, subln_g, lambda_init):
    tq = ATT_TQ
    nq = st.s_pad // tq
    hw = 2 * DA_HEAD_DIM
    nh = DA_HEADS
    return pl.pallas_call(
        functools.partial(_attn_kernel, s_lat=st.s_lat, s_ctx=st.s_ctx, lambda_init=lambda_init),
        grid=(st.batch, nh, nq),
        in_specs=[pl.BlockSpec((1, 1, tq, hw), lambda b, h, i: (b, h, i, 0)),
                  pl.BlockSpec((1, 1, st.s_pad, hw), lambda b, h, i: (b, nh + h, 0, 0)),
                  pl.BlockSpec((1, 1, st.s_pad, hw), lambda b, h, i: (b, 2 * nh + h, 0, 0)),
                  pl.BlockSpec((2, DA_HEAD_DIM), lambda b, h, i: (0, 0)),
                  pl.BlockSpec((2, DA_HEAD_DIM), lambda b, h, i: (0, 0)),
                  pl.BlockSpec((1, hw), lambda b, h, i: (0, 0))],
        out_specs=pl.BlockSpec((1, 1, tq, hw), lambda b, h, i: (b, h, i, 0)),
        out_shape=jax.ShapeDtypeStruct((st.batch, nh, st.s_pad, hw), BF16),
        scratch_shapes=[pltpu.VMEM((hw, 2 * tq), BF16),
                        pltpu.VMEM((st.s_pad // ATT_TV, hw + ATT_SUM_ROWS, ATT_TV), BF16),
                        pltpu.VMEM((1, 2 * tq), F32),
                        pltpu.VMEM((hw + ATT_SUM_ROWS, 2 * tq), F32),
                        pltpu.VMEM((2, 2 * tq // LANE, ATT_TK, LANE), F32)],
        compiler_params=_cparams(("parallel", "parallel", "arbitrary")),
        name="diff_attn",
    )(qkv, qkv, qkv, lam_q, lam_k, subln_g.reshape(1, hw))


def _extract_top(s, dst_ref, n):
    w = s
    for k in range(n):
        m = jnp.max(w, axis=0, keepdims=True)
        dst_ref[k:k + 1, :] = m
        w = jnp.where(w == m, -jnp.inf, w)


def _gate_kernel(s_ref, e1_ref, tau_ref, e2_ref, a_ref, b_ref):
    k = PEER_TOPK
    n_lb = s_ref.shape[4]
    s1 = jnp.concatenate([s_ref[0, 0, 0, 0, lb] for lb in range(n_lb)], axis=1)
    s2 = jnp.concatenate([s_ref[0, 0, 0, 1, lb] for lb in range(n_lb)], axis=1)
    _extract_top(s1, a_ref, k)
    _extract_top(s2, b_ref, k)
    a = a_ref[...]
    b = b_ref[...]
    cand = jnp.concatenate(
        [a[0:1, :] + b] + [a[i:i + 1, :] + b[0:8, :] for i in range(1, 8)] + [a[8:16, :] + b[0:1, :]],
        axis=0)
    w = cand
    tau = None
    for it in range(k):
        tau = jnp.max(w, axis=0, keepdims=True)
        if it + 1 < k:
            w = jnp.where(w == tau, -jnp.inf, w)
    top = a[0:1, :] + b[0:1, :]
    z = jnp.sum(jnp.where(cand >= tau, jnp.exp(cand - top), 0.0), axis=0, keepdims=True)
    blk_shape = e1_ref.shape[2:3] + e1_ref.shape[4:]
    e1_ref[0, 0, :, 0] = jnp.exp(s1 - a[0:1, :]).reshape(blk_shape)
    e2 = jnp.exp(s2 - b[0:1, :]) / z
    for lb in range(n_lb):
        e2_ref[0, 0, 0, lb] = e2[:, lb * LANE:(lb + 1) * LANE]
    tau_ref[0, 0, :, 0] = (tau - s1).reshape(blk_shape)


def _gate_call(st, scores):
    nk = PEER_NKEYS
    tt = TOK_TILE
    rows = PEER_EBLK // nk
    n_blk = nk // rows
    shp_blk = jax.ShapeDtypeStruct((st.batch, st.n_tok, n_blk, PEER_HEADS, rows, tt), F32)
    spec_blk = pl.BlockSpec((1, 1, n_blk, 1, rows, tt), lambda b, i, h: (b, i, 0, h, 0, 0))
    n_lb = tt // LANE
    shp_e2 = jax.ShapeDtypeStruct((st.batch, st.n_tok, PEER_HEADS, n_lb, nk, LANE), F32)
    spec_e2 = pl.BlockSpec((1, 1, 1, n_lb, nk, LANE), lambda b, i, h: (b, i, h, 0, 0, 0))
    return pl.pallas_call(
        _gate_kernel,
        grid=(st.batch, st.n_tok, PEER_HEADS),
        in_specs=[pl.BlockSpec((1, 1, 1, 2, n_lb, nk, LANE), lambda b, i, h: (b, i, h, 0, 0, 0, 0))],
        out_specs=[spec_blk, spec_blk, spec_e2],
        out_shape=[shp_blk, shp_blk, shp_e2],
        scratch_shapes=[pltpu.VMEM((PEER_TOPK, TOK_TILE), F32), pltpu.VMEM((PEER_TOPK, TOK_TILE), F32)],
        compiler_params=_cparams(("parallel", "parallel", "parallel")),
        name="peer_gates",
    )(scores)


def _expert_kernel(hq_ref, u0_ref, un_ref, vt_ref, e1_ref, tau_ref, s2_ref, e2_ref, xr_ref, gate_ref, o_ref,
                   acc_ref, act_ref, p_ref, hqt_ref, *, gi):
    e = pl.program_id(2)
    nk = PEER_NKEYS
    tt = hq_ref.shape[1]
    half = tt // 2
    sqrt_half = np.float32(math.sqrt(0.5))

    def put_act(slot, r0, r1, c0, c1, val):
        for lb in range(c0 // LANE, c1 // LANE):
            act_ref[slot, lb, r0:r1, :] = val[:, lb * LANE - c0:(lb + 1) * LANE - c0]

    @pl.when(e == 0)
    def _():
        acc_ref[...] = jnp.zeros_like(acc_ref)
        hqt_ref[...] = hq_ref[0].astype(F32).T.astype(BF16)
        put_act(0, 0, PEER_EBLK, 0, tt, jnp.dot(u0_ref[...], hqt_ref[...], preferred_element_type=F32))

    def gate_tile(cur, a_list, t0):
        cols = slice(t0, t0 + LANE)
        lb = t0 // LANE
        e1 = [[e1_ref[0, 0, 0, h, a:a + 1, cols] for h in range(PEER_HEADS)] for a in a_list]
        tau = [[tau_ref[0, 0, 0, h, a:a + 1, cols] for h in range(PEER_HEADS)] for a in a_list]
        for r0 in range(0, nk, PEER_RB):
            rows = slice(r0, r0 + PEER_RB)
            g = [None] * len(a_list)
            for h in range(PEER_HEADS):
                s2 = s2_ref[0, 0, h, 0, lb, rows, :]
                e2 = e2_ref[0, 0, h, lb, rows, :]
                for ai in range(len(a_list)):
                    term = e1[ai][h] * jnp.where(s2 >= tau[ai][h], e2, 0.0)
                    g[ai] = term if g[ai] is None else g[ai] + term
            for ai, a in enumerate(a_list):
                x = act_ref[cur, lb, a * nk + r0:a * nk + r0 + PEER_RB, :]
                gelu = 0.5 * x * (1.0 + lax.erf(x * sqrt_half))
                p_ref[lb, a * nk + r0:a * nk + r0 + PEER_RB, :] = (g[ai] * gelu).astype(BF16)

    def stages(cur, nxt):
        rows_per = 2 * nk
        act_rows = 2 * rows_per
        for c0 in (0, half):
            for r0 in range(0, PEER_EBLK, rows_per):
                if r0 % act_rows == 0:
                    put_act(nxt, r0, r0 + act_rows, c0, c0 + half, jnp.dot(
                        un_ref[r0:r0 + act_rows, :], hqt_ref[:, c0:c0 + half],
                        preferred_element_type=F32))
                a_list = list(range(r0 // nk, (r0 + rows_per) // nk))
                for t0 in range(c0, c0 + half, LANE):
                    gate_tile(cur, a_list, t0)
                p_blk = jnp.concatenate([p_ref[lb, r0:r0 + rows_per, :]
                                         for lb in range(c0 // LANE, (c0 + half) // LANE)], axis=1)
                acc_ref[:, c0:c0 + half] += jnp.dot(vt_ref[0, :, r0:r0 + rows_per], p_blk,
                                                    preferred_element_type=F32)

    for parity in range(2):
        pl.when(e % 2 == parity)(functools.partial(stages, parity, 1 - parity))

    @pl.when(e == pl.num_programs(2) - 1)
    def _():
        o_ref[0] = xr_ref[0] + gate_ref[0][gi:gi + 1, :] * acc_ref[...].T


def _expert_call(st, hq, u, vt, e1, tau, scores, e2, x, modtab, gi):
    n_exp, d = u.shape
    nk = PEER_NKEYS
    rows = PEER_EBLK // nk
    tt = TOK_TILE
    n_blk = n_exp // PEER_EBLK
    return pl.pallas_call(
        functools.partial(_expert_kernel, gi=gi),
        grid=(st.batch, st.n_tok, n_blk),
        in_specs=[pl.BlockSpec((1, tt, d), lambda b, i, e: (b, i, 0)),
                  pl.BlockSpec((PEER_EBLK, d), lambda b, i, e: (0, 0)),
                  pl.BlockSpec((PEER_EBLK, d), lambda b, i, e: (jnp.minimum(e + 1, n_blk - 1), 0)),
                  pl.BlockSpec((1, d, PEER_EBLK), lambda b, i, e: (e, 0, 0)),
                  pl.BlockSpec((1, 1, 1, PEER_HEADS, rows, tt), lambda b, i, e: (b, i, e, 0, 0, 0)),
                  pl.BlockSpec((1, 1, 1, PEER_HEADS, rows, tt), lambda b, i, e: (b, i, e, 0, 0, 0)),
                  pl.BlockSpec((1, 1, PEER_HEADS, 1, tt // LANE, nk, LANE),
                               lambda b, i, e: (b, i, 0, 1, 0, 0, 0)),
                  pl.BlockSpec((1, 1, PEER_HEADS, tt // LANE, nk, LANE), lambda b, i, e: (b, i, 0, 0, 0, 0)),
                  pl.BlockSpec((1, tt, d), lambda b, i, e: (b, i, 0)),
                  pl.BlockSpec((1, 6, d), lambda b, i, e: (st.mod_row(b, i), 0, 0))],
        out_specs=pl.BlockSpec((1, tt, d), lambda b, i, e: (b, i, 0)),
        out_shape=jax.ShapeDtypeStruct((st.batch, st.s_pad, d), F32),
        scratch_shapes=[pltpu.VMEM((d, tt), F32),
                        pltpu.VMEM((2, tt // LANE, PEER_EBLK, LANE), F32),
                        pltpu.VMEM((tt // LANE, PEER_EBLK, LANE), BF16),
                        pltpu.VMEM((d, tt), BF16)],
        compiler_params=_cparams(("parallel", "parallel", "arbitrary")),
        name="peer_experts",
    )(hq, u, u, vt, e1, tau, scores, e2, x, modtab)


def _final_norm_kernel(x_ref, g_ref, o_ref):
    o_ref[0] = _rms(x_ref[0], g_ref[...])


def _final_norm(st, x, g):
    d = st.d
    return pl.pallas_call(
        _final_norm_kernel,
        grid=(st.batch, st.n_lat_tok),
        in_specs=[pl.BlockSpec((1, TOK_TILE, d), lambda b, i: (b, i, 0)),
                  pl.BlockSpec((1, d), lambda b, i: (0, 0))],
        out_specs=pl.BlockSpec((1, TOK_TILE, d), lambda b, i: (b, i, 0)),
        out_shape=jax.ShapeDtypeStruct((st.batch, st.s_lat, d), F32),
        compiler_params=_cparams(("parallel", "parallel")),
        name="final_norm",
    )(x, g.reshape(1, d))


def _rope_tables(st):
    rows = st.s_lat // GRID_W
    row = jnp.repeat(jnp.arange(rows, dtype=F32), GRID_W)
    col = jnp.tile(jnp.arange(GRID_W, dtype=F32), rows)
    half = DA_HEAD_DIM // 2
    freqs = ROPE_BASE ** (-jnp.arange(0, half, 2, dtype=F32) / half)
    ang_r = row[:, None] * freqs
    ang_c = col[:, None] * freqs
    ang = jnp.concatenate([ang_r, ang_r, ang_c, ang_c] * 2, axis=-1)
    extra = st.s_pad - st.s_lat
    cos = jnp.concatenate([jnp.cos(ang), jnp.ones((extra, LANE), F32)], axis=0)
    sin = jnp.concatenate([jnp.sin(ang), jnp.zeros((extra, LANE), F32)], axis=0)
    return cos, sin


def _qkv_weight(w):
    d = w.shape[0]
    qk = DA_HEADS * DA_HEAD_DIM * 2

    def regroup(t):
        return t.reshape(d, 2, DA_HEADS, DA_HEAD_DIM).transpose(0, 2, 1, 3).reshape(d, qk)

    return jnp.concatenate([regroup(w[:, :qk]), regroup(w[:, qk:2 * qk]), w[:, 2 * qk:]], axis=1).astype(BF16)


def _blocked_vt(v):
    n_exp, d = v.shape
    return v.reshape(n_exp // PEER_EBLK, PEER_EBLK, d).transpose(0, 2, 1).astype(BF16)


def _dt_weight(w_dt):
    d = w_dt.shape[0]
    t = w_dt.reshape(d, 2, SSD_HEADS)
    return jnp.pad(t, ((0, 0), (0, 0), (0, LANE - SSD_HEADS))).reshape(d, 2 * LANE).astype(BF16)


def _scan_params(dt_bias, a_log, d_skip):
    def lanes(t):
        return jnp.pad(t.reshape(2, SSD_HEADS), ((0, 0), (0, LANE - SSD_HEADS)))

    rows = jnp.stack([lanes(dt_bias), lanes(a_log)], axis=1)
    dlane = jnp.repeat(d_skip.reshape(2, SSD_HEADS), SSD_HEADDIM, axis=1)[:, None, :]
    return jnp.pad(rows, ((0, 0), (0, 6), (0, 0))).astype(F32), dlane.astype(F32)


def kernel(x, c, ctx, c_ctx, ada_w, ada_b, norm_mix_g, norm_ffn_g, ssd_in_w, ssd_conv_w, ssd_conv_b,
           ssd_dt_bias, ssd_a_log, ssd_d, ssd_norm_g, ssd_out_w, attn_qkv_w, attn_lambda_q, attn_lambda_k,
           attn_subln_g, attn_out_w, peer_q_w, peer_keys, peer_u, peer_v, final_norm_g):
    batch, s_lat, d = x.shape
    s_ctx = ctx.shape[1]
    depth = ada_w.shape[0]
    st = _Stream(batch, s_lat, s_ctx, d)

    xs = jnp.concatenate([x, ctx, jnp.zeros((batch, st.s_pad - st.s_real, d), F32)], axis=1)

    cc = jnp.zeros((8, d), F32).at[:batch].set(c).at[batch].set(c_ctx)
    mods = _ada_call(cc, ada_w, ada_b).reshape(depth, 8, 6, d)
    cos, sin = _rope_tables(st)
    zxw = SSD_D_INNER + SSD_CONV_DIM

    for i in range(depth):
        lat = mods[i, :batch]
        con = jnp.broadcast_to(mods[i, batch], (batch, 6, d))
        modtab = jnp.stack([con, lat], axis=1).reshape(batch * 2, 6, d)
        jm = i // 2
        if i % 2 == 0:
            w_in = ssd_in_w[jm]
            zx = _proj_mod(st, xs, norm_mix_g[i], modtab, w_in[:, :zxw].astype(BF16), 0, 1, 1024, BF16,
                           "ssd_in_proj")
            dtr = _proj_mod(st, xs, norm_mix_g[i], modtab, _dt_weight(w_in[:, zxw:]), 0, 1, LANE, F32,
                            "ssd_dt_proj")
            xc = _conv_call(st, zx, ssd_conv_w[jm], ssd_conv_b[jm])
            y2 = _scan_call(st, xc, dtr, *_scan_params(ssd_dt_bias[jm], ssd_a_log[jm], ssd_d[jm]))
            xs = _proj_gate_resid(st, y2, zx, ssd_norm_g[jm], ssd_out_w[jm].astype(BF16), xs, modtab, 2)
        else:
            lambda_init = 0.8 - 0.6 * math.exp(-0.3 * i)
            qkv = _proj_qkv(st, xs, norm_mix_g[i], modtab, _qkv_weight(attn_qkv_w[jm]), cos, sin, 0, 1)
            o = _attn_call(st, qkv, attn_lambda_q[jm], attn_lambda_k[jm], attn_subln_g[jm], lambda_init)
            xs = _proj_heads_resid(st, o, attn_out_w[jm].astype(BF16), xs, modtab, 2)
        scores, hq = _proj_scores(st, xs, norm_ffn_g[i], modtab, peer_q_w[i].astype(BF16), peer_keys[i], 3, 4)
        e1, tau, e2 = _gate_call(st, scores)
        xs = _expert_call(st, hq, peer_u[i].astype(BF16), _blocked_vt(peer_v[i]), e1, tau, scores, e2,
                          xs, modtab, 5)
    return _final_norm(st, xs, final_norm_g)
```

```python
import functools
import math

import numpy as np
import jax
import jax.numpy as jnp
from jax import lax
from jax.experimental import pallas as pl
from jax.experimental.pallas import tpu as pltpu

F32 = jnp.float32
BF16 = jnp.bfloat16

EPS = 1e-6
GRID_W = 64
ROPE_BASE = 10000.0

SSD_HEADDIM = 64
SSD_GROUPS = 4
SSD_HPG = 8
SSD_HEADS = SSD_GROUPS * SSD_HPG
SSD_STATE = 128
SSD_CONV_W = 5
SSD_CHUNK = 128
SSD_D_INNER = SSD_HEADS * SSD_HEADDIM
SSD_GN = SSD_GROUPS * SSD_STATE
SSD_CONV_DIM = SSD_D_INNER + 2 * SSD_GN

DA_HEADS = 8
DA_HEAD_DIM = 64

PEER_HEADS = 8
PEER_NKEYS = 128
PEER_TOPK = 16
PEER_EBLK = 1024
PEER_RB = 64

LANE = 128
TOK_TILE = 512
CONV_TILE = 256
ATT_TQ = 512
ATT_TK = 1024
ATT_TV = 512
ATT_SUM_ROWS = 16
VMEM_LIMIT = 56 * 1024 * 1024

NT_DIMS = (((1,), (1,)), ((), ()))


def _cparams(sem):
    return pltpu.CompilerParams(dimension_semantics=sem, vmem_limit_bytes=VMEM_LIMIT)


def _rms(xf, g):
    return xf * lax.rsqrt(jnp.mean(xf * xf, axis=-1, keepdims=True) + EPS) * g


def _silu(x):
    return x * (1.0 / (1.0 + jnp.exp(-x)))


def _ada_kernel(c_ref, w_ref, b_ref, o_ref):
    c = c_ref[...]
    o_ref[0] = jnp.dot(_silu(c), w_ref[0], preferred_element_type=F32,
                       precision=lax.Precision.HIGHEST) + b_ref[0]


def _ada_call(cc, ada_w, ada_b):
    depth, d, n = ada_w.shape
    tn = 1536
    return pl.pallas_call(
        _ada_kernel,
        grid=(depth, n // tn),
        in_specs=[pl.BlockSpec((8, d), lambda l, j: (0, 0)),
                  pl.BlockSpec((1, d, tn), lambda l, j: (l, 0, j)),
                  pl.BlockSpec((1, 1, tn), lambda l, j: (l, 0, j))],
        out_specs=pl.BlockSpec((1, 8, tn), lambda l, j: (l, 0, j)),
        out_shape=jax.ShapeDtypeStruct((depth, 8, n), F32),
        compiler_params=_cparams(("parallel", "parallel")),
        name="ada_mod",
    )(cc, ada_w, ada_b.reshape(depth, 1, n))


def _mod_prologue(x_ref, g_ref, mod_ref, hn_ref, sh, sc):
    m = mod_ref[0]
    hn = _rms(x_ref[0], g_ref[...]) * (1.0 + m[sc:sc + 1, :]) + m[sh:sh + 1, :]
    hn_ref[...] = hn.astype(BF16)


def _proj_mod_kernel(x_ref, g_ref, mod_ref, w_ref, o_ref, hn_ref, *, sh, sc):
    @pl.when(pl.program_id(2) == 0)
    def _():
        _mod_prologue(x_ref, g_ref, mod_ref, hn_ref, sh, sc)

    o_ref[0, 0] = jnp.dot(hn_ref[...], w_ref[...], preferred_element_type=F32).astype(o_ref.dtype)


def _rope_block(a, cos, sin, lane):
    fwd = pltpu.roll(a, LANE - 16, axis=1)
    bwd = pltpu.roll(a, 16, axis=1)
    rot = jnp.where(lane % 32 < 16, -fwd, bwd)
    return a * cos + rot * sin


def _proj_qkv_kernel(x_ref, g_ref, mod_ref, w_ref, cos_ref, sin_ref, o_ref, hn_ref, *, sh, sc):
    j = pl.program_id(2)

    @pl.when(j == 0)
    def _():
        _mod_prologue(x_ref, g_ref, mod_ref, hn_ref, sh, sc)

    acc = jnp.dot(hn_ref[...], w_ref[...], preferred_element_type=F32)
    tn = acc.shape[1]

    @pl.when(j < 2)
    def _():
        cos = cos_ref[...]
        sin = sin_ref[...]
        lane = lax.broadcasted_iota(jnp.int32, cos.shape, 1)
        scale = jnp.where(j == 0, DA_HEAD_DIM ** -0.5 * math.log2(math.e), 1.0).astype(F32)
        for cb in range(tn // LANE):
            a = acc[:, cb * LANE:(cb + 1) * LANE]
            o_ref[0, cb] = (_rope_block(a, cos, sin, lane) * scale).astype(o_ref.dtype)

    @pl.when(j >= 2)
    def _():
        for cb in range(tn // LANE):
            o_ref[0, cb] = acc[:, cb * LANE:(cb + 1) * LANE].astype(o_ref.dtype)


def _proj_gate_resid_kernel(yf_ref, yb_ref, z_ref, g_ref, w_ref, xr_ref, gate_ref, o_ref, hn_ref,
                            *, gi):
    @pl.when(pl.program_id(2) == 0)
    def _():
        z = jnp.concatenate([z_ref[c, 0] for c in range(z_ref.shape[0])], axis=1).astype(F32)
        u = (yf_ref[0, 0].astype(F32) + yb_ref[0, 0].astype(F32)) * _silu(z)
        hn_ref[...] = _rms(u, g_ref[...]).astype(BF16)

    acc = jnp.dot(hn_ref[...], w_ref[...], preferred_element_type=F32)
    o_ref[0] = xr_ref[0] + gate_ref[0][gi:gi + 1, :] * acc


def _proj_heads_resid_kernel(h_ref, w_ref, xr_ref, gate_ref, o_ref, *, gi):
    h = jnp.concatenate([h_ref[0, hh] for hh in range(h_ref.shape[1])], axis=1)
    acc = jnp.dot(h, w_ref[...], preferred_element_type=F32)
    o_ref[0] = xr_ref[0] + gate_ref[0][gi:gi + 1, :] * acc


def _proj_scores_kernel(x_ref, g_ref, mod_ref, w_ref, keys_ref, s_ref, hq_ref, hn_ref, *, sh, sc):
    @pl.when(pl.program_id(2) == 0)
    def _():
        _mod_prologue(x_ref, g_ref, mod_ref, hn_ref, sh, sc)
        hq_ref[0] = hn_ref[...]

    q = jnp.dot(hn_ref[...], w_ref[...], preferred_element_type=F32)
    half = PEER_NKEYS
    for c in range(2):
        s_hc = lax.dot_general(
            keys_ref[0, c], q[:, c * half:(c + 1) * half], NT_DIMS,
            preferred_element_type=F32, precision=lax.Precision.HIGHEST)
        for lb in range(s_hc.shape[1] // LANE):
            s_ref[0, 0, 0, c, lb] = s_hc[:, lb * LANE:(lb + 1) * LANE]


class _Stream:
    def __init__(self, batch, s_lat, s_ctx, d):
        assert s_lat % TOK_TILE == 0 and s_ctx % CONV_TILE == 0 and s_lat % GRID_W == 0
        self.batch, self.s_lat, self.s_ctx, self.d = batch, s_lat, s_ctx, d
        self.s_real = s_lat + s_ctx
        self.s_pad = -(-self.s_real // TOK_TILE) * TOK_TILE
        self.n_tok = self.s_pad // TOK_TILE
        self.n_lat_tok = s_lat // TOK_TILE

    def mod_row(self, b, i):
        return b * 2 + jnp.where(i < self.n_lat_tok, 1, 0)


def _mod_specs(st, d):
    return [pl.BlockSpec((1, TOK_TILE, d), lambda b, i, j: (b, i, 0)),
            pl.BlockSpec((1, d), lambda b, i, j: (0, 0)),
            pl.BlockSpec((1, 6, d), lambda b, i, j: (st.mod_row(b, i), 0, 0))]


def _proj_mod(st, x, g, modtab, w, sh, sc, tn, out_dtype, name):
    d, n = w.shape
    return pl.pallas_call(
        functools.partial(_proj_mod_kernel, sh=sh, sc=sc),
        grid=(st.batch, st.n_tok, n // tn),
        in_specs=_mod_specs(st, d) + [pl.BlockSpec((d, tn), lambda b, i, j: (0, j))],
        out_specs=pl.BlockSpec((1, 1, TOK_TILE, tn), lambda b, i, j: (j, b, i, 0)),
        out_shape=jax.ShapeDtypeStruct((n // tn, st.batch, st.s_pad, tn), out_dtype),
        scratch_shapes=[pltpu.VMEM((TOK_TILE, d), BF16)],
        compiler_params=_cparams(("parallel", "parallel", "arbitrary")),
        name=name,
    )(x, g.reshape(1, d), modtab, w)


def _proj_qkv(st, x, g, modtab, w, cos, sin, sh, sc):
    d, n = w.shape
    tn = 1024
    return pl.pallas_call(
        functools.partial(_proj_qkv_kernel, sh=sh, sc=sc),
        grid=(st.batch, st.n_tok, n // tn),
        in_specs=_mod_specs(st, d) + [
            pl.BlockSpec((d, tn), lambda b, i, j: (0, j)),
            pl.BlockSpec((TOK_TILE, LANE), lambda b, i, j: (i, 0)),
            pl.BlockSpec((TOK_TILE, LANE), lambda b, i, j: (i, 0))],
        out_specs=pl.BlockSpec((1, tn // LANE, TOK_TILE, LANE), lambda b, i, j: (b, j, i, 0)),
        out_shape=jax.ShapeDtypeStruct((st.batch, n // LANE, st.s_pad, LANE), BF16),
        scratch_shapes=[pltpu.VMEM((TOK_TILE, d), BF16)],
        compiler_params=_cparams(("parallel", "parallel", "arbitrary")),
        name="attn_qkv_rope",
    )(x, g.reshape(1, d), modtab, w, cos, sin)


def _proj_gate_resid(st, y2, zx, g, w, x, modtab, gi):
    k, n = w.shape
    tn = n
    return pl.pallas_call(
        functools.partial(_proj_gate_resid_kernel, gi=gi),
        grid=(st.batch, st.n_tok, n // tn),
        in_specs=[pl.BlockSpec((1, 1, TOK_TILE, k), lambda b, i, j: (0, b, i, 0)),
                  pl.BlockSpec((1, 1, TOK_TILE, k), lambda b, i, j: (1, b, i, 0)),
                  pl.BlockSpec((k // zx.shape[3], 1, TOK_TILE, zx.shape[3]), lambda b, i, j: (0, b, i, 0)),
                  pl.BlockSpec((1, k), lambda b, i, j: (0, 0)),
                  pl.BlockSpec((k, tn), lambda b, i, j: (0, j)),
                  pl.BlockSpec((1, TOK_TILE, tn), lambda b, i, j: (b, i, j)),
                  pl.BlockSpec((1, 6, tn), lambda b, i, j: (st.mod_row(b, i), 0, j))],
        out_specs=pl.BlockSpec((1, TOK_TILE, tn), lambda b, i, j: (b, i, j)),
        out_shape=jax.ShapeDtypeStruct((st.batch, st.s_pad, n), F32),
        scratch_shapes=[pltpu.VMEM((TOK_TILE, k), BF16)],
        compiler_params=_cparams(("parallel", "parallel", "arbitrary")),
        name="ssd_out_proj",
    )(y2, y2, zx, g.reshape(1, k), w, x, modtab)


def _proj_heads_resid(st, h, w, x, modtab, gi):
    k, n = w.shape
    tn = n
    return pl.pallas_call(
        functools.partial(_proj_heads_resid_kernel, gi=gi),
        grid=(st.batch, st.n_tok, n // tn),
        in_specs=[pl.BlockSpec((1, k // LANE, TOK_TILE, LANE), lambda b, i, j: (b, 0, i, 0)),
                  pl.BlockSpec((k, tn), lambda b, i, j: (0, j)),
                  pl.BlockSpec((1, TOK_TILE, tn), lambda b, i, j: (b, i, j)),
                  pl.BlockSpec((1, 6, tn), lambda b, i, j: (st.mod_row(b, i), 0, j))],
        out_specs=pl.BlockSpec((1, TOK_TILE, tn), lambda b, i, j: (b, i, j)),
        out_shape=jax.ShapeDtypeStruct((st.batch, st.s_pad, n), F32),
        compiler_params=_cparams(("parallel", "parallel", "arbitrary")),
        name="attn_out_proj",
    )(h, w, x, modtab)


def _proj_scores(st, x, g, modtab, w, keys, sh, sc):
    d, n = w.shape
    tn = 2 * PEER_NKEYS
    return pl.pallas_call(
        functools.partial(_proj_scores_kernel, sh=sh, sc=sc),
        grid=(st.batch, st.n_tok, PEER_HEADS),
        in_specs=_mod_specs(st, d) + [
            pl.BlockSpec((d, tn), lambda b, i, j: (0, j)),
            pl.BlockSpec((1, 2, PEER_NKEYS, PEER_NKEYS), lambda b, i, j: (j, 0, 0, 0))],
        out_specs=[pl.BlockSpec((1, 1, 1, 2, TOK_TILE // LANE, PEER_NKEYS, LANE),
                                lambda b, i, j: (b, i, j, 0, 0, 0, 0)),
                   pl.BlockSpec((1, TOK_TILE, d), lambda b, i, j: (b, i, 0))],
        out_shape=[jax.ShapeDtypeStruct(
            (st.batch, st.n_tok, PEER_HEADS, 2, TOK_TILE // LANE, PEER_NKEYS, LANE), F32),
                   jax.ShapeDtypeStruct((st.batch, st.s_pad, d), BF16)],
        scratch_shapes=[pltpu.VMEM((TOK_TILE, d), BF16)],
        compiler_params=_cparams(("parallel", "parallel", "arbitrary")),
        name="peer_scores",
    )(x, g.reshape(1, d), modtab, w, keys)


def _conv_kernel(x_ref, prev_ref, next_ref, w_ref, b_ref, o_ref, buf_ref, *, seg_starts, n_tiles):
    i = pl.program_id(1)
    pad = SSD_CONV_W // 2
    first = functools.reduce(jnp.logical_or, [i == s for s in seg_starts])
    last = functools.reduce(jnp.logical_or, [i == s - 1 for s in seg_starts[1:] + (n_tiles,)])
    t = x_ref.shape[2]
    buf_ref[8:8 + t, :] = x_ref[0, 0].astype(F32)
    buf_ref[0:8, :] = jnp.where(first, 0.0, prev_ref[0, 0].astype(F32)[8:16, :])
    buf_ref[8 + t:16 + t, :] = jnp.where(last, 0.0, next_ref[0, 0].astype(F32)[0:8, :])
    w = w_ref[...]
    acc = b_ref[...] + w[0:1, :] * buf_ref[8 - pad:8 - pad + t, :]
    for k in range(1, SSD_CONV_W):
        acc = acc + w[k:k + 1, :] * buf_ref[8 - pad + k:8 - pad + k + t, :]
    o_ref[0, 0] = _silu(acc).astype(o_ref.dtype)


def _conv_call(st, zx, conv_w, conv_b):
    cw = 1024
    n_cblk = SSD_CONV_DIM // cw
    off = SSD_D_INNER // cw
    n_tiles = st.s_pad // CONV_TILE
    hb = CONV_TILE // 16
    n_hblk = st.s_pad // 16
    seg_starts = (0, st.s_lat // CONV_TILE, st.s_real // CONV_TILE)
    w8 = jnp.zeros((8, SSD_CONV_DIM), F32).at[:SSD_CONV_W].set(conv_w)
    return pl.pallas_call(
        functools.partial(_conv_kernel, seg_starts=seg_starts, n_tiles=n_tiles),
        grid=(st.batch, n_tiles, n_cblk),
        in_specs=[pl.BlockSpec((1, 1, CONV_TILE, cw), lambda b, i, c: (off + c, b, i, 0)),
                  pl.BlockSpec((1, 1, 16, cw), lambda b, i, c: (off + c, b, jnp.maximum(i * hb - 1, 0), 0)),
                  pl.BlockSpec((1, 1, 16, cw),
                               lambda b, i, c: (off + c, b, jnp.minimum((i + 1) * hb, n_hblk - 1), 0)),
                  pl.BlockSpec((8, cw), lambda b, i, c: (0, c)),
                  pl.BlockSpec((1, cw), lambda b, i, c: (0, c))],
        out_specs=pl.BlockSpec((1, 1, CONV_TILE, cw), lambda b, i, c: (c, b, i, 0)),
        out_shape=jax.ShapeDtypeStruct((n_cblk, st.batch, st.s_pad, cw), BF16),
        scratch_shapes=[pltpu.VMEM((CONV_TILE + 16, cw), F32)],
        compiler_params=_cparams(("parallel", "parallel", "parallel")),
        name="ssd_conv",
    )(zx, zx, zx, w8, conv_b.reshape(1, SSD_CONV_DIM))


def _scan_kernel(xs_ref, bc_ref, dt_ref, tri_ref, par_ref, dlane_ref, y_ref, h_ref):
    @pl.when(pl.program_id(2) == 0)
    def _():
        h_ref[...] = jnp.zeros_like(h_ref)

    par = par_ref[0]
    raw = dt_ref[0, 0] + par[0:1, :]
    dt = jnp.maximum(raw, 0.0) + jnp.log1p(jnp.exp(-jnp.abs(raw)))
    a = -jnp.exp(par[1:2, :])
    dta = dt * a
    tri = tri_ref[0]
    cs = jnp.dot(tri, dta, preferred_element_type=F32, precision=lax.Precision.HIGHEST)
    tot = jnp.sum(dta, axis=0, keepdims=True)
    e_tot = jnp.exp(tot)
    w_end = jnp.exp(tot - cs) * dt
    cs_t = cs.T
    dt_t = dt.T
    w_end_t = w_end.T
    mask = tri > 0.5

    e_cs_t = jnp.exp(cs_t)
    n = SSD_STATE
    q = tri.shape[0]
    lane = lax.broadcasted_iota(jnp.int32, (q, LANE), 1)
    left = lane < SSD_HEADDIM
    eye = lax.broadcasted_iota(jnp.int32, (q, q), 0) == lax.broadcasted_iota(jnp.int32, (q, q), 1)
    lane1 = lax.broadcasted_iota(jnp.int32, (1, LANE), 1) < SSD_HEADDIM
    for g in range(SSD_GROUPS):
        bm = bc_ref[0, 0, :, g * n:(g + 1) * n]
        cm = bc_ref[0, 0, :, SSD_GN + g * n:SSD_GN + (g + 1) * n]
        bm_t = bm.astype(F32).T
        cb = lax.dot_general(cm, bm, NT_DIMS, preferred_element_type=F32)
        for r in range(0, SSD_HPG, 2):
            hds = (g * SSD_HPG + r, g * SSD_HPG + r + 1)
            pair = hds[0] // 2
            xb, xo = divmod(pair * LANE, xs_ref.shape[3])
            xp = xs_ref[xb, 0, :, xo:xo + LANE]
            zero = jnp.zeros_like(xp)
            x_blk = jnp.concatenate([jnp.where(left, xp, zero), jnp.where(left, zero, xp)], axis=0)
            h_prev = h_ref[pair]
            ch = jnp.dot(cm, h_prev.astype(BF16), preferred_element_type=F32)
            ch_blk = jnp.concatenate([jnp.where(left, ch, 0.0), jnp.where(left, 0.0, ch)], axis=0)
            w_parts, d_parts, u_parts = [], [], []
            for hd in hds:
                seg = cs[:, hd:hd + 1] - cs_t[hd:hd + 1, :]
                decay = jnp.where(mask, jnp.exp(seg), 0.0)
                w_parts.append(cb * decay * dt_t[hd:hd + 1, :])
                d_parts.append(jnp.where(eye, e_cs_t[hd:hd + 1, :], 0.0))
                u_parts.append(bm_t * w_end_t[hd:hd + 1, :])
            lhs = jnp.concatenate(w_parts + d_parts, axis=1).astype(BF16)
            rhs = jnp.concatenate([x_blk, ch_blk.astype(BF16)], axis=0)
            y = jnp.dot(lhs, rhs, preferred_element_type=F32)
            y = y + dlane_ref[0, :, pair * LANE:(pair + 1) * LANE] * xp.astype(F32)
            y_ref[0, 0, :, pair * LANE:(pair + 1) * LANE] = y.astype(y_ref.dtype)
            upd = jnp.dot(jnp.concatenate(u_parts, axis=1).astype(BF16), x_blk, preferred_element_type=F32)
            e_tot_pair = jnp.where(lane1, e_tot[:, hds[0]:hds[0] + 1], e_tot[:, hds[1]:hds[1] + 1])
            h_ref[pair] = e_tot_pair * h_prev + upd


def _scan_call(st, xc, dtr, par, dlane):
    q = SSD_CHUNK
    n_lat, n_ctx = st.s_lat // q, st.s_ctx // q
    n_real = n_lat + n_ctx
    n_all = st.s_pad // q
    cw = xc.shape[3]
    n_xblk = SSD_D_INNER // cw

    def chunk(d, s):
        fwd = jnp.where(s < n_ctx, n_lat + s, s - n_ctx)
        bwd = n_real - 1 - s
        return jnp.where(s < n_real, jnp.where(d == 0, fwd, bwd), s)

    idx = np.arange(q)
    tri = jnp.asarray(np.stack([idx[:, None] >= idx[None, :], idx[:, None] <= idx[None, :]]), F32)
    return pl.pallas_call(
        _scan_kernel,
        grid=(st.batch, 2, n_all),
        in_specs=[pl.BlockSpec((n_xblk, 1, q, cw), lambda b, d, s: (0, b, chunk(d, s), 0)),
                  pl.BlockSpec((1, 1, q, cw), lambda b, d, s: (n_xblk, b, chunk(d, s), 0)),
                  pl.BlockSpec((1, 1, q, LANE), lambda b, d, s: (d, b, chunk(d, s), 0)),
                  pl.BlockSpec((1, q, q), lambda b, d, s: (d, 0, 0)),
                  pl.BlockSpec((1, 8, LANE), lambda b, d, s: (d, 0, 0)),
                  pl.BlockSpec((1, 1, SSD_D_INNER), lambda b, d, s: (d, 0, 0))],
        out_specs=pl.BlockSpec((1, 1, q, SSD_D_INNER), lambda b, d, s: (d, b, chunk(d, s), 0)),
        out_shape=jax.ShapeDtypeStruct((2, st.batch, st.s_pad, SSD_D_INNER), BF16),
        scratch_shapes=[pltpu.VMEM((SSD_HEADS // 2, SSD_STATE, 2 * SSD_HEADDIM), F32)],
        compiler_params=_cparams(("parallel", "parallel", "arbitrary")),
        name="ssd_scan",
    )(xc, xc, dtr, tri, par, dlane)


def _attn_kernel(q_ref, k_ref, v_ref, lq_ref, lk_ref, g_ref, o_ref, qs_ref, vt_ref, m_ref, acc_ref,
                 s_ref, *, s_lat, s_ctx, lambda_init):
    qi = pl.program_id(2)
    tq = q_ref.shape[2]
    tk = ATT_TK
    tv = vt_ref.shape[2]
    n_sub = tk // tv
    hw = q_ref.shape[3]

    @pl.when(qi == 0)
    def _():
        def tr(c, carry):
            rows = v_ref[0, 0, pl.ds(pl.multiple_of(c * tv, tv), tv), :]
            vt_ref[c, 0:hw, :] = rows.astype(F32).T.astype(BF16)
            sub = lax.broadcasted_iota(jnp.int32, (vt_ref.shape[1] - hw, tv), 0)
            vt_ref[c, hw:, :] = jnp.where(sub == 0, 1.0, 0.0).astype(BF16)
            return carry

        lax.fori_loop(0, vt_ref.shape[0], tr, 0)

    qt = q_ref[0, 0].astype(F32).T
    row = lax.broadcasted_iota(jnp.int32, qt.shape, 0)
    qs_ref[:, 0:tq] = jnp.where(row < DA_HEAD_DIM, qt, 0.0).astype(BF16)
    qs_ref[:, tq:2 * tq] = jnp.where(row >= DA_HEAD_DIM, qt, 0.0).astype(BF16)
    m_ref[...] = jnp.full_like(m_ref, -jnp.inf)
    acc_ref[...] = jnp.zeros_like(acc_ref)

    def scores(k):
        return jnp.dot(k, qs_ref[...], preferred_element_type=F32)

    def update(read_s, vts):
        m_prev = m_ref[...]
        m_new = jnp.maximum(m_prev, jnp.max(read_s(), axis=0, keepdims=True))
        alpha = jnp.exp2(m_prev - m_new)
        pb = jnp.exp2(read_s() - m_new).astype(BF16)
        rows = pb.shape[0] // len(vts)
        pv = None
        for i, vt in enumerate(vts):
            d = jnp.dot(vt, pb[i * rows:(i + 1) * rows, :], preferred_element_type=F32)
            pv = d if pv is None else pv + d
        acc_ref[...] = alpha * acc_ref[...] + pv
        m_ref[...] = m_new

    n_lat = s_lat // tk

    def k_chunk(c):
        return k_ref[0, 0, pl.ds(pl.multiple_of(c * tk, tk), tk), :]

    @pl.when(qi < s_lat // tq)
    def _():
        s_ref[0] = scores(k_chunk(0))

        def half_step(c, slot):
            s_ref[1 - slot] = scores(k_chunk(jnp.minimum(c + 1, n_lat - 1)))
            update(lambda: s_ref[slot], [vt_ref[c * n_sub + i] for i in range(n_sub)])

        def body(cp, carry):
            half_step(2 * cp, 0)
            half_step(2 * cp + 1, 1)
            return carry

        lax.fori_loop(0, n_lat // 2, body, 0)
        if n_lat % 2:
            half_step(n_lat - 1, 0)

    s_ctx_keys = scores(k_ref[0, 0, s_lat:s_lat + s_ctx, :])
    update(lambda: s_ctx_keys, [vt_ref[s_lat // tv, :, 0:s_ctx]])

    lam = (jnp.exp(jnp.sum(lq_ref[0:1, :] * lk_ref[0:1, :], axis=-1, keepdims=True))
           - jnp.exp(jnp.sum(lq_ref[1:2, :] * lk_ref[1:2, :], axis=-1, keepdims=True)) + lambda_init)
    o_all = acc_ref[0:hw, :] / acc_ref[hw:hw + 1, :]
    o = (o_all[:, 0:tq] - lam * o_all[:, tq:2 * tq]).T
    o_ref[0, 0] = (_rms(o, g_ref[...]) * (1.0 - lambda_init)).astype(o_ref.dtype)


def _attn_call(st, qkv, lam_q, lam_k, subln_g, lambda_init):
    tq = ATT_TQ
    nq = st.s_pad // tq
    hw = 2 * DA_HEAD_DIM
    nh = DA_HEADS
    return pl.pallas_call(
        functools.partial(_attn_kernel, s_lat=st.s_lat, s_ctx=st.s_ctx, lambda_init=lambda_init),
        grid=(st.batch, nh, nq),
        in_specs=[pl.BlockSpec((1, 1, tq, hw), lambda b, h, i: (b, h, i, 0)),
                  pl.BlockSpec((1, 1, st.s_pad, hw), lambda b, h, i: (b, nh + h, 0, 0)),
                  pl.BlockSpec((1, 1, st.s_pad, hw), lambda b, h, i: (b, 2 * nh + h, 0, 0)),
                  pl.BlockSpec((2, DA_HEAD_DIM), lambda b, h, i: (0, 0)),
                  pl.BlockSpec((2, DA_HEAD_DIM), lambda b, h, i: (0, 0)),
                  pl.BlockSpec((1, hw), lambda b, h, i: (0, 0))],
        out_specs=pl.BlockSpec((1, 1, tq, hw), lambda b, h, i: (b, h, i, 0)),
        out_shape=jax.ShapeDtypeStruct((st.batch, nh, st.s_pad, hw), BF16),
        scratch_shapes=[pltpu.VMEM((hw, 2 * tq), BF16),
                        pltpu.VMEM((st.s_pad // ATT_TV, hw + ATT_SUM_ROWS, ATT_TV), BF16),
                        pltpu.VMEM((1, 2 * tq), F32),
                        pltpu.VMEM((hw + ATT_SUM_ROWS, 2 * tq), F32),
                        pltpu.VMEM((2, ATT_TK, 2 * tq), F32)],
        compiler_params=_cparams(("parallel", "parallel", "arbitrary")),
        name="diff_attn",
    )(qkv, qkv, qkv, lam_q, lam_k, subln_g.reshape(1, hw))


def _extract_top(s, dst_ref, n):
    w = s
    for k in range(n):
        m = jnp.max(w, axis=0, keepdims=True)
        dst_ref[k:k + 1, :] = m
        if k + 1 < n:
            w = jnp.where(w == m, -jnp.inf, w)


def _gate_kernel(s_ref, e1_ref, tau_ref, e2_ref, a_ref, b_ref):
    k = PEER_TOPK
    n_lb = s_ref.shape[4]
    s1 = jnp.concatenate([s_ref[0, 0, 0, 0, lb] for lb in range(n_lb)], axis=1)
    s2 = jnp.concatenate([s_ref[0, 0, 0, 1, lb] for lb in range(n_lb)], axis=1)
    _extract_top(s1, a_ref, k)
    _extract_top(s2, b_ref, k)
    a = a_ref[...]
    b = b_ref[...]
    cand = jnp.concatenate(
        [a[0:1, :] + b] + [a[i:i + 1, :] + b[0:8, :] for i in range(1, 8)] + [a[8:16, :] + b[0:1, :]],
        axis=0)
    w = cand
    tau = None
    for it in range(k):
        tau = jnp.max(w, axis=0, keepdims=True)
        if it + 1 < k:
            w = jnp.where(w == tau, -jnp.inf, w)
    top = a[0:1, :] + b[0:1, :]
    z = jnp.sum(jnp.where(cand >= tau, jnp.exp(cand - top), 0.0), axis=0, keepdims=True)
    blk_shape = e1_ref.shape[2:3] + e1_ref.shape[4:]
    e1_ref[0, 0, :, 0] = jnp.exp(s1 - a[0:1, :]).reshape(blk_shape)
    e2 = jnp.exp(s2 - b[0:1, :]) / z
    for lb in range(n_lb):
        e2_ref[0, 0, 0, lb] = e2[:, lb * LANE:(lb + 1) * LANE]
    tau_ref[0, 0, :, 0] = (tau - s1).reshape(blk_shape)


def _gate_call(st, scores):
    nk = PEER_NKEYS
    tt = TOK_TILE
    rows = PEER_EBLK // nk
    n_blk = nk // rows
    shp_blk = jax.ShapeDtypeStruct((st.batch, st.n_tok, n_blk, PEER_HEADS, rows, tt), F32)
    spec_blk = pl.BlockSpec((1, 1, n_blk, 1, rows, tt), lambda b, i, h: (b, i, 0, h, 0, 0))
    n_lb = tt // LANE
    shp_e2 = jax.ShapeDtypeStruct((st.batch, st.n_tok, PEER_HEADS, n_lb, nk, LANE), F32)
    spec_e2 = pl.BlockSpec((1, 1, 1, n_lb, nk, LANE), lambda b, i, h: (b, i, h, 0, 0, 0))
    return pl.pallas_call(
        _gate_kernel,
        grid=(st.batch, st.n_tok, PEER_HEADS),
        in_specs=[pl.BlockSpec((1, 1, 1, 2, n_lb, nk, LANE), lambda b, i, h: (b, i, h, 0, 0, 0, 0))],
        out_specs=[spec_blk, spec_blk, spec_e2],
        out_shape=[shp_blk, shp_blk, shp_e2],
        scratch_shapes=[pltpu.VMEM((PEER_TOPK, TOK_TILE), F32), pltpu.VMEM((PEER_TOPK, TOK_TILE), F32)],
        compiler_params=_cparams(("parallel", "parallel", "parallel")),
        name="peer_gates",
    )(scores)


def _expert_kernel(hq_ref, u0_ref, un_ref, vt_ref, e1_ref, tau_ref, s2_ref, e2_ref, xr_ref, gate_ref, o_ref,
                   acc_ref, act_ref, p_ref, hqt_ref, *, gi):
    e = pl.program_id(2)
    nk = PEER_NKEYS
    tt = hq_ref.shape[1]
    half = tt // 2
    sqrt_half = np.float32(math.sqrt(0.5))

    def put_act(slot, r0, r1, c0, c1, val):
        for lb in range(c0 // LANE, c1 // LANE):
            act_ref[slot, lb, r0:r1, :] = val[:, lb * LANE - c0:(lb + 1) * LANE - c0]

    @pl.when(e == 0)
    def _():
        acc_ref[...] = jnp.zeros_like(acc_ref)
        hqt_ref[...] = hq_ref[0].astype(F32).T.astype(BF16)
        put_act(0, 0, PEER_EBLK, 0, tt, jnp.dot(u0_ref[...], hqt_ref[...], preferred_element_type=F32))

    def gate_tile(cur, a_list, t0):
        cols = slice(t0, t0 + LANE)
        lb = t0 // LANE
        e1 = [[e1_ref[0, 0, 0, h, a:a + 1, cols] for h in range(PEER_HEADS)] for a in a_list]
        tau = [[tau_ref[0, 0, 0, h, a:a + 1, cols] for h in range(PEER_HEADS)] for a in a_list]
        for r0 in range(0, nk, PEER_RB):
            rows = slice(r0, r0 + PEER_RB)
            g = [None] * len(a_list)
            for h in range(PEER_HEADS):
                s2 = s2_ref[0, 0, h, 0, lb, rows, :]
                e2 = e2_ref[0, 0, h, lb, rows, :]
                for ai in range(len(a_list)):
                    term = e1[ai][h] * jnp.where(s2 >= tau[ai][h], e2, 0.0)
                    g[ai] = term if g[ai] is None else g[ai] + term
            for ai, a in enumerate(a_list):
                x = act_ref[cur, lb, a * nk + r0:a * nk + r0 + PEER_RB, :]
                gelu = 0.5 * x * (1.0 + lax.erf(x * sqrt_half))
                p_ref[lb, a * nk + r0:a * nk + r0 + PEER_RB, :] = (g[ai] * gelu).astype(BF16)

    def stages(cur, nxt):
        rows_per = 2 * nk
        act_rows = 2 * rows_per
        for c0 in (0, half):
            for r0 in range(0, PEER_EBLK, rows_per):
                if r0 % act_rows == 0:
                    put_act(nxt, r0, r0 + act_rows, c0, c0 + half, jnp.dot(
                        un_ref[r0:r0 + act_rows, :], hqt_ref[:, c0:c0 + half],
                        preferred_element_type=F32))
                a_list = list(range(r0 // nk, (r0 + rows_per) // nk))
                for t0 in range(c0, c0 + half, LANE):
                    gate_tile(cur, a_list, t0)
                p_blk = jnp.concatenate([p_ref[lb, r0:r0 + rows_per, :]
                                         for lb in range(c0 // LANE, (c0 + half) // LANE)], axis=1)
                out = jnp.dot(vt_ref[0, :, r0:r0 + rows_per], p_blk, preferred_element_type=F32)
                for j, lb in enumerate(range(c0 // LANE, (c0 + half) // LANE)):
                    acc_ref[lb] += out[:, j * LANE:(j + 1) * LANE]

    for parity in range(2):
        pl.when(e % 2 == parity)(functools.partial(stages, parity, 1 - parity))

    @pl.when(e == pl.num_programs(2) - 1)
    def _():
        acc = jnp.concatenate([acc_ref[lb] for lb in range(acc_ref.shape[0])], axis=1)
        o_ref[0] = xr_ref[0] + gate_ref[0][gi:gi + 1, :] * acc.T


def _expert_call(st, hq, u, vt, e1, tau, scores, e2, x, modtab, gi):
    n_exp, d = u.shape
    nk = PEER_NKEYS
    rows = PEER_EBLK // nk
    tt = TOK_TILE
    n_blk = n_exp // PEER_EBLK
    return pl.pallas_call(
        functools.partial(_expert_kernel, gi=gi),
        grid=(st.batch, st.n_tok, n_blk),
        in_specs=[pl.BlockSpec((1, tt, d), lambda b, i, e: (b, i, 0)),
                  pl.BlockSpec((PEER_EBLK, d), lambda b, i, e: (0, 0)),
                  pl.BlockSpec((PEER_EBLK, d), lambda b, i, e: (jnp.minimum(e + 1, n_blk - 1), 0)),
                  pl.BlockSpec((1, d, PEER_EBLK), lambda b, i, e: (e, 0, 0)),
                  pl.BlockSpec((1, 1, 1, PEER_HEADS, rows, tt), lambda b, i, e: (b, i, e, 0, 0, 0)),
                  pl.BlockSpec((1, 1, 1, PEER_HEADS, rows, tt), lambda b, i, e: (b, i, e, 0, 0, 0)),
                  pl.BlockSpec((1, 1, PEER_HEADS, 1, tt // LANE, nk, LANE),
                               lambda b, i, e: (b, i, 0, 1, 0, 0, 0)),
                  pl.BlockSpec((1, 1, PEER_HEADS, tt // LANE, nk, LANE), lambda b, i, e: (b, i, 0, 0, 0, 0)),
                  pl.BlockSpec((1, tt, d), lambda b, i, e: (b, i, 0)),
                  pl.BlockSpec((1, 6, d), lambda b, i, e: (st.mod_row(b, i), 0, 0))],
        out_specs=pl.BlockSpec((1, tt, d), lambda b, i, e: (b, i, 0)),
        out_shape=jax.ShapeDtypeStruct((st.batch, st.s_pad, d), F32),
        scratch_shapes=[pltpu.VMEM((tt // LANE, d, LANE), F32),
                        pltpu.VMEM((2, tt // LANE, PEER_EBLK, LANE), F32),
                        pltpu.VMEM((tt // LANE, PEER_EBLK, LANE), BF16),
                        pltpu.VMEM((d, tt), BF16)],
        compiler_params=_cparams(("parallel", "parallel", "arbitrary")),
        name="peer_experts",
    )(hq, u, u, vt, e1, tau, scores, e2, x, modtab)


def _final_norm_kernel(x_ref, g_ref, o_ref):
    o_ref[0] = _rms(x_ref[0], g_ref[...])


def _final_norm(st, x, g):
    d = st.d
    return pl.pallas_call(
        _final_norm_kernel,
        grid=(st.batch, st.n_lat_tok),
        in_specs=[pl.BlockSpec((1, TOK_TILE, d), lambda b, i: (b, i, 0)),
                  pl.BlockSpec((1, d), lambda b, i: (0, 0))],
        out_specs=pl.BlockSpec((1, TOK_TILE, d), lambda b, i: (b, i, 0)),
        out_shape=jax.ShapeDtypeStruct((st.batch, st.s_lat, d), F32),
        compiler_params=_cparams(("parallel", "parallel")),
        name="final_norm",
    )(x, g.reshape(1, d))


def _rope_tables(st):
    rows = st.s_lat // GRID_W
    row = jnp.repeat(jnp.arange(rows, dtype=F32), GRID_W)
    col = jnp.tile(jnp.arange(GRID_W, dtype=F32), rows)
    half = DA_HEAD_DIM // 2
    freqs = ROPE_BASE ** (-jnp.arange(0, half, 2, dtype=F32) / half)
    ang_r = row[:, None] * freqs
    ang_c = col[:, None] * freqs
    ang = jnp.concatenate([ang_r, ang_r, ang_c, ang_c] * 2, axis=-1)
    extra = st.s_pad - st.s_lat
    cos = jnp.concatenate([jnp.cos(ang), jnp.ones((extra, LANE), F32)], axis=0)
    sin = jnp.concatenate([jnp.sin(ang), jnp.zeros((extra, LANE), F32)], axis=0)
    return cos, sin


def _qkv_weight(w):
    d = w.shape[0]
    qk = DA_HEADS * DA_HEAD_DIM * 2

    def regroup(t):
        return t.reshape(d, 2, DA_HEADS, DA_HEAD_DIM).transpose(0, 2, 1, 3).reshape(d, qk)

    return jnp.concatenate([regroup(w[:, :qk]), regroup(w[:, qk:2 * qk]), w[:, 2 * qk:]], axis=1).astype(BF16)


def _blocked_vt(v):
    n_exp, d = v.shape
    return v.reshape(n_exp // PEER_EBLK, PEER_EBLK, d).transpose(0, 2, 1).astype(BF16)


def _dt_weight(w_dt):
    d = w_dt.shape[0]
    t = w_dt.reshape(d, 2, SSD_HEADS)
    return jnp.pad(t, ((0, 0), (0, 0), (0, LANE - SSD_HEADS))).reshape(d, 2 * LANE).astype(BF16)


def _scan_params(dt_bias, a_log, d_skip):
    def lanes(t):
        return jnp.pad(t.reshape(2, SSD_HEADS), ((0, 0), (0, LANE - SSD_HEADS)))

    rows = jnp.stack([lanes(dt_bias), lanes(a_log)], axis=1)
    dlane = jnp.repeat(d_skip.reshape(2, SSD_HEADS), SSD_HEADDIM, axis=1)[:, None, :]
    return jnp.pad(rows, ((0, 0), (0, 6), (0, 0))).astype(F32), dlane.astype(F32)


def kernel(x, c, ctx, c_ctx, ada_w, ada_b, norm_mix_g, norm_ffn_g, ssd_in_w, ssd_conv_w, ssd_conv_b,
           ssd_dt_bias, ssd_a_log, ssd_d, ssd_norm_g, ssd_out_w, attn_qkv_w, attn_lambda_q, attn_lambda_k,
           attn_subln_g, attn_out_w, peer_q_w, peer_keys, peer_u, peer_v, final_norm_g):
    batch, s_lat, d = x.shape
    s_ctx = ctx.shape[1]
    depth = ada_w.shape[0]
    st = _Stream(batch, s_lat, s_ctx, d)

    xs = jnp.concatenate([x, ctx, jnp.zeros((batch, st.s_pad - st.s_real, d), F32)], axis=1)

    cc = jnp.zeros((8, d), F32).at[:batch].set(c).at[batch].set(c_ctx)
    mods = _ada_call(cc, ada_w, ada_b).reshape(depth, 8, 6, d)
    cos, sin = _rope_tables(st)
    zxw = SSD_D_INNER + SSD_CONV_DIM

    for i in range(depth):
        lat = mods[i, :batch]
        con = jnp.broadcast_to(mods[i, batch], (batch, 6, d))
        modtab = jnp.stack([con, lat], axis=1).reshape(batch * 2, 6, d)
        jm = i // 2
        if i % 2 == 0:
            w_in = ssd_in_w[jm]
            zx = _proj_mod(st, xs, norm_mix_g[i], modtab, w_in[:, :zxw].astype(BF16), 0, 1, 1024, BF16,
                           "ssd_in_proj")
            dtr = _proj_mod(st, xs, norm_mix_g[i], modtab, _dt_weight(w_in[:, zxw:]), 0, 1, LANE, F32,
                            "ssd_dt_proj")
            xc = _conv_call(st, zx, ssd_conv_w[jm], ssd_conv_b[jm])
            y2 = _scan_call(st, xc, dtr, *_scan_params(ssd_dt_bias[jm], ssd_a_log[jm], ssd_d[jm]))
            xs = _proj_gate_resid(st, y2, zx, ssd_norm_g[jm], ssd_out_w[jm].astype(BF16), xs, modtab, 2)
        else:
            lambda_init = 0.8 - 0.6 * math.exp(-0.3 * i)
            qkv = _proj_qkv(st, xs, norm_mix_g[i], modtab, _qkv_weight(attn_qkv_w[jm]), cos, sin, 0, 1)
            o = _attn_call(st, qkv, attn_lambda_q[jm], attn_lambda_k[jm], attn_subln_g[jm], lambda_init)
            xs = _proj_heads_resid(st, o, attn_out_w[jm].astype(BF16), xs, modtab, 2)
        scores, hq = _proj_scores(st, xs, norm_ffn_g[i], modtab, peer_q_w[i].astype(BF16), peer_keys[i], 3, 4)
        e1, tau, e2 = _gate_call(st, scores)
        xs = _expert_call(st, hq, peer_u[i].astype(BF16), _blocked_vt(peer_v[i]), e1, tau, scores, e2,
                          xs, modtab, 5)
    return _final_norm(st, xs, final_norm_g)
```

```python
import functools
import math

import numpy as np
import jax
import jax.numpy as jnp
from jax import lax
from jax.experimental import pallas as pl
from jax.experimental.pallas import tpu as pltpu

F32 = jnp.float32
BF16 = jnp.bfloat16

EPS = 1e-6
GRID_W = 64
ROPE_BASE = 10000.0

SSD_HEADDIM = 64
SSD_GROUPS = 4
SSD_HPG = 8
SSD_HEADS = SSD_GROUPS * SSD_HPG
SSD_STATE = 128
SSD_CONV_W = 5
SSD_CHUNK = 128
SSD_D_INNER = SSD_HEADS * SSD_HEADDIM
SSD_GN = SSD_GROUPS * SSD_STATE
SSD_CONV_DIM = SSD_D_INNER + 2 * SSD_GN

DA_HEADS = 8
DA_HEAD_DIM = 64

PEER_HEADS = 8
PEER_NKEYS = 128
PEER_TOPK = 16
PEER_EBLK = 1024
PEER_RB = 64

LANE = 128
TOK_TILE = 512
CONV_TILE = 256
ATT_TQ = 512
ATT_TK = 1024
ATT_TV = 512
ATT_SUM_ROWS = 16
VMEM_LIMIT = 56 * 1024 * 1024

NT_DIMS = (((1,), (1,)), ((), ()))


def _cparams(sem):
    return pltpu.CompilerParams(dimension_semantics=sem, vmem_limit_bytes=VMEM_LIMIT)


def _rms(xf, g):
    return xf * lax.rsqrt(jnp.mean(xf * xf, axis=-1, keepdims=True) + EPS) * g


def _silu(x):
    return x * (1.0 / (1.0 + jnp.exp(-x)))


def _ada_kernel(c_ref, w_ref, b_ref, o_ref):
    c = c_ref[...]
    o_ref[0] = jnp.dot(_silu(c), w_ref[0], preferred_element_type=F32,
                       precision=lax.Precision.HIGHEST) + b_ref[0]


def _ada_call(cc, ada_w, ada_b):
    depth, d, n = ada_w.shape
    tn = 1536
    return pl.pallas_call(
        _ada_kernel,
        grid=(depth, n // tn),
        in_specs=[pl.BlockSpec((8, d), lambda l, j: (0, 0)),
                  pl.BlockSpec((1, d, tn), lambda l, j: (l, 0, j)),
                  pl.BlockSpec((1, 1, tn), lambda l, j: (l, 0, j))],
        out_specs=pl.BlockSpec((1, 8, tn), lambda l, j: (l, 0, j)),
        out_shape=jax.ShapeDtypeStruct((depth, 8, n), F32),
        compiler_params=_cparams(("parallel", "parallel")),
        name="ada_mod",
    )(cc, ada_w, ada_b.reshape(depth, 1, n))


def _mod_prologue(x_ref, g_ref, mod_ref, hn_ref, sh, sc):
    m = mod_ref[0]
    hn = _rms(x_ref[0], g_ref[...]) * (1.0 + m[sc:sc + 1, :]) + m[sh:sh + 1, :]
    hn_ref[...] = hn.astype(BF16)


def _proj_mod_kernel(x_ref, g_ref, mod_ref, w_ref, o_ref, hn_ref, *, sh, sc):
    @pl.when(pl.program_id(2) == 0)
    def _():
        _mod_prologue(x_ref, g_ref, mod_ref, hn_ref, sh, sc)

    o_ref[0, 0] = jnp.dot(hn_ref[...], w_ref[...], preferred_element_type=F32).astype(o_ref.dtype)


def _rope_block(a, cos, sin, lane):
    fwd = pltpu.roll(a, LANE - 16, axis=1)
    bwd = pltpu.roll(a, 16, axis=1)
    rot = jnp.where(lane % 32 < 16, -fwd, bwd)
    return a * cos + rot * sin


def _proj_qkv_kernel(x_ref, g_ref, mod_ref, w_ref, cos_ref, sin_ref, o_ref, hn_ref, *, sh, sc):
    j = pl.program_id(2)

    @pl.when(j == 0)
    def _():
        _mod_prologue(x_ref, g_ref, mod_ref, hn_ref, sh, sc)

    acc = jnp.dot(hn_ref[...], w_ref[...], preferred_element_type=F32)
    tn = acc.shape[1]

    @pl.when(j < 2)
    def _():
        cos = cos_ref[...]
        sin = sin_ref[...]
        lane = lax.broadcasted_iota(jnp.int32, cos.shape, 1)
        scale = jnp.where(j == 0, DA_HEAD_DIM ** -0.5 * math.log2(math.e), 1.0).astype(F32)
        for cb in range(tn // LANE):
            a = acc[:, cb * LANE:(cb + 1) * LANE]
            o_ref[0, cb] = (_rope_block(a, cos, sin, lane) * scale).astype(o_ref.dtype)

    @pl.when(j >= 2)
    def _():
        for cb in range(tn // LANE):
            o_ref[0, cb] = acc[:, cb * LANE:(cb + 1) * LANE].astype(o_ref.dtype)


def _proj_gate_resid_kernel(yf_ref, yb_ref, z_ref, g_ref, w_ref, xr_ref, gate_ref, o_ref, hn_ref,
                            *, gi):
    @pl.when(pl.program_id(2) == 0)
    def _():
        z = jnp.concatenate([z_ref[c, 0] for c in range(z_ref.shape[0])], axis=1).astype(F32)
        u = (yf_ref[0, 0].astype(F32) + yb_ref[0, 0].astype(F32)) * _silu(z)
        hn_ref[...] = _rms(u, g_ref[...]).astype(BF16)

    acc = jnp.dot(hn_ref[...], w_ref[...], preferred_element_type=F32)
    o_ref[0] = xr_ref[0] + gate_ref[0][gi:gi + 1, :] * acc


def _proj_heads_resid_kernel(h_ref, w_ref, xr_ref, gate_ref, o_ref, *, gi):
    h = jnp.concatenate([h_ref[0, hh] for hh in range(h_ref.shape[1])], axis=1)
    acc = jnp.dot(h, w_ref[...], preferred_element_type=F32)
    o_ref[0] = xr_ref[0] + gate_ref[0][gi:gi + 1, :] * acc


def _proj_scores_kernel(x_ref, g_ref, mod_ref, w_ref, keys_ref, s_ref, hq_ref, hn_ref, *, sh, sc):
    @pl.when(pl.program_id(2) == 0)
    def _():
        _mod_prologue(x_ref, g_ref, mod_ref, hn_ref, sh, sc)
        hq_ref[0] = hn_ref[...]

    q = jnp.dot(hn_ref[...], w_ref[...], preferred_element_type=F32)
    half = PEER_NKEYS
    for c in range(2):
        s_hc = lax.dot_general(
            keys_ref[0, c], q[:, c * half:(c + 1) * half], NT_DIMS,
            preferred_element_type=F32, precision=lax.Precision.HIGHEST)
        for lb in range(s_hc.shape[1] // LANE):
            s_ref[0, 0, 0, c, lb] = s_hc[:, lb * LANE:(lb + 1) * LANE]


class _Stream:
    def __init__(self, batch, s_lat, s_ctx, d):
        assert s_lat % TOK_TILE == 0 and s_ctx % CONV_TILE == 0 and s_lat % GRID_W == 0
        self.batch, self.s_lat, self.s_ctx, self.d = batch, s_lat, s_ctx, d
        self.s_real = s_lat + s_ctx
        self.s_pad = -(-self.s_real // TOK_TILE) * TOK_TILE
        self.n_tok = self.s_pad // TOK_TILE
        self.n_lat_tok = s_lat // TOK_TILE

    def mod_row(self, b, i):
        return b * 2 + jnp.where(i < self.n_lat_tok, 1, 0)


def _mod_specs(st, d):
    return [pl.BlockSpec((1, TOK_TILE, d), lambda b, i, j: (b, i, 0)),
            pl.BlockSpec((1, d), lambda b, i, j: (0, 0)),
            pl.BlockSpec((1, 6, d), lambda b, i, j: (st.mod_row(b, i), 0, 0))]


def _proj_mod(st, x, g, modtab, w, sh, sc, tn, out_dtype, name):
    d, n = w.shape
    return pl.pallas_call(
        functools.partial(_proj_mod_kernel, sh=sh, sc=sc),
        grid=(st.batch, st.n_tok, n // tn),
        in_specs=_mod_specs(st, d) + [pl.BlockSpec((d, tn), lambda b, i, j: (0, j))],
        out_specs=pl.BlockSpec((1, 1, TOK_TILE, tn), lambda b, i, j: (j, b, i, 0)),
        out_shape=jax.ShapeDtypeStruct((n // tn, st.batch, st.s_pad, tn), out_dtype),
        scratch_shapes=[pltpu.VMEM((TOK_TILE, d), BF16)],
        compiler_params=_cparams(("parallel", "parallel", "arbitrary")),
        name=name,
    )(x, g.reshape(1, d), modtab, w)


def _proj_qkv(st, x, g, modtab, w, cos, sin, sh, sc):
    d, n = w.shape
    tn = 1024
    return pl.pallas_call(
        functools.partial(_proj_qkv_kernel, sh=sh, sc=sc),
        grid=(st.batch, st.n_tok, n // tn),
        in_specs=_mod_specs(st, d) + [
            pl.BlockSpec((d, tn), lambda b, i, j: (0, j)),
            pl.BlockSpec((TOK_TILE, LANE), lambda b, i, j: (i, 0)),
            pl.BlockSpec((TOK_TILE, LANE), lambda b, i, j: (i, 0))],
        out_specs=pl.BlockSpec((1, tn // LANE, TOK_TILE, LANE), lambda b, i, j: (b, j, i, 0)),
        out_shape=jax.ShapeDtypeStruct((st.batch, n // LANE, st.s_pad, LANE), BF16),
        scratch_shapes=[pltpu.VMEM((TOK_TILE, d), BF16)],
        compiler_params=_cparams(("parallel", "parallel", "arbitrary")),
        name="attn_qkv_rope",
    )(x, g.reshape(1, d), modtab, w, cos, sin)


def _proj_gate_resid(st, y2, zx, g, w, x, modtab, gi):
    k, n = w.shape
    tn = n
    return pl.pallas_call(
        functools.partial(_proj_gate_resid_kernel, gi=gi),
        grid=(st.batch, st.n_tok, n // tn),
        in_specs=[pl.BlockSpec((1, 1, TOK_TILE, k), lambda b, i, j: (0, b, i, 0)),
                  pl.BlockSpec((1, 1, TOK_TILE, k), lambda b, i, j: (1, b, i, 0)),
                  pl.BlockSpec((k // zx.shape[3], 1, TOK_TILE, zx.shape[3]), lambda b, i, j: (0, b, i, 0)),
                  pl.BlockSpec((1, k), lambda b, i, j: (0, 0)),
                  pl.BlockSpec((k, tn), lambda b, i, j: (0, j)),
                  pl.BlockSpec((1, TOK_TILE, tn), lambda b, i, j: (b, i, j)),
                  pl.BlockSpec((1, 6, tn), lambda b, i, j: (st.mod_row(b, i), 0, j))],
        out_specs=pl.BlockSpec((1, TOK_TILE, tn), lambda b, i, j: (b, i, j)),
        out_shape=jax.ShapeDtypeStruct((st.batch, st.s_pad, n), F32),
        scratch_shapes=[pltpu.VMEM((TOK_TILE, k), BF16)],
        compiler_params=_cparams(("parallel", "parallel", "arbitrary")),
        name="ssd_out_proj",
    )(y2, y2, zx, g.reshape(1, k), w, x, modtab)


def _proj_heads_resid(st, h, w, x, modtab, gi):
    k, n = w.shape
    tn = n
    return pl.pallas_call(
        functools.partial(_proj_heads_resid_kernel, gi=gi),
        grid=(st.batch, st.n_tok, n // tn),
        in_specs=[pl.BlockSpec((1, k // LANE, TOK_TILE, LANE), lambda b, i, j: (b, 0, i, 0)),
                  pl.BlockSpec((k, tn), lambda b, i, j: (0, j)),
                  pl.BlockSpec((1, TOK_TILE, tn), lambda b, i, j: (b, i, j)),
                  pl.BlockSpec((1, 6, tn), lambda b, i, j: (st.mod_row(b, i), 0, j))],
        out_specs=pl.BlockSpec((1, TOK_TILE, tn), lambda b, i, j: (b, i, j)),
        out_shape=jax.ShapeDtypeStruct((st.batch, st.s_pad, n), F32),
        compiler_params=_cparams(("parallel", "parallel", "arbitrary")),
        name="attn_out_proj",
    )(h, w, x, modtab)


def _proj_scores(st, x, g, modtab, w, keys, sh, sc):
    d, n = w.shape
    tn = 2 * PEER_NKEYS
    return pl.pallas_call(
        functools.partial(_proj_scores_kernel, sh=sh, sc=sc),
        grid=(st.batch, st.n_tok, PEER_HEADS),
        in_specs=_mod_specs(st, d) + [
            pl.BlockSpec((d, tn), lambda b, i, j: (0, j)),
            pl.BlockSpec((1, 2, PEER_NKEYS, PEER_NKEYS), lambda b, i, j: (j, 0, 0, 0))],
        out_specs=[pl.BlockSpec((1, 1, 1, 2, TOK_TILE // LANE, PEER_NKEYS, LANE),
                                lambda b, i, j: (b, i, j, 0, 0, 0, 0)),
                   pl.BlockSpec((1, TOK_TILE, d), lambda b, i, j: (b, i, 0))],
        out_shape=[jax.ShapeDtypeStruct(
            (st.batch, st.n_tok, PEER_HEADS, 2, TOK_TILE // LANE, PEER_NKEYS, LANE), F32),
                   jax.ShapeDtypeStruct((st.batch, st.s_pad, d), BF16)],
        scratch_shapes=[pltpu.VMEM((TOK_TILE, d), BF16)],
        compiler_params=_cparams(("parallel", "parallel", "arbitrary")),
        name="peer_scores",
    )(x, g.reshape(1, d), modtab, w, keys)


def _conv_kernel(x_ref, prev_ref, next_ref, w_ref, b_ref, o_ref, buf_ref, *, seg_starts, n_tiles):
    i = pl.program_id(1)
    pad = SSD_CONV_W // 2
    first = functools.reduce(jnp.logical_or, [i == s for s in seg_starts])
    last = functools.reduce(jnp.logical_or, [i == s - 1 for s in seg_starts[1:] + (n_tiles,)])
    t = x_ref.shape[2]
    buf_ref[8:8 + t, :] = x_ref[0, 0].astype(F32)
    buf_ref[0:8, :] = jnp.where(first, 0.0, prev_ref[0, 0].astype(F32)[8:16, :])
    buf_ref[8 + t:16 + t, :] = jnp.where(last, 0.0, next_ref[0, 0].astype(F32)[0:8, :])
    w = w_ref[...]
    acc = b_ref[...] + w[0:1, :] * buf_ref[8 - pad:8 - pad + t, :]
    for k in range(1, SSD_CONV_W):
        acc = acc + w[k:k + 1, :] * buf_ref[8 - pad + k:8 - pad + k + t, :]
    o_ref[0, 0] = _silu(acc).astype(o_ref.dtype)


def _conv_call(st, zx, conv_w, conv_b):
    cw = 1024
    n_cblk = SSD_CONV_DIM // cw
    off = SSD_D_INNER // cw
    n_tiles = st.s_pad // CONV_TILE
    hb = CONV_TILE // 16
    n_hblk = st.s_pad // 16
    seg_starts = (0, st.s_lat // CONV_TILE, st.s_real // CONV_TILE)
    w8 = jnp.zeros((8, SSD_CONV_DIM), F32).at[:SSD_CONV_W].set(conv_w)
    return pl.pallas_call(
        functools.partial(_conv_kernel, seg_starts=seg_starts, n_tiles=n_tiles),
        grid=(st.batch, n_tiles, n_cblk),
        in_specs=[pl.BlockSpec((1, 1, CONV_TILE, cw), lambda b, i, c: (off + c, b, i, 0)),
                  pl.BlockSpec((1, 1, 16, cw), lambda b, i, c: (off + c, b, jnp.maximum(i * hb - 1, 0), 0)),
                  pl.BlockSpec((1, 1, 16, cw),
                               lambda b, i, c: (off + c, b, jnp.minimum((i + 1) * hb, n_hblk - 1), 0)),
                  pl.BlockSpec((8, cw), lambda b, i, c: (0, c)),
                  pl.BlockSpec((1, cw), lambda b, i, c: (0, c))],
        out_specs=pl.BlockSpec((1, 1, CONV_TILE, cw), lambda b, i, c: (c, b, i, 0)),
        out_shape=jax.ShapeDtypeStruct((n_cblk, st.batch, st.s_pad, cw), BF16),
        scratch_shapes=[pltpu.VMEM((CONV_TILE + 16, cw), F32)],
        compiler_params=_cparams(("parallel", "parallel", "parallel")),
        name="ssd_conv",
    )(zx, zx, zx, w8, conv_b.reshape(1, SSD_CONV_DIM))


def _scan_kernel(xs_ref, bc_ref, dt_ref, tri_ref, par_ref, dlane_ref, y_ref, h_ref):
    @pl.when(pl.program_id(2) == 0)
    def _():
        h_ref[...] = jnp.zeros_like(h_ref)

    par = par_ref[0]
    raw = dt_ref[0, 0] + par[0:1, :]
    dt = jnp.maximum(raw, 0.0) + jnp.log1p(jnp.exp(-jnp.abs(raw)))
    a = -jnp.exp(par[1:2, :])
    dta = dt * a
    tri = tri_ref[0]
    cs = jnp.dot(tri, dta, preferred_element_type=F32, precision=lax.Precision.HIGHEST)
    tot = jnp.sum(dta, axis=0, keepdims=True)
    e_tot = jnp.exp(tot)
    w_end = jnp.exp(tot - cs) * dt
    cs_t = cs.T
    dt_t = dt.T
    w_end_t = w_end.T
    mask = tri > 0.5

    e_cs_t = jnp.exp(cs_t)
    n = SSD_STATE
    q = tri.shape[0]
    lane = lax.broadcasted_iota(jnp.int32, (q, LANE), 1)
    left = lane < SSD_HEADDIM
    eye = lax.broadcasted_iota(jnp.int32, (q, q), 0) == lax.broadcasted_iota(jnp.int32, (q, q), 1)
    lane1 = lax.broadcasted_iota(jnp.int32, (1, LANE), 1) < SSD_HEADDIM
    for g in range(SSD_GROUPS):
        bm = bc_ref[0, 0, :, g * n:(g + 1) * n]
        cm = bc_ref[0, 0, :, SSD_GN + g * n:SSD_GN + (g + 1) * n]
        bm_t = bm.astype(F32).T
        cb = lax.dot_general(cm, bm, NT_DIMS, preferred_element_type=F32)
        for r in range(0, SSD_HPG, 2):
            hds = (g * SSD_HPG + r, g * SSD_HPG + r + 1)
            pair = hds[0] // 2
            xb, xo = divmod(pair * LANE, xs_ref.shape[3])
            xp = xs_ref[xb, 0, :, xo:xo + LANE]
            zero = jnp.zeros_like(xp)
            x_blk = jnp.concatenate([jnp.where(left, xp, zero), jnp.where(left, zero, xp)], axis=0)
            h_prev = h_ref[pair]
            ch = jnp.dot(cm, h_prev.astype(BF16), preferred_element_type=F32)
            ch_blk = jnp.concatenate([jnp.where(left, ch, 0.0), jnp.where(left, 0.0, ch)], axis=0)
            w_parts, d_parts, u_parts = [], [], []
            for hd in hds:
                seg = cs[:, hd:hd + 1] - cs_t[hd:hd + 1, :]
                decay = jnp.where(mask, jnp.exp(seg), 0.0)
                w_parts.append(cb * decay * dt_t[hd:hd + 1, :])
                d_parts.append(jnp.where(eye, e_cs_t[hd:hd + 1, :], 0.0))
                u_parts.append(bm_t * w_end_t[hd:hd + 1, :])
            lhs = jnp.concatenate(w_parts + d_parts, axis=1).astype(BF16)
            rhs = jnp.concatenate([x_blk, ch_blk.astype(BF16)], axis=0)
            y = jnp.dot(lhs, rhs, preferred_element_type=F32)
            y = y + dlane_ref[0, :, pair * LANE:(pair + 1) * LANE] * xp.astype(F32)
            y_ref[0, 0, :, pair * LANE:(pair + 1) * LANE] = y.astype(y_ref.dtype)
            upd = jnp.dot(jnp.concatenate(u_parts, axis=1).astype(BF16), x_blk, preferred_element_type=F32)
            e_tot_pair = jnp.where(lane1, e_tot[:, hds[0]:hds[0] + 1], e_tot[:, hds[1]:hds[1] + 1])
            h_ref[pair] = e_tot_pair * h_prev + upd


def _scan_call(st, xc, dtr, par, dlane):
    q = SSD_CHUNK
    n_lat, n_ctx = st.s_lat // q, st.s_ctx // q
    n_real = n_lat + n_ctx
    n_all = st.s_pad // q
    cw = xc.shape[3]
    n_xblk = SSD_D_INNER // cw

    def chunk(d, s):
        fwd = jnp.where(s < n_ctx, n_lat + s, s - n_ctx)
        bwd = n_real - 1 - s
        return jnp.where(s < n_real, jnp.where(d == 0, fwd, bwd), s)

    idx = np.arange(q)
    tri = jnp.asarray(np.stack([idx[:, None] >= idx[None, :], idx[:, None] <= idx[None, :]]), F32)
    return pl.pallas_call(
        _scan_kernel,
        grid=(st.batch, 2, n_all),
        in_specs=[pl.BlockSpec((n_xblk, 1, q, cw), lambda b, d, s: (0, b, chunk(d, s), 0)),
                  pl.BlockSpec((1, 1, q, cw), lambda b, d, s: (n_xblk, b, chunk(d, s), 0)),
                  pl.BlockSpec((1, 1, q, LANE), lambda b, d, s: (d, b, chunk(d, s), 0)),
                  pl.BlockSpec((1, q, q), lambda b, d, s: (d, 0, 0)),
                  pl.BlockSpec((1, 8, LANE), lambda b, d, s: (d, 0, 0)),
                  pl.BlockSpec((1, 1, SSD_D_INNER), lambda b, d, s: (d, 0, 0))],
        out_specs=pl.BlockSpec((1, 1, q, SSD_D_INNER), lambda b, d, s: (d, b, chunk(d, s), 0)),
        out_shape=jax.ShapeDtypeStruct((2, st.batch, st.s_pad, SSD_D_INNER), BF16),
        scratch_shapes=[pltpu.VMEM((SSD_HEADS // 2, SSD_STATE, 2 * SSD_HEADDIM), F32)],
        compiler_params=_cparams(("parallel", "parallel", "arbitrary")),
        name="ssd_scan",
    )(xc, xc, dtr, tri, par, dlane)


def _attn_kernel(q_ref, k_ref, v_ref, lq_ref, lk_ref, g_ref, o_ref, qs_ref, vt_ref, m_ref, acc_ref,
                 s_ref, *, s_lat, s_ctx, lambda_init):
    qi = pl.program_id(2)
    tq = q_ref.shape[2]
    tk = ATT_TK
    tv = vt_ref.shape[2]
    n_sub = tk // tv
    hw = q_ref.shape[3]

    @pl.when(qi == 0)
    def _():
        def tr(c, carry):
            rows = v_ref[0, 0, pl.ds(pl.multiple_of(c * tv, tv), tv), :]
            vt_ref[c, 0:hw, :] = rows.astype(F32).T.astype(BF16)
            sub = lax.broadcasted_iota(jnp.int32, (vt_ref.shape[1] - hw, tv), 0)
            vt_ref[c, hw:, :] = jnp.where(sub == 0, 1.0, 0.0).astype(BF16)
            return carry

        lax.fori_loop(0, vt_ref.shape[0], tr, 0)

    qt = q_ref[0, 0].astype(F32).T
    row = lax.broadcasted_iota(jnp.int32, qt.shape, 0)
    qs_ref[:, 0:tq] = jnp.where(row < DA_HEAD_DIM, qt, 0.0).astype(BF16)
    qs_ref[:, tq:2 * tq] = jnp.where(row >= DA_HEAD_DIM, qt, 0.0).astype(BF16)
    m_ref[...] = jnp.full_like(m_ref, -jnp.inf)
    acc_ref[...] = jnp.zeros_like(acc_ref)

    def scores(k):
        return jnp.dot(k, qs_ref[...], preferred_element_type=F32)

    def update(read_s, vts):
        m_prev = m_ref[...]
        m_new = jnp.maximum(m_prev, jnp.max(read_s(), axis=0, keepdims=True))
        alpha = jnp.exp2(m_prev - m_new)
        pb = jnp.exp2(read_s() - m_new).astype(BF16)
        rows = pb.shape[0] // len(vts)
        pv = None
        for i, vt in enumerate(vts):
            d = jnp.dot(vt, pb[i * rows:(i + 1) * rows, :], preferred_element_type=F32)
            pv = d if pv is None else pv + d
        acc_ref[...] = alpha * acc_ref[...] + pv
        m_ref[...] = m_new

    n_lat = s_lat // tk

    def k_chunk(c):
        return k_ref[0, 0, pl.ds(pl.multiple_of(c * tk, tk), tk), :]

    @pl.when(qi < s_lat // tq)
    def _():
        s_ref[0] = scores(k_chunk(0))

        def half_step(c, slot):
            s_ref[1 - slot] = scores(k_chunk(jnp.minimum(c + 1, n_lat - 1)))
            update(lambda: s_ref[slot], [vt_ref[c * n_sub + i] for i in range(n_sub)])

        def body(cp, carry):
            half_step(2 * cp, 0)
            half_step(2 * cp + 1, 1)
            return carry

        lax.fori_loop(0, n_lat // 2, body, 0)
        if n_lat % 2:
            half_step(n_lat - 1, 0)

    s_ctx_keys = scores(k_ref[0, 0, s_lat:s_lat + s_ctx, :])
    update(lambda: s_ctx_keys, [vt_ref[s_lat // tv, :, 0:s_ctx]])

    lam = (jnp.exp(jnp.sum(lq_ref[0:1, :] * lk_ref[0:1, :], axis=-1, keepdims=True))
           - jnp.exp(jnp.sum(lq_ref[1:2, :] * lk_ref[1:2, :], axis=-1, keepdims=True)) + lambda_init)
    o_all = acc_ref[0:hw, :] / acc_ref[hw:hw + 1, :]
    o = (o_all[:, 0:tq] - lam * o_all[:, tq:2 * tq]).T
    o_ref[0, 0] = (_rms(o, g_ref[...]) * (1.0 - lambda_init)).astype(o_ref.dtype)


def _attn_call(st, qkv, lam_q, lam_k, subln_g, lambda_init):
    tq = ATT_TQ
    nq = st.s_pad // tq
    hw = 2 * DA_HEAD_DIM
    nh = DA_HEADS
    return pl.pallas_call(
        functools.partial(_attn_kernel, s_lat=st.s_lat, s_ctx=st.s_ctx, lambda_init=lambda_init),
        grid=(st.batch, nh, nq),
        in_specs=[pl.BlockSpec((1, 1, tq, hw), lambda b, h, i: (b, h, i, 0)),
                  pl.BlockSpec((1, 1, st.s_pad, hw), lambda b, h, i: (b, nh + h, 0, 0)),
                  pl.BlockSpec((1, 1, st.s_pad, hw), lambda b, h, i: (b, 2 * nh + h, 0, 0)),
                  pl.BlockSpec((2, DA_HEAD_DIM), lambda b, h, i: (0, 0)),
                  pl.BlockSpec((2, DA_HEAD_DIM), lambda b, h, i: (0, 0)),
                  pl.BlockSpec((1, hw), lambda b, h, i: (0, 0))],
        out_specs=pl.BlockSpec((1, 1, tq, hw), lambda b, h, i: (b, h, i, 0)),
        out_shape=jax.ShapeDtypeStruct((st.batch, nh, st.s_pad, hw), BF16),
        scratch_shapes=[pltpu.VMEM((hw, 2 * tq), BF16),
                        pltpu.VMEM((st.s_pad // ATT_TV, hw + ATT_SUM_ROWS, ATT_TV), BF16),
                        pltpu.VMEM((1, 2 * tq), F32),
                        pltpu.VMEM((hw + ATT_SUM_ROWS, 2 * tq), F32),
                        pltpu.VMEM((2, ATT_TK, 2 * tq), F32)],
        compiler_params=_cparams(("parallel", "parallel", "arbitrary")),
        name="diff_attn",
    )(qkv, qkv, qkv, lam_q, lam_k, subln_g.reshape(1, hw))


def _extract_top(s, dst_ref, n):
    w = s
    rank = jnp.full_like(s, float(n))
    for k in range(n):
        m = jnp.max(w, axis=0, keepdims=True)
        dst_ref[k:k + 1, :] = m
        hit = w == m
        rank = jnp.where(hit, float(k), rank)
        if k + 1 < n:
            w = jnp.where(hit, -jnp.inf, w)
    return rank


def _gate_kernel(s_ref, e1_ref, rank_ref, e2_ref, level_ref, a_ref, b_ref):
    k = PEER_TOPK
    n_lb = s_ref.shape[4]
    s1 = jnp.concatenate([s_ref[0, 0, 0, 0, lb] for lb in range(n_lb)], axis=1)
    s2 = jnp.concatenate([s_ref[0, 0, 0, 1, lb] for lb in range(n_lb)], axis=1)
    rank = _extract_top(s1, a_ref, k)
    _extract_top(s2, b_ref, k)
    a = a_ref[...]
    b = b_ref[...]
    cand = jnp.concatenate(
        [a[0:1, :] + b] + [a[i:i + 1, :] + b[0:8, :] for i in range(1, 8)] + [a[8:16, :] + b[0:1, :]],
        axis=0)
    w = cand
    tau = None
    for it in range(k):
        tau = jnp.max(w, axis=0, keepdims=True)
        if it + 1 < k:
            w = jnp.where(w == tau, -jnp.inf, w)
    top = a[0:1, :] + b[0:1, :]
    z = jnp.sum(jnp.where(cand >= tau, jnp.exp(cand - top), 0.0), axis=0, keepdims=True)
    level = jnp.zeros_like(s2)
    for j in range(k):
        level = jnp.where(s2 >= tau - a[j:j + 1, :], float(j + 1), level)
    blk_shape = e1_ref.shape[2:3] + e1_ref.shape[4:]
    e1_ref[0, 0, :, 0] = jnp.exp(s1 - a[0:1, :]).reshape(blk_shape)
    rank_ref[0, 0, :, 0] = rank.reshape(blk_shape)
    e2 = (jnp.exp(s2 - b[0:1, :]) / z).astype(BF16)
    level = level.astype(BF16)
    for lb in range(n_lb):
        e2_ref[0, 0, 0, lb] = e2[:, lb * LANE:(lb + 1) * LANE]
        level_ref[0, 0, 0, lb] = level[:, lb * LANE:(lb + 1) * LANE]


def _gate_call(st, scores):
    nk = PEER_NKEYS
    tt = TOK_TILE
    rows = PEER_EBLK // nk
    n_blk = nk // rows
    shp_blk = jax.ShapeDtypeStruct((st.batch, st.n_tok, n_blk, PEER_HEADS, rows, tt), F32)
    spec_blk = pl.BlockSpec((1, 1, n_blk, 1, rows, tt), lambda b, i, h: (b, i, 0, h, 0, 0))
    n_lb = tt // LANE
    shp_e2 = jax.ShapeDtypeStruct((st.batch, st.n_tok, PEER_HEADS, n_lb, nk, LANE), BF16)
    spec_e2 = pl.BlockSpec((1, 1, 1, n_lb, nk, LANE), lambda b, i, h: (b, i, h, 0, 0, 0))
    return pl.pallas_call(
        _gate_kernel,
        grid=(st.batch, st.n_tok, PEER_HEADS),
        in_specs=[pl.BlockSpec((1, 1, 1, 2, n_lb, nk, LANE), lambda b, i, h: (b, i, h, 0, 0, 0, 0))],
        out_specs=[spec_blk, spec_blk, spec_e2, spec_e2],
        out_shape=[shp_blk, shp_blk, shp_e2, shp_e2],
        scratch_shapes=[pltpu.VMEM((PEER_TOPK, TOK_TILE), F32), pltpu.VMEM((PEER_TOPK, TOK_TILE), F32)],
        compiler_params=_cparams(("parallel", "parallel", "parallel")),
        name="peer_gates",
    )(scores)


def _expert_kernel(hq_ref, u0_ref, un_ref, vt_ref, e1_ref, rank_ref, level_ref, e2_ref, xr_ref, gate_ref, o_ref,
                   acc_ref, act_ref, p_ref, hqt_ref, *, gi):
    e = pl.program_id(2)
    nk = PEER_NKEYS
    tt = hq_ref.shape[1]
    half = tt // 2
    sqrt_half = np.float32(math.sqrt(0.5))

    def put_act(slot, r0, r1, c0, c1, val):
        for lb in range(c0 // LANE, c1 // LANE):
            act_ref[slot, lb, r0:r1, :] = val[:, lb * LANE - c0:(lb + 1) * LANE - c0]

    @pl.when(e == 0)
    def _():
        acc_ref[...] = jnp.zeros_like(acc_ref)
        hqt_ref[...] = hq_ref[0].astype(F32).T.astype(BF16)
        put_act(0, 0, PEER_EBLK, 0, tt, jnp.dot(u0_ref[...], hqt_ref[...], preferred_element_type=F32))

    def gate_tile(cur, a_list, t0):
        cols = slice(t0, t0 + LANE)
        lb = t0 // LANE
        e1 = [[e1_ref[0, 0, 0, h, a:a + 1, cols].astype(BF16) for h in range(PEER_HEADS)] for a in a_list]
        rank = [[rank_ref[0, 0, 0, h, a:a + 1, cols].astype(BF16) for h in range(PEER_HEADS)] for a in a_list]
        zero = jnp.zeros((PEER_RB, LANE), BF16)
        for r0 in range(0, nk, PEER_RB):
            rows = slice(r0, r0 + PEER_RB)
            g = [None] * len(a_list)
            for h in range(PEER_HEADS):
                level = level_ref[0, 0, h, lb, rows, :]
                e2 = e2_ref[0, 0, h, lb, rows, :]
                for ai in range(len(a_list)):
                    term = e1[ai][h] * jnp.where(rank[ai][h] < level, e2, zero)
                    g[ai] = term if g[ai] is None else g[ai] + term
            for ai, a in enumerate(a_list):
                x = act_ref[cur, lb, a * nk + r0:a * nk + r0 + PEER_RB, :]
                gelu = 0.5 * x * (1.0 + lax.erf(x * sqrt_half))
                p_ref[lb, a * nk + r0:a * nk + r0 + PEER_RB, :] = g[ai] * gelu.astype(BF16)

    def stages(cur, nxt):
        rows_per = 2 * nk
        act_rows = 2 * rows_per
        for c0 in (0, half):
            for r0 in range(0, PEER_EBLK, rows_per):
                if r0 % act_rows == 0:
                    put_act(nxt, r0, r0 + act_rows, c0, c0 + half, jnp.dot(
                        un_ref[r0:r0 + act_rows, :], hqt_ref[:, c0:c0 + half],
                        preferred_element_type=F32))
                a_list = list(range(r0 // nk, (r0 + rows_per) // nk))
                for t0 in range(c0, c0 + half, LANE):
                    gate_tile(cur, a_list, t0)
                p_blk = jnp.concatenate([p_ref[lb, r0:r0 + rows_per, :]
                                         for lb in range(c0 // LANE, (c0 + half) // LANE)], axis=1)
                out = jnp.dot(vt_ref[0, :, r0:r0 + rows_per], p_blk, preferred_element_type=F32)
                for j, lb in enumerate(range(c0 // LANE, (c0 + half) // LANE)):
                    acc_ref[lb] += out[:, j * LANE:(j + 1) * LANE]

    for parity in range(2):
        pl.when(e % 2 == parity)(functools.partial(stages, parity, 1 - parity))

    @pl.when(e == pl.num_programs(2) - 1)
    def _():
        acc = jnp.concatenate([acc_ref[lb] for lb in range(acc_ref.shape[0])], axis=1)
        o_ref[0] = xr_ref[0] + gate_ref[0][gi:gi + 1, :] * acc.T


def _expert_call(st, hq, u, vt, e1, rank, level, e2, x, modtab, gi):
    n_exp, d = u.shape
    nk = PEER_NKEYS
    rows = PEER_EBLK // nk
    tt = TOK_TILE
    n_blk = n_exp // PEER_EBLK
    return pl.pallas_call(
        functools.partial(_expert_kernel, gi=gi),
        grid=(st.batch, st.n_tok, n_blk),
        in_specs=[pl.BlockSpec((1, tt, d), lambda b, i, e: (b, i, 0)),
                  pl.BlockSpec((PEER_EBLK, d), lambda b, i, e: (0, 0)),
                  pl.BlockSpec((PEER_EBLK, d), lambda b, i, e: (jnp.minimum(e + 1, n_blk - 1), 0)),
                  pl.BlockSpec((1, d, PEER_EBLK), lambda b, i, e: (e, 0, 0)),
                  pl.BlockSpec((1, 1, 1, PEER_HEADS, rows, tt), lambda b, i, e: (b, i, e, 0, 0, 0)),
                  pl.BlockSpec((1, 1, 1, PEER_HEADS, rows, tt), lambda b, i, e: (b, i, e, 0, 0, 0)),
                  pl.BlockSpec((1, 1, PEER_HEADS, tt // LANE, nk, LANE), lambda b, i, e: (b, i, 0, 0, 0, 0)),
                  pl.BlockSpec((1, 1, PEER_HEADS, tt // LANE, nk, LANE), lambda b, i, e: (b, i, 0, 0, 0, 0)),
                  pl.BlockSpec((1, tt, d), lambda b, i, e: (b, i, 0)),
                  pl.BlockSpec((1, 6, d), lambda b, i, e: (st.mod_row(b, i), 0, 0))],
        out_specs=pl.BlockSpec((1, tt, d), lambda b, i, e: (b, i, 0)),
        out_shape=jax.ShapeDtypeStruct((st.batch, st.s_pad, d), F32),
        scratch_shapes=[pltpu.VMEM((tt // LANE, d, LANE), F32),
                        pltpu.VMEM((2, tt // LANE, PEER_EBLK, LANE), F32),
                        pltpu.VMEM((tt // LANE, PEER_EBLK, LANE), BF16),
                        pltpu.VMEM((d, tt), BF16)],
        compiler_params=_cparams(("parallel", "parallel", "arbitrary")),
        name="peer_experts",
    )(hq, u, u, vt, e1, rank, level, e2, x, modtab)


def _final_norm_kernel(x_ref, g_ref, o_ref):
    o_ref[0] = _rms(x_ref[0], g_ref[...])


def _final_norm(st, x, g):
    d = st.d
    return pl.pallas_call(
        _final_norm_kernel,
        grid=(st.batch, st.n_lat_tok),
        in_specs=[pl.BlockSpec((1, TOK_TILE, d), lambda b, i: (b, i, 0)),
                  pl.BlockSpec((1, d), lambda b, i: (0, 0))],
        out_specs=pl.BlockSpec((1, TOK_TILE, d), lambda b, i: (b, i, 0)),
        out_shape=jax.ShapeDtypeStruct((st.batch, st.s_lat, d), F32),
        compiler_params=_cparams(("parallel", "parallel")),
        name="final_norm",
    )(x, g.reshape(1, d))


def _rope_tables(st):
    rows = st.s_lat // GRID_W
    row = jnp.repeat(jnp.arange(rows, dtype=F32), GRID_W)
    col = jnp.tile(jnp.arange(GRID_W, dtype=F32), rows)
    half = DA_HEAD_DIM // 2
    freqs = ROPE_BASE ** (-jnp.arange(0, half, 2, dtype=F32) / half)
    ang_r = row[:, None] * freqs
    ang_c = col[:, None] * freqs
    ang = jnp.concatenate([ang_r, ang_r, ang_c, ang_c] * 2, axis=-1)
    extra = st.s_pad - st.s_lat
    cos = jnp.concatenate([jnp.cos(ang), jnp.ones((extra, LANE), F32)], axis=0)
    sin = jnp.concatenate([jnp.sin(ang), jnp.zeros((extra, LANE), F32)], axis=0)
    return cos, sin


def _qkv_weight(w):
    d = w.shape[0]
    qk = DA_HEADS * DA_HEAD_DIM * 2

    def regroup(t):
        return t.reshape(d, 2, DA_HEADS, DA_HEAD_DIM).transpose(0, 2, 1, 3).reshape(d, qk)

    return jnp.concatenate([regroup(w[:, :qk]), regroup(w[:, qk:2 * qk]), w[:, 2 * qk:]], axis=1).astype(BF16)


def _blocked_vt(v):
    n_exp, d = v.shape
    return v.reshape(n_exp // PEER_EBLK, PEER_EBLK, d).transpose(0, 2, 1).astype(BF16)


def _dt_weight(w_dt):
    d = w_dt.shape[0]
    t = w_dt.reshape(d, 2, SSD_HEADS)
    return jnp.pad(t, ((0, 0), (0, 0), (0, LANE - SSD_HEADS))).reshape(d, 2 * LANE).astype(BF16)


def _scan_params(dt_bias, a_log, d_skip):
    def lanes(t):
        return jnp.pad(t.reshape(2, SSD_HEADS), ((0, 0), (0, LANE - SSD_HEADS)))

    rows = jnp.stack([lanes(dt_bias), lanes(a_log)], axis=1)
    dlane = jnp.repeat(d_skip.reshape(2, SSD_HEADS), SSD_HEADDIM, axis=1)[:, None, :]
    return jnp.pad(rows, ((0, 0), (0, 6), (0, 0))).astype(F32), dlane.astype(F32)


def kernel(x, c, ctx, c_ctx, ada_w, ada_b, norm_mix_g, norm_ffn_g, ssd_in_w, ssd_conv_w, ssd_conv_b,
           ssd_dt_bias, ssd_a_log, ssd_d, ssd_norm_g, ssd_out_w, attn_qkv_w, attn_lambda_q, attn_lambda_k,
           attn_subln_g, attn_out_w, peer_q_w, peer_keys, peer_u, peer_v, final_norm_g):
    batch, s_lat, d = x.shape
    s_ctx = ctx.shape[1]
    depth = ada_w.shape[0]
    st = _Stream(batch, s_lat, s_ctx, d)

    xs = jnp.concatenate([x, ctx, jnp.zeros((batch, st.s_pad - st.s_real, d), F32)], axis=1)

    cc = jnp.zeros((8, d), F32).at[:batch].set(c).at[batch].set(c_ctx)
    mods = _ada_call(cc, ada_w, ada_b).reshape(depth, 8, 6, d)
    cos, sin = _rope_tables(st)
    zxw = SSD_D_INNER + SSD_CONV_DIM

    for i in range(depth):
        lat = mods[i, :batch]
        con = jnp.broadcast_to(mods[i, batch], (batch, 6, d))
        modtab = jnp.stack([con, lat], axis=1).reshape(batch * 2, 6, d)
        jm = i // 2
        if i % 2 == 0:
            w_in = ssd_in_w[jm]
            zx = _proj_mod(st, xs, norm_mix_g[i], modtab, w_in[:, :zxw].astype(BF16), 0, 1, 1024, BF16,
                           "ssd_in_proj")
            dtr = _proj_mod(st, xs, norm_mix_g[i], modtab, _dt_weight(w_in[:, zxw:]), 0, 1, LANE, F32,
                            "ssd_dt_proj")
            xc = _conv_call(st, zx, ssd_conv_w[jm], ssd_conv_b[jm])
            y2 = _scan_call(st, xc, dtr, *_scan_params(ssd_dt_bias[jm], ssd_a_log[jm], ssd_d[jm]))
            xs = _proj_gate_resid(st, y2, zx, ssd_norm_g[jm], ssd_out_w[jm].astype(BF16), xs, modtab, 2)
        else:
            lambda_init = 0.8 - 0.6 * math.exp(-0.3 * i)
            qkv = _proj_qkv(st, xs, norm_mix_g[i], modtab, _qkv_weight(attn_qkv_w[jm]), cos, sin, 0, 1)
            o = _attn_call(st, qkv, attn_lambda_q[jm], attn_lambda_k[jm], attn_subln_g[jm], lambda_init)
            xs = _proj_heads_resid(st, o, attn_out_w[jm].astype(BF16), xs, modtab, 2)
        scores, hq = _proj_scores(st, xs, norm_ffn_g[i], modtab, peer_q_w[i].astype(BF16), peer_keys[i], 3, 4)
        e1, rank, e2, level = _gate_call(st, scores)
        xs = _expert_call(st, hq, peer_u[i].astype(BF16), _blocked_vt(peer_v[i]), e1, rank, level, e2,
                          xs, modtab, 5)
    return _final_norm(st, xs, final_norm_g)
```

```python
import functools
import math

import numpy as np
import jax
import jax.numpy as jnp
from jax import lax
from jax.experimental import pallas as pl
from jax.experimental.pallas import tpu as pltpu

F32 = jnp.float32
BF16 = jnp.bfloat16

EPS = 1e-6
GRID_W = 64
ROPE_BASE = 10000.0

SSD_HEADDIM = 64
SSD_GROUPS = 4
SSD_HPG = 8
SSD_HEADS = SSD_GROUPS * SSD_HPG
SSD_STATE = 128
SSD_CONV_W = 5
SSD_CHUNK = 128
SSD_D_INNER = SSD_HEADS * SSD_HEADDIM
SSD_GN = SSD_GROUPS * SSD_STATE
SSD_CONV_DIM = SSD_D_INNER + 2 * SSD_GN

DA_HEADS = 8
DA_HEAD_DIM = 64

PEER_HEADS = 8
PEER_NKEYS = 128
PEER_TOPK = 16
PEER_EBLK = 1024
PEER_RB = 64

LANE = 128
TOK_TILE = 512
CONV_TILE = 256
ATT_TQ = 512
ATT_TK = 1024
ATT_TV = 512
ATT_SUM_ROWS = 16
VMEM_LIMIT = 56 * 1024 * 1024

NT_DIMS = (((1,), (1,)), ((), ()))


def _cparams(sem):
    return pltpu.CompilerParams(dimension_semantics=sem, vmem_limit_bytes=VMEM_LIMIT)


def _rms(xf, g):
    return xf * lax.rsqrt(jnp.mean(xf * xf, axis=-1, keepdims=True) + EPS) * g


def _silu(x):
    return x * (1.0 / (1.0 + jnp.exp(-x)))


def _ada_kernel(c_ref, w_ref, b_ref, o_ref):
    c = c_ref[...]
    o_ref[0] = jnp.dot(_silu(c), w_ref[0], preferred_element_type=F32,
                       precision=lax.Precision.HIGHEST) + b_ref[0]


def _ada_call(cc, ada_w, ada_b):
    depth, d, n = ada_w.shape
    tn = 1536
    return pl.pallas_call(
        _ada_kernel,
        grid=(depth, n // tn),
        in_specs=[pl.BlockSpec((8, d), lambda l, j: (0, 0)),
                  pl.BlockSpec((1, d, tn), lambda l, j: (l, 0, j)),
                  pl.BlockSpec((1, 1, tn), lambda l, j: (l, 0, j))],
        out_specs=pl.BlockSpec((1, 8, tn), lambda l, j: (l, 0, j)),
        out_shape=jax.ShapeDtypeStruct((depth, 8, n), F32),
        compiler_params=_cparams(("parallel", "parallel")),
        name="ada_mod",
    )(cc, ada_w, ada_b.reshape(depth, 1, n))


def _mod_prologue(x_ref, g_ref, mod_ref, hn_ref, sh, sc):
    m = mod_ref[0]
    hn = _rms(x_ref[0], g_ref[...]) * (1.0 + m[sc:sc + 1, :]) + m[sh:sh + 1, :]
    hn_ref[...] = hn.astype(BF16)


def _proj_mod_kernel(x_ref, g_ref, mod_ref, w_ref, o_ref, hn_ref, *, sh, sc):
    @pl.when(pl.program_id(2) == 0)
    def _():
        _mod_prologue(x_ref, g_ref, mod_ref, hn_ref, sh, sc)

    o_ref[0, 0] = jnp.dot(hn_ref[...], w_ref[...], preferred_element_type=F32).astype(o_ref.dtype)


def _rope_block(a, cos, sin, lane):
    fwd = pltpu.roll(a, LANE - 16, axis=1)
    bwd = pltpu.roll(a, 16, axis=1)
    rot = jnp.where(lane % 32 < 16, -fwd, bwd)
    return a * cos + rot * sin


def _proj_qkv_kernel(x_ref, g_ref, mod_ref, w_ref, cos_ref, sin_ref, o_ref, hn_ref, *, sh, sc):
    j = pl.program_id(2)

    @pl.when(j == 0)
    def _():
        _mod_prologue(x_ref, g_ref, mod_ref, hn_ref, sh, sc)

    acc = jnp.dot(hn_ref[...], w_ref[...], preferred_element_type=F32)
    tn = acc.shape[1]

    @pl.when(j < 2)
    def _():
        cos = cos_ref[...]
        sin = sin_ref[...]
        lane = lax.broadcasted_iota(jnp.int32, cos.shape, 1)
        scale = jnp.where(j == 0, DA_HEAD_DIM ** -0.5 * math.log2(math.e), 1.0).astype(F32)
        for cb in range(tn // LANE):
            a = acc[:, cb * LANE:(cb + 1) * LANE]
            o_ref[0, cb] = (_rope_block(a, cos, sin, lane) * scale).astype(o_ref.dtype)

    @pl.when(j >= 2)
    def _():
        for cb in range(tn // LANE):
            o_ref[0, cb] = acc[:, cb * LANE:(cb + 1) * LANE].astype(o_ref.dtype)


def _proj_gate_resid_kernel(yf_ref, yb_ref, z_ref, g_ref, w_ref, xr_ref, gate_ref, o_ref, hn_ref,
                            *, gi):
    @pl.when(pl.program_id(2) == 0)
    def _():
        z = jnp.concatenate([z_ref[c, 0] for c in range(z_ref.shape[0])], axis=1).astype(F32)
        u = (yf_ref[0, 0].astype(F32) + yb_ref[0, 0].astype(F32)) * _silu(z)
        hn_ref[...] = _rms(u, g_ref[...]).astype(BF16)

    acc = jnp.dot(hn_ref[...], w_ref[...], preferred_element_type=F32)
    o_ref[0] = xr_ref[0] + gate_ref[0][gi:gi + 1, :] * acc


def _proj_heads_resid_kernel(h_ref, w_ref, xr_ref, gate_ref, o_ref, *, gi):
    h = jnp.concatenate([h_ref[0, hh] for hh in range(h_ref.shape[1])], axis=1)
    acc = jnp.dot(h, w_ref[...], preferred_element_type=F32)
    o_ref[0] = xr_ref[0] + gate_ref[0][gi:gi + 1, :] * acc


def _proj_scores_kernel(x_ref, g_ref, mod_ref, w_ref, keys_ref, s_ref, hq_ref, hn_ref, *, sh, sc):
    @pl.when(pl.program_id(2) == 0)
    def _():
        _mod_prologue(x_ref, g_ref, mod_ref, hn_ref, sh, sc)
        hq_ref[0] = hn_ref[...]

    q = jnp.dot(hn_ref[...], w_ref[...], preferred_element_type=F32)
    half = PEER_NKEYS
    for c in range(2):
        s_hc = lax.dot_general(
            keys_ref[0, c], q[:, c * half:(c + 1) * half], NT_DIMS,
            preferred_element_type=F32, precision=lax.Precision.HIGHEST)
        for lb in range(s_hc.shape[1] // LANE):
            s_ref[0, 0, 0, c, lb] = s_hc[:, lb * LANE:(lb + 1) * LANE]


class _Stream:
    def __init__(self, batch, s_lat, s_ctx, d):
        assert s_lat % TOK_TILE == 0 and s_ctx % CONV_TILE == 0 and s_lat % GRID_W == 0
        self.batch, self.s_lat, self.s_ctx, self.d = batch, s_lat, s_ctx, d
        self.s_real = s_lat + s_ctx
        self.s_pad = -(-self.s_real // TOK_TILE) * TOK_TILE
        self.n_tok = self.s_pad // TOK_TILE
        self.n_lat_tok = s_lat // TOK_TILE

    def mod_row(self, b, i):
        return b * 2 + jnp.where(i < self.n_lat_tok, 1, 0)


def _mod_specs(st, d):
    return [pl.BlockSpec((1, TOK_TILE, d), lambda b, i, j: (b, i, 0)),
            pl.BlockSpec((1, d), lambda b, i, j: (0, 0)),
            pl.BlockSpec((1, 6, d), lambda b, i, j: (st.mod_row(b, i), 0, 0))]


def _proj_mod(st, x, g, modtab, w, sh, sc, tn, out_dtype, name):
    d, n = w.shape
    return pl.pallas_call(
        functools.partial(_proj_mod_kernel, sh=sh, sc=sc),
        grid=(st.batch, st.n_tok, n // tn),
        in_specs=_mod_specs(st, d) + [pl.BlockSpec((d, tn), lambda b, i, j: (0, j))],
        out_specs=pl.BlockSpec((1, 1, TOK_TILE, tn), lambda b, i, j: (j, b, i, 0)),
        out_shape=jax.ShapeDtypeStruct((n // tn, st.batch, st.s_pad, tn), out_dtype),
        scratch_shapes=[pltpu.VMEM((TOK_TILE, d), BF16)],
        compiler_params=_cparams(("parallel", "parallel", "arbitrary")),
        name=name,
    )(x, g.reshape(1, d), modtab, w)


def _proj_qkv(st, x, g, modtab, w, cos, sin, sh, sc):
    d, n = w.shape
    tn = 1024
    return pl.pallas_call(
        functools.partial(_proj_qkv_kernel, sh=sh, sc=sc),
        grid=(st.batch, st.n_tok, n // tn),
        in_specs=_mod_specs(st, d) + [
            pl.BlockSpec((d, tn), lambda b, i, j: (0, j)),
            pl.BlockSpec((TOK_TILE, LANE), lambda b, i, j: (i, 0)),
            pl.BlockSpec((TOK_TILE, LANE), lambda b, i, j: (i, 0))],
        out_specs=pl.BlockSpec((1, tn // LANE, TOK_TILE, LANE), lambda b, i, j: (b, j, i, 0)),
        out_shape=jax.ShapeDtypeStruct((st.batch, n // LANE, st.s_pad, LANE), BF16),
        scratch_shapes=[pltpu.VMEM((TOK_TILE, d), BF16)],
        compiler_params=_cparams(("parallel", "parallel", "arbitrary")),
        name="attn_qkv_rope",
    )(x, g.reshape(1, d), modtab, w, cos, sin)


def _proj_gate_resid(st, y2, zx, g, w, x, modtab, gi):
    k, n = w.shape
    tn = n
    return pl.pallas_call(
        functools.partial(_proj_gate_resid_kernel, gi=gi),
        grid=(st.batch, st.n_tok, n // tn),
        in_specs=[pl.BlockSpec((1, 1, TOK_TILE, k), lambda b, i, j: (0, b, i, 0)),
                  pl.BlockSpec((1, 1, TOK_TILE, k), lambda b, i, j: (1, b, i, 0)),
                  pl.BlockSpec((k // zx.shape[3], 1, TOK_TILE, zx.shape[3]), lambda b, i, j: (0, b, i, 0)),
                  pl.BlockSpec((1, k), lambda b, i, j: (0, 0)),
                  pl.BlockSpec((k, tn), lambda b, i, j: (0, j)),
                  pl.BlockSpec((1, TOK_TILE, tn), lambda b, i, j: (b, i, j)),
                  pl.BlockSpec((1, 6, tn), lambda b, i, j: (st.mod_row(b, i), 0, j))],
        out_specs=pl.BlockSpec((1, TOK_TILE, tn), lambda b, i, j: (b, i, j)),
        out_shape=jax.ShapeDtypeStruct((st.batch, st.s_pad, n), F32),
        scratch_shapes=[pltpu.VMEM((TOK_TILE, k), BF16)],
        compiler_params=_cparams(("parallel", "parallel", "arbitrary")),
        name="ssd_out_proj",
    )(y2, y2, zx, g.reshape(1, k), w, x, modtab)


def _proj_heads_resid(st, h, w, x, modtab, gi):
    k, n = w.shape
    tn = n
    return pl.pallas_call(
        functools.partial(_proj_heads_resid_kernel, gi=gi),
        grid=(st.batch, st.n_tok, n // tn),
        in_specs=[pl.BlockSpec((1, k // LANE, TOK_TILE, LANE), lambda b, i, j: (b, 0, i, 0)),
                  pl.BlockSpec((k, tn), lambda b, i, j: (0, j)),
                  pl.BlockSpec((1, TOK_TILE, tn), lambda b, i, j: (b, i, j)),
                  pl.BlockSpec((1, 6, tn), lambda b, i, j: (st.mod_row(b, i), 0, j))],
        out_specs=pl.BlockSpec((1, TOK_TILE, tn), lambda b, i, j: (b, i, j)),
        out_shape=jax.ShapeDtypeStruct((st.batch, st.s_pad, n), F32),
        compiler_params=_cparams(("parallel", "parallel", "arbitrary")),
        name="attn_out_proj",
    )(h, w, x, modtab)


def _proj_scores(st, x, g, modtab, w, keys, sh, sc):
    d, n = w.shape
    tn = 2 * PEER_NKEYS
    return pl.pallas_call(
        functools.partial(_proj_scores_kernel, sh=sh, sc=sc),
        grid=(st.batch, st.n_tok, PEER_HEADS),
        in_specs=_mod_specs(st, d) + [
            pl.BlockSpec((d, tn), lambda b, i, j: (0, j)),
            pl.BlockSpec((1, 2, PEER_NKEYS, PEER_NKEYS), lambda b, i, j: (j, 0, 0, 0))],
        out_specs=[pl.BlockSpec((1, 1, 1, 2, TOK_TILE // LANE, PEER_NKEYS, LANE),
                                lambda b, i, j: (b, i, j, 0, 0, 0, 0)),
                   pl.BlockSpec((1, TOK_TILE, d), lambda b, i, j: (b, i, 0))],
        out_shape=[jax.ShapeDtypeStruct(
            (st.batch, st.n_tok, PEER_HEADS, 2, TOK_TILE // LANE, PEER_NKEYS, LANE), F32),
                   jax.ShapeDtypeStruct((st.batch, st.s_pad, d), BF16)],
        scratch_shapes=[pltpu.VMEM((TOK_TILE, d), BF16)],
        compiler_params=_cparams(("parallel", "parallel", "arbitrary")),
        name="peer_scores",
    )(x, g.reshape(1, d), modtab, w, keys)


def _conv_kernel(x_ref, prev_ref, next_ref, w_ref, b_ref, o_ref, buf_ref, *, seg_starts, n_tiles):
    i = pl.program_id(1)
    pad = SSD_CONV_W // 2
    first = functools.reduce(jnp.logical_or, [i == s for s in seg_starts])
    last = functools.reduce(jnp.logical_or, [i == s - 1 for s in seg_starts[1:] + (n_tiles,)])
    t = x_ref.shape[2]
    buf_ref[8:8 + t, :] = x_ref[0, 0].astype(F32)
    buf_ref[0:8, :] = jnp.where(first, 0.0, prev_ref[0, 0].astype(F32)[8:16, :])
    buf_ref[8 + t:16 + t, :] = jnp.where(last, 0.0, next_ref[0, 0].astype(F32)[0:8, :])
    w = w_ref[...]
    acc = b_ref[...] + w[0:1, :] * buf_ref[8 - pad:8 - pad + t, :]
    for k in range(1, SSD_CONV_W):
        acc = acc + w[k:k + 1, :] * buf_ref[8 - pad + k:8 - pad + k + t, :]
    o_ref[0, 0] = _silu(acc).astype(o_ref.dtype)


def _conv_call(st, zx, conv_w, conv_b):
    cw = 1024
    n_cblk = SSD_CONV_DIM // cw
    off = SSD_D_INNER // cw
    n_tiles = st.s_pad // CONV_TILE
    hb = CONV_TILE // 16
    n_hblk = st.s_pad // 16
    seg_starts = (0, st.s_lat // CONV_TILE, st.s_real // CONV_TILE)
    w8 = jnp.zeros((8, SSD_CONV_DIM), F32).at[:SSD_CONV_W].set(conv_w)
    return pl.pallas_call(
        functools.partial(_conv_kernel, seg_starts=seg_starts, n_tiles=n_tiles),
        grid=(st.batch, n_tiles, n_cblk),
        in_specs=[pl.BlockSpec((1, 1, CONV_TILE, cw), lambda b, i, c: (off + c, b, i, 0)),
                  pl.BlockSpec((1, 1, 16, cw), lambda b, i, c: (off + c, b, jnp.maximum(i * hb - 1, 0), 0)),
                  pl.BlockSpec((1, 1, 16, cw),
                               lambda b, i, c: (off + c, b, jnp.minimum((i + 1) * hb, n_hblk - 1), 0)),
                  pl.BlockSpec((8, cw), lambda b, i, c: (0, c)),
                  pl.BlockSpec((1, cw), lambda b, i, c: (0, c))],
        out_specs=pl.BlockSpec((1, 1, CONV_TILE, cw), lambda b, i, c: (c, b, i, 0)),
        out_shape=jax.ShapeDtypeStruct((n_cblk, st.batch, st.s_pad, cw), BF16),
        scratch_shapes=[pltpu.VMEM((CONV_TILE + 16, cw), F32)],
        compiler_params=_cparams(("parallel", "parallel", "parallel")),
        name="ssd_conv",
    )(zx, zx, zx, w8, conv_b.reshape(1, SSD_CONV_DIM))


def _scan_kernel(xs_ref, bc_ref, dt_ref, tri_ref, par_ref, dlane_ref, y_ref, h_ref):
    @pl.when(pl.program_id(2) == 0)
    def _():
        h_ref[...] = jnp.zeros_like(h_ref)

    par = par_ref[0]
    raw = dt_ref[0, 0] + par[0:1, :]
    dt = jnp.maximum(raw, 0.0) + jnp.log1p(jnp.exp(-jnp.abs(raw)))
    a = -jnp.exp(par[1:2, :])
    dta = dt * a
    tri = tri_ref[0]
    cs = jnp.dot(tri, dta, preferred_element_type=F32, precision=lax.Precision.HIGHEST)
    tot = jnp.sum(dta, axis=0, keepdims=True)
    e_tot = jnp.exp(tot)
    w_end = jnp.exp(tot - cs) * dt
    cs_t = cs.T
    dt_t = dt.T
    w_end_t = w_end.T
    mask = tri > 0.5

    e_cs_t = jnp.exp(cs_t)
    n = SSD_STATE
    q = tri.shape[0]
    lane = lax.broadcasted_iota(jnp.int32, (q, LANE), 1)
    left = lane < SSD_HEADDIM
    eye = lax.broadcasted_iota(jnp.int32, (q, q), 0) == lax.broadcasted_iota(jnp.int32, (q, q), 1)
    lane1 = lax.broadcasted_iota(jnp.int32, (1, LANE), 1) < SSD_HEADDIM
    for g in range(SSD_GROUPS):
        bm = bc_ref[0, 0, :, g * n:(g + 1) * n]
        cm = bc_ref[0, 0, :, SSD_GN + g * n:SSD_GN + (g + 1) * n]
        bm_t = bm.astype(F32).T
        cb = lax.dot_general(cm, bm, NT_DIMS, preferred_element_type=F32)
        for r in range(0, SSD_HPG, 2):
            hds = (g * SSD_HPG + r, g * SSD_HPG + r + 1)
            pair = hds[0] // 2
            xb, xo = divmod(pair * LANE, xs_ref.shape[3])
            xp = xs_ref[xb, 0, :, xo:xo + LANE]
            zero = jnp.zeros_like(xp)
            x_blk = jnp.concatenate([jnp.where(left, xp, zero), jnp.where(left, zero, xp)], axis=0)
            h_prev = h_ref[pair]
            ch = jnp.dot(cm, h_prev.astype(BF16), preferred_element_type=F32)
            ch_blk = jnp.concatenate([jnp.where(left, ch, 0.0), jnp.where(left, 0.0, ch)], axis=0)
            w_parts, d_parts, u_parts = [], [], []
            for hd in hds:
                seg = cs[:, hd:hd + 1] - cs_t[hd:hd + 1, :]
                decay = jnp.where(mask, jnp.exp(seg), 0.0)
                w_parts.append(cb * decay * dt_t[hd:hd + 1, :])
                d_parts.append(jnp.where(eye, e_cs_t[hd:hd + 1, :], 0.0))
                u_parts.append(bm_t * w_end_t[hd:hd + 1, :])
            lhs = jnp.concatenate(w_parts + d_parts, axis=1).astype(BF16)
            rhs = jnp.concatenate([x_blk, ch_blk.astype(BF16)], axis=0)
            y = jnp.dot(lhs, rhs, preferred_element_type=F32)
            y = y + dlane_ref[0, :, pair * LANE:(pair + 1) * LANE] * xp.astype(F32)
            y_ref[0, 0, :, pair * LANE:(pair + 1) * LANE] = y.astype(y_ref.dtype)
            upd = jnp.dot(jnp.concatenate(u_parts, axis=1).astype(BF16), x_blk, preferred_element_type=F32)
            e_tot_pair = jnp.where(lane1, e_tot[:, hds[0]:hds[0] + 1], e_tot[:, hds[1]:hds[1] + 1])
            h_ref[pair] = e_tot_pair * h_prev + upd


def _scan_call(st, xc, dtr, par, dlane):
    q = SSD_CHUNK
    n_lat, n_ctx = st.s_lat // q, st.s_ctx // q
    n_real = n_lat + n_ctx
    n_all = st.s_pad // q
    cw = xc.shape[3]
    n_xblk = SSD_D_INNER // cw

    def chunk(d, s):
        fwd = jnp.where(s < n_ctx, n_lat + s, s - n_ctx)
        bwd = n_real - 1 - s
        return jnp.where(s < n_real, jnp.where(d == 0, fwd, bwd), s)

    idx = np.arange(q)
    tri = jnp.asarray(np.stack([idx[:, None] >= idx[None, :], idx[:, None] <= idx[None, :]]), F32)
    return pl.pallas_call(
        _scan_kernel,
        grid=(st.batch, 2, n_all),
        in_specs=[pl.BlockSpec((n_xblk, 1, q, cw), lambda b, d, s: (0, b, chunk(d, s), 0)),
                  pl.BlockSpec((1, 1, q, cw), lambda b, d, s: (n_xblk, b, chunk(d, s), 0)),
                  pl.BlockSpec((1, 1, q, LANE), lambda b, d, s: (d, b, chunk(d, s), 0)),
                  pl.BlockSpec((1, q, q), lambda b, d, s: (d, 0, 0)),
                  pl.BlockSpec((1, 8, LANE), lambda b, d, s: (d, 0, 0)),
                  pl.BlockSpec((1, 1, SSD_D_INNER), lambda b, d, s: (d, 0, 0))],
        out_specs=pl.BlockSpec((1, 1, q, SSD_D_INNER), lambda b, d, s: (d, b, chunk(d, s), 0)),
        out_shape=jax.ShapeDtypeStruct((2, st.batch, st.s_pad, SSD_D_INNER), BF16),
        scratch_shapes=[pltpu.VMEM((SSD_HEADS // 2, SSD_STATE, 2 * SSD_HEADDIM), F32)],
        compiler_params=_cparams(("parallel", "parallel", "arbitrary")),
        name="ssd_scan",
    )(xc, xc, dtr, tri, par, dlane)


def _attn_kernel(q_ref, k_ref, v_ref, lq_ref, lk_ref, g_ref, o_ref, qs_ref, vt_ref, m_ref, acc_ref,
                 s_ref, *, s_lat, s_ctx, lambda_init):
    qi = pl.program_id(2)
    tq = q_ref.shape[2]
    tk = ATT_TK
    tv = vt_ref.shape[2]
    n_sub = tk // tv
    hw = q_ref.shape[3]

    @pl.when(qi == 0)
    def _():
        def tr(c, carry):
            rows = v_ref[0, 0, pl.ds(pl.multiple_of(c * tv, tv), tv), :]
            vt_ref[c, 0:hw, :] = rows.astype(F32).T.astype(BF16)
            sub = lax.broadcasted_iota(jnp.int32, (vt_ref.shape[1] - hw, tv), 0)
            vt_ref[c, hw:, :] = jnp.where(sub == 0, 1.0, 0.0).astype(BF16)
            return carry

        lax.fori_loop(0, vt_ref.shape[0], tr, 0)

    qt = q_ref[0, 0].astype(F32).T
    row = lax.broadcasted_iota(jnp.int32, qt.shape, 0)
    qs_ref[:, 0:tq] = jnp.where(row < DA_HEAD_DIM, qt, 0.0).astype(BF16)
    qs_ref[:, tq:2 * tq] = jnp.where(row >= DA_HEAD_DIM, qt, 0.0).astype(BF16)
    m_ref[...] = jnp.full_like(m_ref, -jnp.inf)
    acc_ref[...] = jnp.zeros_like(acc_ref)

    def scores(k):
        return jnp.dot(k, qs_ref[...], preferred_element_type=F32)

    def update(read_s, vts):
        m_prev = m_ref[...]
        m_new = jnp.maximum(m_prev, jnp.max(read_s(), axis=0, keepdims=True))
        alpha = jnp.exp2(m_prev - m_new)
        pb = jnp.exp2(read_s() - m_new).astype(BF16)
        rows = pb.shape[0] // len(vts)
        pv = None
        for i, vt in enumerate(vts):
            d = jnp.dot(vt, pb[i * rows:(i + 1) * rows, :], preferred_element_type=F32)
            pv = d if pv is None else pv + d
        acc_ref[...] = alpha * acc_ref[...] + pv
        m_ref[...] = m_new

    n_lat = s_lat // tk

    def k_chunk(c):
        return k_ref[0, 0, pl.ds(pl.multiple_of(c * tk, tk), tk), :]

    def context_chunk():
        s_ctx_keys = scores(k_ref[0, 0, s_lat:s_lat + s_ctx, :])
        update(lambda: s_ctx_keys, [vt_ref[s_lat // tv, :, 0:s_ctx]])

    @pl.when(qi < s_lat // tq)
    def _():
        s_ref[0] = scores(k_chunk(0))
        context_chunk()

        def half_step(c, slot, prefetch=True):
            if prefetch:
                s_ref[1 - slot] = scores(k_chunk(c + 1))
            update(lambda: s_ref[slot], [vt_ref[c * n_sub + i] for i in range(n_sub)])

        def body(cp, carry):
            half_step(2 * cp, 0)
            half_step(2 * cp + 1, 1)
            return carry

        n_pairs = (n_lat - 1) // 2
        lax.fori_loop(0, n_pairs, body, 0)
        for c in range(2 * n_pairs, n_lat):
            half_step(c, c % 2, prefetch=c + 1 < n_lat)

    @pl.when(qi >= s_lat // tq)
    def _():
        context_chunk()

    lam = (jnp.exp(jnp.sum(lq_ref[0:1, :] * lk_ref[0:1, :], axis=-1, keepdims=True))
           - jnp.exp(jnp.sum(lq_ref[1:2, :] * lk_ref[1:2, :], axis=-1, keepdims=True)) + lambda_init)
    o_all = acc_ref[0:hw, :] / acc_ref[hw:hw + 1, :]
    o = (o_all[:, 0:tq] - lam * o_all[:, tq:2 * tq]).T
    o_ref[0, 0] = (_rms(o, g_ref[...]) * (1.0 - lambda_init)).astype(o_ref.dtype)


def _attn_call(st, qkv, lam_q, lam_k, subln_g, lambda_init):
    tq = ATT_TQ
    nq = st.s_pad // tq
    hw = 2 * DA_HEAD_DIM
    nh = DA_HEADS
    return pl.pallas_call(
        functools.partial(_attn_kernel, s_lat=st.s_lat, s_ctx=st.s_ctx, lambda_init=lambda_init),
        grid=(st.batch, nh, nq),
        in_specs=[pl.BlockSpec((1, 1, tq, hw), lambda b, h, i: (b, h, i, 0)),
                  pl.BlockSpec((1, 1, st.s_pad, hw), lambda b, h, i: (b, nh + h, 0, 0)),
                  pl.BlockSpec((1, 1, st.s_pad, hw), lambda b, h, i: (b, 2 * nh + h, 0, 0)),
                  pl.BlockSpec((2, DA_HEAD_DIM), lambda b, h, i: (0, 0)),
                  pl.BlockSpec((2, DA_HEAD_DIM), lambda b, h, i: (0, 0)),
                  pl.BlockSpec((1, hw), lambda b, h, i: (0, 0))],
        out_specs=pl.BlockSpec((1, 1, tq, hw), lambda b, h, i: (b, h, i, 0)),
        out_shape=jax.ShapeDtypeStruct((st.batch, nh, st.s_pad, hw), BF16),
        scratch_shapes=[pltpu.VMEM((hw, 2 * tq), BF16),
                        pltpu.VMEM((st.s_pad // ATT_TV, hw + ATT_SUM_ROWS, ATT_TV), BF16),
                        pltpu.VMEM((1, 2 * tq), F32),
                        pltpu.VMEM((hw + ATT_SUM_ROWS, 2 * tq), F32),
                        pltpu.VMEM((2, ATT_TK, 2 * tq), F32)],
        compiler_params=_cparams(("parallel", "parallel", "arbitrary")),
        name="diff_attn",
    )(qkv, qkv, qkv, lam_q, lam_k, subln_g.reshape(1, hw))


def _extract_top(s, dst_ref, n):
    w = s
    rank = jnp.full_like(s, float(n))
    for k in range(n):
        m = jnp.max(w, axis=0, keepdims=True)
        dst_ref[k:k + 1, :] = m
        hit = w == m
        rank = jnp.where(hit, float(k), rank)
        if k + 1 < n:
            w = jnp.where(hit, -jnp.inf, w)
    return rank


def _gate_kernel(s_ref, e1_ref, rank_ref, e2_ref, level_ref, a_ref, b_ref):
    k = PEER_TOPK
    n_lb = s_ref.shape[4]
    s1 = jnp.concatenate([s_ref[0, 0, 0, 0, lb] for lb in range(n_lb)], axis=1)
    s2 = jnp.concatenate([s_ref[0, 0, 0, 1, lb] for lb in range(n_lb)], axis=1)
    rank = _extract_top(s1, a_ref, k)
    _extract_top(s2, b_ref, k)
    a = a_ref[...]
    b = b_ref[...]
    cand = jnp.concatenate(
        [a[0:1, :] + b] + [a[i:i + 1, :] + b[0:8, :] for i in range(1, 8)] + [a[8:16, :] + b[0:1, :]],
        axis=0)
    w = cand
    tau = None
    for it in range(k):
        tau = jnp.max(w, axis=0, keepdims=True)
        if it + 1 < k:
            w = jnp.where(w == tau, -jnp.inf, w)
    top = a[0:1, :] + b[0:1, :]
    z = jnp.sum(jnp.where(cand >= tau, jnp.exp(cand - top), 0.0), axis=0, keepdims=True)
    level = jnp.zeros_like(s2)
    for j in range(k):
        level = jnp.where(s2 >= tau - a[j:j + 1, :], float(j + 1), level)
    blk_shape = e1_ref.shape[2:3] + e1_ref.shape[4:]
    e1_ref[0, 0, :, 0] = jnp.exp(s1 - a[0:1, :]).reshape(blk_shape)
    rank_ref[0, 0, :, 0] = rank.reshape(blk_shape)
    e2 = (jnp.exp(s2 - b[0:1, :]) / z).astype(BF16)
    level = level.astype(BF16)
    for lb in range(n_lb):
        e2_ref[0, 0, 0, lb] = e2[:, lb * LANE:(lb + 1) * LANE]
        level_ref[0, 0, 0, lb] = level[:, lb * LANE:(lb + 1) * LANE]


def _gate_call(st, scores):
    nk = PEER_NKEYS
    tt = TOK_TILE
    rows = PEER_EBLK // nk
    n_blk = nk // rows
    shp_blk = jax.ShapeDtypeStruct((st.batch, st.n_tok, n_blk, PEER_HEADS, rows, tt), F32)
    spec_blk = pl.BlockSpec((1, 1, n_blk, 1, rows, tt), lambda b, i, h: (b, i, 0, h, 0, 0))
    n_lb = tt // LANE
    shp_e2 = jax.ShapeDtypeStruct((st.batch, st.n_tok, PEER_HEADS, n_lb, nk, LANE), BF16)
    spec_e2 = pl.BlockSpec((1, 1, 1, n_lb, nk, LANE), lambda b, i, h: (b, i, h, 0, 0, 0))
    return pl.pallas_call(
        _gate_kernel,
        grid=(st.batch, st.n_tok, PEER_HEADS),
        in_specs=[pl.BlockSpec((1, 1, 1, 2, n_lb, nk, LANE), lambda b, i, h: (b, i, h, 0, 0, 0, 0))],
        out_specs=[spec_blk, spec_blk, spec_e2, spec_e2],
        out_shape=[shp_blk, shp_blk, shp_e2, shp_e2],
        scratch_shapes=[pltpu.VMEM((PEER_TOPK, TOK_TILE), F32), pltpu.VMEM((PEER_TOPK, TOK_TILE), F32)],
        compiler_params=_cparams(("parallel", "parallel", "parallel")),
        name="peer_gates",
    )(scores)


def _expert_kernel(hq_ref, u0_ref, un_ref, vt_ref, e1_ref, rank_ref, level_ref, e2_ref, xr_ref, gate_ref, o_ref,
                   acc_ref, act_ref, p_ref, hqt_ref, *, gi):
    e = pl.program_id(2)
    nk = PEER_NKEYS
    tt = hq_ref.shape[1]
    half = tt // 2
    sqrt_half = np.float32(math.sqrt(0.5))

    def put_act(slot, r0, r1, c0, c1, val):
        for lb in range(c0 // LANE, c1 // LANE):
            act_ref[slot, lb, r0:r1, :] = val[:, lb * LANE - c0:(lb + 1) * LANE - c0]

    @pl.when(e == 0)
    def _():
        acc_ref[...] = jnp.zeros_like(acc_ref)
        hqt_ref[...] = hq_ref[0].astype(F32).T.astype(BF16)
        put_act(0, 0, PEER_EBLK, 0, tt, jnp.dot(u0_ref[...], hqt_ref[...], preferred_element_type=F32))

    def gate_tile(cur, a_list, t0):
        cols = slice(t0, t0 + LANE)
        lb = t0 // LANE
        e1 = [[e1_ref[0, 0, 0, h, a:a + 1, cols].astype(BF16) for h in range(PEER_HEADS)] for a in a_list]
        rank = [[rank_ref[0, 0, 0, h, a:a + 1, cols].astype(BF16) for h in range(PEER_HEADS)] for a in a_list]
        zero = jnp.zeros((PEER_RB, LANE), BF16)
        for r0 in range(0, nk, PEER_RB):
            rows = slice(r0, r0 + PEER_RB)
            g = [None] * len(a_list)
            for h in range(PEER_HEADS):
                level = level_ref[0, 0, h, lb, rows, :]
                e2 = e2_ref[0, 0, h, lb, rows, :]
                for ai in range(len(a_list)):
                    term = e1[ai][h] * jnp.where(rank[ai][h] < level, e2, zero)
                    g[ai] = term if g[ai] is None else g[ai] + term
            for ai, a in enumerate(a_list):
                x = act_ref[cur, lb, a * nk + r0:a * nk + r0 + PEER_RB, :]
                gelu = 0.5 * x * (1.0 + lax.erf(x * sqrt_half))
                p_ref[lb, a * nk + r0:a * nk + r0 + PEER_RB, :] = g[ai] * gelu.astype(BF16)

    def stages(cur, nxt):
        rows_per = 2 * nk
        act_rows = 2 * rows_per
        for c0 in (0, half):
            for r0 in range(0, PEER_EBLK, rows_per):
                if r0 % act_rows == 0:
                    put_act(nxt, r0, r0 + act_rows, c0, c0 + half, jnp.dot(
                        un_ref[r0:r0 + act_rows, :], hqt_ref[:, c0:c0 + half],
                        preferred_element_type=F32))
                a_list = list(range(r0 // nk, (r0 + rows_per) // nk))
                for t0 in range(c0, c0 + half, LANE):
                    gate_tile(cur, a_list, t0)
                p_blk = jnp.concatenate([p_ref[lb, r0:r0 + rows_per, :]
                                         for lb in range(c0 // LANE, (c0 + half) // LANE)], axis=1)
                out = jnp.dot(vt_ref[0, :, r0:r0 + rows_per], p_blk, preferred_element_type=F32)
                for j, lb in enumerate(range(c0 // LANE, (c0 + half) // LANE)):
                    acc_ref[lb] += out[:, j * LANE:(j + 1) * LANE]

    for parity in range(2):
        pl.when(e % 2 == parity)(functools.partial(stages, parity, 1 - parity))

    @pl.when(e == pl.num_programs(2) - 1)
    def _():
        acc = jnp.concatenate([acc_ref[lb] for lb in range(acc_ref.shape[0])], axis=1)
        o_ref[0] = xr_ref[0] + gate_ref[0][gi:gi + 1, :] * acc.T


def _expert_call(st, hq, u, vt, e1, rank, level, e2, x, modtab, gi):
    n_exp, d = u.shape
    nk = PEER_NKEYS
    rows = PEER_EBLK // nk
    tt = TOK_TILE
    n_blk = n_exp // PEER_EBLK
    return pl.pallas_call(
        functools.partial(_expert_kernel, gi=gi),
        grid=(st.batch, st.n_tok, n_blk),
        in_specs=[pl.BlockSpec((1, tt, d), lambda b, i, e: (b, i, 0)),
                  pl.BlockSpec((PEER_EBLK, d), lambda b, i, e: (0, 0)),
                  pl.BlockSpec((PEER_EBLK, d), lambda b, i, e: (jnp.minimum(e + 1, n_blk - 1), 0)),
                  pl.BlockSpec((1, d, PEER_EBLK), lambda b, i, e: (e, 0, 0)),
                  pl.BlockSpec((1, 1, 1, PEER_HEADS, rows, tt), lambda b, i, e: (b, i, e, 0, 0, 0)),
                  pl.BlockSpec((1, 1, 1, PEER_HEADS, rows, tt), lambda b, i, e: (b, i, e, 0, 0, 0)),
                  pl.BlockSpec((1, 1, PEER_HEADS, tt // LANE, nk, LANE), lambda b, i, e: (b, i, 0, 0, 0, 0)),
                  pl.BlockSpec((1, 1, PEER_HEADS, tt // LANE, nk, LANE), lambda b, i, e: (b, i, 0, 0, 0, 0)),
                  pl.BlockSpec((1, tt, d), lambda b, i, e: (b, i, 0)),
                  pl.BlockSpec((1, 6, d), lambda b, i, e: (st.mod_row(b, i), 0, 0))],
        out_specs=pl.BlockSpec((1, tt, d), lambda b, i, e: (b, i, 0)),
        out_shape=jax.ShapeDtypeStruct((st.batch, st.s_pad, d), F32),
        scratch_shapes=[pltpu.VMEM((tt // LANE, d, LANE), F32),
                        pltpu.VMEM((2, tt // LANE, PEER_EBLK, LANE), F32),
                        pltpu.VMEM((tt // LANE, PEER_EBLK, LANE), BF16),
                        pltpu.VMEM((d, tt), BF16)],
        compiler_params=_cparams(("parallel", "parallel", "arbitrary")),
        name="peer_experts",
    )(hq, u, u, vt, e1, rank, level, e2, x, modtab)


def _final_norm_kernel(x_ref, g_ref, o_ref):
    o_ref[0] = _rms(x_ref[0], g_ref[...])


def _final_norm(st, x, g):
    d = st.d
    return pl.pallas_call(
        _final_norm_kernel,
        grid=(st.batch, st.n_lat_tok),
        in_specs=[pl.BlockSpec((1, TOK_TILE, d), lambda b, i: (b, i, 0)),
                  pl.BlockSpec((1, d), lambda b, i: (0, 0))],
        out_specs=pl.BlockSpec((1, TOK_TILE, d), lambda b, i: (b, i, 0)),
        out_shape=jax.ShapeDtypeStruct((st.batch, st.s_lat, d), F32),
        compiler_params=_cparams(("parallel", "parallel")),
        name="final_norm",
    )(x, g.reshape(1, d))


def _rope_tables(st):
    rows = st.s_lat // GRID_W
    row = jnp.repeat(jnp.arange(rows, dtype=F32), GRID_W)
    col = jnp.tile(jnp.arange(GRID_W, dtype=F32), rows)
    half = DA_HEAD_DIM // 2
    freqs = ROPE_BASE ** (-jnp.arange(0, half, 2, dtype=F32) / half)
    ang_r = row[:, None] * freqs
    ang_c = col[:, None] * freqs
    ang = jnp.concatenate([ang_r, ang_r, ang_c, ang_c] * 2, axis=-1)
    extra = st.s_pad - st.s_lat
    cos = jnp.concatenate([jnp.cos(ang), jnp.ones((extra, LANE), F32)], axis=0)
    sin = jnp.concatenate([jnp.sin(ang), jnp.zeros((extra, LANE), F32)], axis=0)
    return cos, sin


def _qkv_weight(w):
    d = w.shape[0]
    qk = DA_HEADS * DA_HEAD_DIM * 2

    def regroup(t):
        return t.reshape(d, 2, DA_HEADS, DA_HEAD_DIM).transpose(0, 2, 1, 3).reshape(d, qk)

    return jnp.concatenate([regroup(w[:, :qk]), regroup(w[:, qk:2 * qk]), w[:, 2 * qk:]], axis=1).astype(BF16)


def _blocked_vt(v):
    n_exp, d = v.shape
    return v.reshape(n_exp // PEER_EBLK, PEER_EBLK, d).transpose(0, 2, 1).astype(BF16)


def _dt_weight(w_dt):
    d = w_dt.shape[0]
    t = w_dt.reshape(d, 2, SSD_HEADS)
    return jnp.pad(t, ((0, 0), (0, 0), (0, LANE - SSD_HEADS))).reshape(d, 2 * LANE).astype(BF16)


def _scan_params(dt_bias, a_log, d_skip):
    def lanes(t):
        return jnp.pad(t.reshape(2, SSD_HEADS), ((0, 0), (0, LANE - SSD_HEADS)))

    rows = jnp.stack([lanes(dt_bias), lanes(a_log)], axis=1)
    dlane = jnp.repeat(d_skip.reshape(2, SSD_HEADS), SSD_HEADDIM, axis=1)[:, None, :]
    return jnp.pad(rows, ((0, 0), (0, 6), (0, 0))).astype(F32), dlane.astype(F32)


def kernel(x, c, ctx, c_ctx, ada_w, ada_b, norm_mix_g, norm_ffn_g, ssd_in_w, ssd_conv_w, ssd_conv_b,
           ssd_dt_bias, ssd_a_log, ssd_d, ssd_norm_g, ssd_out_w, attn_qkv_w, attn_lambda_q, attn_lambda_k,
           attn_subln_g, attn_out_w, peer_q_w, peer_keys, peer_u, peer_v, final_norm_g):
    batch, s_lat, d = x.shape
    s_ctx = ctx.shape[1]
    depth = ada_w.shape[0]
    st = _Stream(batch, s_lat, s_ctx, d)

    xs = jnp.concatenate([x, ctx, jnp.zeros((batch, st.s_pad - st.s_real, d), F32)], axis=1)

    cc = jnp.zeros((8, d), F32).at[:batch].set(c).at[batch].set(c_ctx)
    mods = _ada_call(cc, ada_w, ada_b).reshape(depth, 8, 6, d)
    cos, sin = _rope_tables(st)
    zxw = SSD_D_INNER + SSD_CONV_DIM

    for i in range(depth):
        lat = mods[i, :batch]
        con = jnp.broadcast_to(mods[i, batch], (batch, 6, d))
        modtab = jnp.stack([con, lat], axis=1).reshape(batch * 2, 6, d)
        jm = i // 2
        if i % 2 == 0:
            w_in = ssd_in_w[jm]
            zx = _proj_mod(st, xs, norm_mix_g[i], modtab, w_in[:, :zxw].astype(BF16), 0, 1, 1024, BF16,
                           "ssd_in_proj")
            dtr = _proj_mod(st, xs, norm_mix_g[i], modtab, _dt_weight(w_in[:, zxw:]), 0, 1, LANE, F32,
                            "ssd_dt_proj")
            xc = _conv_call(st, zx, ssd_conv_w[jm], ssd_conv_b[jm])
            y2 = _scan_call(st, xc, dtr, *_scan_params(ssd_dt_bias[jm], ssd_a_log[jm], ssd_d[jm]))
            xs = _proj_gate_resid(st, y2, zx, ssd_norm_g[jm], ssd_out_w[jm].astype(BF16), xs, modtab, 2)
        else:
            lambda_init = 0.8 - 0.6 * math.exp(-0.3 * i)
            qkv = _proj_qkv(st, xs, norm_mix_g[i], modtab, _qkv_weight(attn_qkv_w[jm]), cos, sin, 0, 1)
            o = _attn_call(st, qkv, attn_lambda_q[jm], attn_lambda_k[jm], attn_subln_g[jm], lambda_init)
            xs = _proj_heads_resid(st, o, attn_out_w[jm].astype(BF16), xs, modtab, 2)
        scores, hq = _proj_scores(st, xs, norm_ffn_g[i], modtab, peer_q_w[i].astype(BF16), peer_keys[i], 3, 4)
        e1, rank, e2, level = _gate_call(st, scores)
        xs = _expert_call(st, hq, peer_u[i].astype(BF16), _blocked_vt(peer_v[i]), e1, rank, level, e2,
                          xs, modtab, 5)
    return _final_norm(st, xs, final_norm_g)
```

```python
import functools
import math

import numpy as np
import jax
import jax.numpy as jnp
from jax import lax
from jax.experimental import pallas as pl
from jax.experimental.pallas import tpu as pltpu

F32 = jnp.float32
BF16 = jnp.bfloat16

EPS = 1e-6
GRID_W = 64
ROPE_BASE = 10000.0

SSD_HEADDIM = 64
SSD_GROUPS = 4
SSD_HPG = 8
SSD_HEADS = SSD_GROUPS * SSD_HPG
SSD_STATE = 128
SSD_CONV_W = 5
SSD_CHUNK = 128
SSD_D_INNER = SSD_HEADS * SSD_HEADDIM
SSD_GN = SSD_GROUPS * SSD_STATE
SSD_CONV_DIM = SSD_D_INNER + 2 * SSD_GN

DA_HEADS = 8
DA_HEAD_DIM = 64

PEER_HEADS = 8
PEER_NKEYS = 128
PEER_TOPK = 16
PEER_EBLK = 1024
PEER_RB = 64

LANE = 128
TOK_TILE = 512
CONV_TILE = 256
ATT_TQ = 512
ATT_TK = 1024
ATT_TV = 512
ATT_SUM_ROWS = 16
VMEM_LIMIT = 56 * 1024 * 1024

NT_DIMS = (((1,), (1,)), ((), ()))


def _cparams(sem):
    return pltpu.CompilerParams(dimension_semantics=sem, vmem_limit_bytes=VMEM_LIMIT)


def _rms(xf, g):
    return xf * lax.rsqrt(jnp.mean(xf * xf, axis=-1, keepdims=True) + EPS) * g


def _silu(x):
    return x * (1.0 / (1.0 + jnp.exp(-x)))


def _ada_kernel(c_ref, w_ref, b_ref, o_ref):
    c = c_ref[...]
    o_ref[0] = jnp.dot(_silu(c), w_ref[0], preferred_element_type=F32,
                       precision=lax.Precision.HIGHEST) + b_ref[0]


def _ada_call(cc, ada_w, ada_b):
    depth, d, n = ada_w.shape
    tn = 1536
    return pl.pallas_call(
        _ada_kernel,
        grid=(depth, n // tn),
        in_specs=[pl.BlockSpec((8, d), lambda l, j: (0, 0)),
                  pl.BlockSpec((1, d, tn), lambda l, j: (l, 0, j)),
                  pl.BlockSpec((1, 1, tn), lambda l, j: (l, 0, j))],
        out_specs=pl.BlockSpec((1, 8, tn), lambda l, j: (l, 0, j)),
        out_shape=jax.ShapeDtypeStruct((depth, 8, n), F32),
        compiler_params=_cparams(("parallel", "parallel")),
        name="ada_mod",
    )(cc, ada_w, ada_b.reshape(depth, 1, n))


def _mod_prologue(x_ref, g_ref, mod_ref, hn_ref, sh, sc):
    m = mod_ref[0]
    hn = _rms(x_ref[0], g_ref[...]) * (1.0 + m[sc:sc + 1, :]) + m[sh:sh + 1, :]
    hn_ref[...] = hn.astype(BF16)


def _proj_mod_kernel(x_ref, g_ref, mod_ref, w_ref, o_ref, hn_ref, *, sh, sc):
    @pl.when(pl.program_id(2) == 0)
    def _():
        _mod_prologue(x_ref, g_ref, mod_ref, hn_ref, sh, sc)

    o_ref[0, 0] = jnp.dot(hn_ref[...], w_ref[...], preferred_element_type=F32).astype(o_ref.dtype)


def _rope_block(a, cos, sin, lane):
    fwd = pltpu.roll(a, LANE - 16, axis=1)
    bwd = pltpu.roll(a, 16, axis=1)
    rot = jnp.where(lane % 32 < 16, -fwd, bwd)
    return a * cos + rot * sin


def _proj_qkv_kernel(x_ref, g_ref, mod_ref, w_ref, cos_ref, sin_ref, o_ref, hn_ref, *, sh, sc):
    j = pl.program_id(2)

    @pl.when(j == 0)
    def _():
        _mod_prologue(x_ref, g_ref, mod_ref, hn_ref, sh, sc)

    acc = jnp.dot(hn_ref[...], w_ref[...], preferred_element_type=F32)
    tn = acc.shape[1]

    @pl.when(j < 2)
    def _():
        cos = cos_ref[...]
        sin = sin_ref[...]
        lane = lax.broadcasted_iota(jnp.int32, cos.shape, 1)
        scale = jnp.where(j == 0, DA_HEAD_DIM ** -0.5 * math.log2(math.e), 1.0).astype(F32)
        for cb in range(tn // LANE):
            a = acc[:, cb * LANE:(cb + 1) * LANE]
            o_ref[0, cb] = (_rope_block(a, cos, sin, lane) * scale).astype(o_ref.dtype)

    @pl.when(j >= 2)
    def _():
        for cb in range(tn // LANE):
            o_ref[0, cb] = acc[:, cb * LANE:(cb + 1) * LANE].astype(o_ref.dtype)


def _proj_gate_resid_kernel(yf_ref, yb_ref, z_ref, g_ref, w_ref, xr_ref, gate_ref, o_ref, hn_ref,
                            *, gi):
    @pl.when(pl.program_id(2) == 0)
    def _():
        z = jnp.concatenate([z_ref[c, 0] for c in range(z_ref.shape[0])], axis=1).astype(F32)
        u = (yf_ref[0, 0].astype(F32) + yb_ref[0, 0].astype(F32)) * _silu(z)
        hn_ref[...] = _rms(u, g_ref[...]).astype(BF16)

    acc = jnp.dot(hn_ref[...], w_ref[...], preferred_element_type=F32)
    o_ref[0] = xr_ref[0] + gate_ref[0][gi:gi + 1, :] * acc


def _proj_heads_resid_kernel(h_ref, w_ref, xr_ref, gate_ref, o_ref, *, gi):
    h = jnp.concatenate([h_ref[0, hh] for hh in range(h_ref.shape[1])], axis=1)
    acc = jnp.dot(h, w_ref[...], preferred_element_type=F32)
    o_ref[0] = xr_ref[0] + gate_ref[0][gi:gi + 1, :] * acc


def _proj_scores_kernel(x_ref, g_ref, mod_ref, w_ref, keys_ref, e1_ref, rank_ref, e2_ref, level_ref, hq_ref,
                        hn_ref, a_ref, b_ref, *, sh, sc):
    @pl.when(pl.program_id(2) == 0)
    def _():
        _mod_prologue(x_ref, g_ref, mod_ref, hn_ref, sh, sc)
        hq_ref[0] = hn_ref[...]

    q = jnp.dot(hn_ref[...], w_ref[...], preferred_element_type=F32)
    half = PEER_NKEYS
    s1, s2 = [lax.dot_general(keys_ref[0, c], q[:, c * half:(c + 1) * half], NT_DIMS,
                              preferred_element_type=F32, precision=lax.Precision.HIGHEST)
              for c in range(2)]
    _gate_body(s1, s2, e1_ref, rank_ref, e2_ref, level_ref, a_ref, b_ref)


class _Stream:
    def __init__(self, batch, s_lat, s_ctx, d):
        assert s_lat % TOK_TILE == 0 and s_ctx % CONV_TILE == 0 and s_lat % GRID_W == 0
        self.batch, self.s_lat, self.s_ctx, self.d = batch, s_lat, s_ctx, d
        self.s_real = s_lat + s_ctx
        self.s_pad = -(-self.s_real // TOK_TILE) * TOK_TILE
        self.n_tok = self.s_pad // TOK_TILE
        self.n_lat_tok = s_lat // TOK_TILE

    def mod_row(self, b, i):
        return b * 2 + jnp.where(i < self.n_lat_tok, 1, 0)


def _mod_specs(st, d):
    return [pl.BlockSpec((1, TOK_TILE, d), lambda b, i, j: (b, i, 0)),
            pl.BlockSpec((1, d), lambda b, i, j: (0, 0)),
            pl.BlockSpec((1, 6, d), lambda b, i, j: (st.mod_row(b, i), 0, 0))]


def _proj_mod(st, x, g, modtab, w, sh, sc, tn, out_dtype, name):
    d, n = w.shape
    return pl.pallas_call(
        functools.partial(_proj_mod_kernel, sh=sh, sc=sc),
        grid=(st.batch, st.n_tok, n // tn),
        in_specs=_mod_specs(st, d) + [pl.BlockSpec((d, tn), lambda b, i, j: (0, j))],
        out_specs=pl.BlockSpec((1, 1, TOK_TILE, tn), lambda b, i, j: (j, b, i, 0)),
        out_shape=jax.ShapeDtypeStruct((n // tn, st.batch, st.s_pad, tn), out_dtype),
        scratch_shapes=[pltpu.VMEM((TOK_TILE, d), BF16)],
        compiler_params=_cparams(("parallel", "parallel", "arbitrary")),
        name=name,
    )(x, g.reshape(1, d), modtab, w)


def _proj_qkv(st, x, g, modtab, w, cos, sin, sh, sc):
    d, n = w.shape
    tn = 1024
    return pl.pallas_call(
        functools.partial(_proj_qkv_kernel, sh=sh, sc=sc),
        grid=(st.batch, st.n_tok, n // tn),
        in_specs=_mod_specs(st, d) + [
            pl.BlockSpec((d, tn), lambda b, i, j: (0, j)),
            pl.BlockSpec((TOK_TILE, LANE), lambda b, i, j: (i, 0)),
            pl.BlockSpec((TOK_TILE, LANE), lambda b, i, j: (i, 0))],
        out_specs=pl.BlockSpec((1, tn // LANE, TOK_TILE, LANE), lambda b, i, j: (b, j, i, 0)),
        out_shape=jax.ShapeDtypeStruct((st.batch, n // LANE, st.s_pad, LANE), BF16),
        scratch_shapes=[pltpu.VMEM((TOK_TILE, d), BF16)],
        compiler_params=_cparams(("parallel", "parallel", "arbitrary")),
        name="attn_qkv_rope",
    )(x, g.reshape(1, d), modtab, w, cos, sin)


def _proj_gate_resid(st, y2, zx, g, w, x, modtab, gi):
    k, n = w.shape
    tn = n
    return pl.pallas_call(
        functools.partial(_proj_gate_resid_kernel, gi=gi),
        grid=(st.batch, st.n_tok, n // tn),
        in_specs=[pl.BlockSpec((1, 1, TOK_TILE, k), lambda b, i, j: (0, b, i, 0)),
                  pl.BlockSpec((1, 1, TOK_TILE, k), lambda b, i, j: (1, b, i, 0)),
                  pl.BlockSpec((k // zx.shape[3], 1, TOK_TILE, zx.shape[3]), lambda b, i, j: (0, b, i, 0)),
                  pl.BlockSpec((1, k), lambda b, i, j: (0, 0)),
                  pl.BlockSpec((k, tn), lambda b, i, j: (0, j)),
                  pl.BlockSpec((1, TOK_TILE, tn), lambda b, i, j: (b, i, j)),
                  pl.BlockSpec((1, 6, tn), lambda b, i, j: (st.mod_row(b, i), 0, j))],
        out_specs=pl.BlockSpec((1, TOK_TILE, tn), lambda b, i, j: (b, i, j)),
        out_shape=jax.ShapeDtypeStruct((st.batch, st.s_pad, n), F32),
        scratch_shapes=[pltpu.VMEM((TOK_TILE, k), BF16)],
        compiler_params=_cparams(("parallel", "parallel", "arbitrary")),
        name="ssd_out_proj",
    )(y2, y2, zx, g.reshape(1, k), w, x, modtab)


def _proj_heads_resid(st, h, w, x, modtab, gi):
    k, n = w.shape
    tn = n
    return pl.pallas_call(
        functools.partial(_proj_heads_resid_kernel, gi=gi),
        grid=(st.batch, st.n_tok, n // tn),
        in_specs=[pl.BlockSpec((1, k // LANE, TOK_TILE, LANE), lambda b, i, j: (b, 0, i, 0)),
                  pl.BlockSpec((k, tn), lambda b, i, j: (0, j)),
                  pl.BlockSpec((1, TOK_TILE, tn), lambda b, i, j: (b, i, j)),
                  pl.BlockSpec((1, 6, tn), lambda b, i, j: (st.mod_row(b, i), 0, j))],
        out_specs=pl.BlockSpec((1, TOK_TILE, tn), lambda b, i, j: (b, i, j)),
        out_shape=jax.ShapeDtypeStruct((st.batch, st.s_pad, n), F32),
        compiler_params=_cparams(("parallel", "parallel", "arbitrary")),
        name="attn_out_proj",
    )(h, w, x, modtab)


def _proj_scores(st, x, g, modtab, w, keys, sh, sc):
    d, n = w.shape
    nk = PEER_NKEYS
    tt = TOK_TILE
    rows = PEER_EBLK // nk
    n_blk = nk // rows
    n_lb = tt // LANE
    shp_blk = jax.ShapeDtypeStruct((st.batch, st.n_tok, n_blk, PEER_HEADS, rows, tt), F32)
    spec_blk = pl.BlockSpec((1, 1, n_blk, 1, rows, tt), lambda b, i, h: (b, i, 0, h, 0, 0))
    shp_e2 = jax.ShapeDtypeStruct((st.batch, st.n_tok, PEER_HEADS, n_lb, nk, LANE), BF16)
    spec_e2 = pl.BlockSpec((1, 1, 1, n_lb, nk, LANE), lambda b, i, h: (b, i, h, 0, 0, 0))
    return pl.pallas_call(
        functools.partial(_proj_scores_kernel, sh=sh, sc=sc),
        grid=(st.batch, st.n_tok, PEER_HEADS),
        in_specs=_mod_specs(st, d) + [
            pl.BlockSpec((d, 2 * nk), lambda b, i, j: (0, j)),
            pl.BlockSpec((1, 2, nk, nk), lambda b, i, j: (j, 0, 0, 0))],
        out_specs=[spec_blk, spec_blk, spec_e2, spec_e2,
                   pl.BlockSpec((1, tt, d), lambda b, i, j: (b, i, 0))],
        out_shape=[shp_blk, shp_blk, shp_e2, shp_e2,
                   jax.ShapeDtypeStruct((st.batch, st.s_pad, d), BF16)],
        scratch_shapes=[pltpu.VMEM((tt, d), BF16),
                        pltpu.VMEM((PEER_TOPK, tt), F32), pltpu.VMEM((PEER_TOPK, tt), F32)],
        compiler_params=_cparams(("parallel", "parallel", "arbitrary")),
        name="peer_scores_gates",
    )(x, g.reshape(1, d), modtab, w, keys)


def _conv_kernel(x_ref, prev_ref, next_ref, w_ref, b_ref, o_ref, buf_ref, *, seg_starts, n_tiles):
    i = pl.program_id(1)
    pad = SSD_CONV_W // 2
    first = functools.reduce(jnp.logical_or, [i == s for s in seg_starts])
    last = functools.reduce(jnp.logical_or, [i == s - 1 for s in seg_starts[1:] + (n_tiles,)])
    t = x_ref.shape[2]
    buf_ref[8:8 + t, :] = x_ref[0, 0].astype(F32)
    buf_ref[0:8, :] = jnp.where(first, 0.0, prev_ref[0, 0].astype(F32)[8:16, :])
    buf_ref[8 + t:16 + t, :] = jnp.where(last, 0.0, next_ref[0, 0].astype(F32)[0:8, :])
    w = w_ref[...]
    acc = b_ref[...] + w[0:1, :] * buf_ref[8 - pad:8 - pad + t, :]
    for k in range(1, SSD_CONV_W):
        acc = acc + w[k:k + 1, :] * buf_ref[8 - pad + k:8 - pad + k + t, :]
    o_ref[0, 0] = _silu(acc).astype(o_ref.dtype)


def _conv_call(st, zx, conv_w, conv_b):
    cw = 1024
    n_cblk = SSD_CONV_DIM // cw
    off = SSD_D_INNER // cw
    n_tiles = st.s_pad // CONV_TILE
    hb = CONV_TILE // 16
    n_hblk = st.s_pad // 16
    seg_starts = (0, st.s_lat // CONV_TILE, st.s_real // CONV_TILE)
    w8 = jnp.zeros((8, SSD_CONV_DIM), F32).at[:SSD_CONV_W].set(conv_w)
    return pl.pallas_call(
        functools.partial(_conv_kernel, seg_starts=seg_starts, n_tiles=n_tiles),
        grid=(st.batch, n_tiles, n_cblk),
        in_specs=[pl.BlockSpec((1, 1, CONV_TILE, cw), lambda b, i, c: (off + c, b, i, 0)),
                  pl.BlockSpec((1, 1, 16, cw), lambda b, i, c: (off + c, b, jnp.maximum(i * hb - 1, 0), 0)),
                  pl.BlockSpec((1, 1, 16, cw),
                               lambda b, i, c: (off + c, b, jnp.minimum((i + 1) * hb, n_hblk - 1), 0)),
                  pl.BlockSpec((8, cw), lambda b, i, c: (0, c)),
                  pl.BlockSpec((1, cw), lambda b, i, c: (0, c))],
        out_specs=pl.BlockSpec((1, 1, CONV_TILE, cw), lambda b, i, c: (c, b, i, 0)),
        out_shape=jax.ShapeDtypeStruct((n_cblk, st.batch, st.s_pad, cw), BF16),
        scratch_shapes=[pltpu.VMEM((CONV_TILE + 16, cw), F32)],
        compiler_params=_cparams(("parallel", "parallel", "parallel")),
        name="ssd_conv",
    )(zx, zx, zx, w8, conv_b.reshape(1, SSD_CONV_DIM))


def _scan_kernel(xs_ref, bc_ref, dt_ref, tri_ref, par_ref, dlane_ref, y_ref, h_ref):
    @pl.when(pl.program_id(2) == 0)
    def _():
        h_ref[...] = jnp.zeros_like(h_ref)

    par = par_ref[0]
    raw = dt_ref[0, 0] + par[0:1, :]
    dt = jnp.maximum(raw, 0.0) + jnp.log1p(jnp.exp(-jnp.abs(raw)))
    a = -jnp.exp(par[1:2, :])
    dta = dt * a
    tri = tri_ref[0]
    cs = jnp.dot(tri, dta, preferred_element_type=F32, precision=lax.Precision.HIGHEST)
    tot = jnp.sum(dta, axis=0, keepdims=True)
    e_tot = jnp.exp(tot)
    w_end = jnp.exp(tot - cs) * dt
    cs_t = cs.T
    dt_t = dt.T
    w_end_t = w_end.T
    mask = tri > 0.5

    e_cs_t = jnp.exp(cs_t)
    n = SSD_STATE
    q = tri.shape[0]
    lane = lax.broadcasted_iota(jnp.int32, (q, LANE), 1)
    left = lane < SSD_HEADDIM
    eye = lax.broadcasted_iota(jnp.int32, (q, q), 0) == lax.broadcasted_iota(jnp.int32, (q, q), 1)
    lane1 = lax.broadcasted_iota(jnp.int32, (1, LANE), 1) < SSD_HEADDIM
    for g in range(SSD_GROUPS):
        bm = bc_ref[0, 0, :, g * n:(g + 1) * n]
        cm = bc_ref[0, 0, :, SSD_GN + g * n:SSD_GN + (g + 1) * n]
        bm_t = bm.astype(F32).T
        cb = lax.dot_general(cm, bm, NT_DIMS, preferred_element_type=F32)
        for r in range(0, SSD_HPG, 2):
            hds = (g * SSD_HPG + r, g * SSD_HPG + r + 1)
            pair = hds[0] // 2
            xb, xo = divmod(pair * LANE, xs_ref.shape[3])
            xp = xs_ref[xb, 0, :, xo:xo + LANE]
            zero = jnp.zeros_like(xp)
            x_blk = jnp.concatenate([jnp.where(left, xp, zero), jnp.where(left, zero, xp)], axis=0)
            h_prev = h_ref[pair]
            ch = jnp.dot(cm, h_prev.astype(BF16), preferred_element_type=F32)
            ch_blk = jnp.concatenate([jnp.where(left, ch, 0.0), jnp.where(left, 0.0, ch)], axis=0)
            w_parts, d_parts, u_parts = [], [], []
            for hd in hds:
                seg = cs[:, hd:hd + 1] - cs_t[hd:hd + 1, :]
                decay = jnp.where(mask, jnp.exp(seg), 0.0)
                w_parts.append(cb * decay * dt_t[hd:hd + 1, :])
                d_parts.append(jnp.where(eye, e_cs_t[hd:hd + 1, :], 0.0))
                u_parts.append(bm_t * w_end_t[hd:hd + 1, :])
            lhs = jnp.concatenate(w_parts + d_parts, axis=1).astype(BF16)
            rhs = jnp.concatenate([x_blk, ch_blk.astype(BF16)], axis=0)
            y = jnp.dot(lhs, rhs, preferred_element_type=F32)
            y = y + dlane_ref[0, :, pair * LANE:(pair + 1) * LANE] * xp.astype(F32)
            y_ref[0, 0, :, pair * LANE:(pair + 1) * LANE] = y.astype(y_ref.dtype)
            upd = jnp.dot(jnp.concatenate(u_parts, axis=1).astype(BF16), x_blk, preferred_element_type=F32)
            e_tot_pair = jnp.where(lane1, e_tot[:, hds[0]:hds[0] + 1], e_tot[:, hds[1]:hds[1] + 1])
            h_ref[pair] = e_tot_pair * h_prev + upd


def _scan_call(st, xc, dtr, par, dlane):
    q = SSD_CHUNK
    n_lat, n_ctx = st.s_lat // q, st.s_ctx // q
    n_real = n_lat + n_ctx
    n_all = st.s_pad // q
    cw = xc.shape[3]
    n_xblk = SSD_D_INNER // cw

    def chunk(d, s):
        fwd = jnp.where(s < n_ctx, n_lat + s, s - n_ctx)
        bwd = n_real - 1 - s
        return jnp.where(s < n_real, jnp.where(d == 0, fwd, bwd), s)

    idx = np.arange(q)
    tri = jnp.asarray(np.stack([idx[:, None] >= idx[None, :], idx[:, None] <= idx[None, :]]), F32)
    return pl.pallas_call(
        _scan_kernel,
        grid=(st.batch, 2, n_all),
        in_specs=[pl.BlockSpec((n_xblk, 1, q, cw), lambda b, d, s: (0, b, chunk(d, s), 0)),
                  pl.BlockSpec((1, 1, q, cw), lambda b, d, s: (n_xblk, b, chunk(d, s), 0)),
                  pl.BlockSpec((1, 1, q, LANE), lambda b, d, s: (d, b, chunk(d, s), 0)),
                  pl.BlockSpec((1, q, q), lambda b, d, s: (d, 0, 0)),
                  pl.BlockSpec((1, 8, LANE), lambda b, d, s: (d, 0, 0)),
                  pl.BlockSpec((1, 1, SSD_D_INNER), lambda b, d, s: (d, 0, 0))],
        out_specs=pl.BlockSpec((1, 1, q, SSD_D_INNER), lambda b, d, s: (d, b, chunk(d, s), 0)),
        out_shape=jax.ShapeDtypeStruct((2, st.batch, st.s_pad, SSD_D_INNER), BF16),
        scratch_shapes=[pltpu.VMEM((SSD_HEADS // 2, SSD_STATE, 2 * SSD_HEADDIM), F32)],
        compiler_params=_cparams(("parallel", "parallel", "arbitrary")),
        name="ssd_scan",
    )(xc, xc, dtr, tri, par, dlane)


def _attn_kernel(q_ref, k_ref, v_ref, lq_ref, lk_ref, g_ref, o_ref, qs_ref, vt_ref, m_ref, acc_ref,
                 s_ref, *, s_lat, s_ctx, lambda_init):
    qi = pl.program_id(2)
    tq = q_ref.shape[2]
    tk = ATT_TK
    tv = vt_ref.shape[2]
    n_sub = tk // tv
    hw = q_ref.shape[3]

    @pl.when(qi == 0)
    def _():
        def tr(c, carry):
            rows = v_ref[0, 0, pl.ds(pl.multiple_of(c * tv, tv), tv), :]
            vt_ref[c, 0:hw, :] = rows.astype(F32).T.astype(BF16)
            sub = lax.broadcasted_iota(jnp.int32, (vt_ref.shape[1] - hw, tv), 0)
            vt_ref[c, hw:, :] = jnp.where(sub == 0, 1.0, 0.0).astype(BF16)
            return carry

        lax.fori_loop(0, vt_ref.shape[0], tr, 0)

    qt = q_ref[0, 0].astype(F32).T
    row = lax.broadcasted_iota(jnp.int32, qt.shape, 0)
    qs_ref[:, 0:tq] = jnp.where(row < DA_HEAD_DIM, qt, 0.0).astype(BF16)
    qs_ref[:, tq:2 * tq] = jnp.where(row >= DA_HEAD_DIM, qt, 0.0).astype(BF16)
    m_ref[...] = jnp.full_like(m_ref, -jnp.inf)
    acc_ref[...] = jnp.zeros_like(acc_ref)

    def scores(k):
        return jnp.dot(k, qs_ref[...], preferred_element_type=F32)

    def update(read_s, vts):
        m_prev = m_ref[...]
        m_new = jnp.maximum(m_prev, jnp.max(read_s(), axis=0, keepdims=True))
        alpha = jnp.exp2(m_prev - m_new)
        pb = jnp.exp2(read_s() - m_new).astype(BF16)
        rows = pb.shape[0] // len(vts)
        pv = None
        for i, vt in enumerate(vts):
            d = jnp.dot(vt, pb[i * rows:(i + 1) * rows, :], preferred_element_type=F32)
            pv = d if pv is None else pv + d
        acc_ref[...] = alpha * acc_ref[...] + pv
        m_ref[...] = m_new

    n_lat = s_lat // tk

    def k_chunk(c):
        return k_ref[0, 0, pl.ds(pl.multiple_of(c * tk, tk), tk), :]

    def context_chunk():
        s_ctx_keys = scores(k_ref[0, 0, s_lat:s_lat + s_ctx, :])
        update(lambda: s_ctx_keys, [vt_ref[s_lat // tv, :, 0:s_ctx]])

    @pl.when(qi < s_lat // tq)
    def _():
        s_ref[0] = scores(k_chunk(0))
        context_chunk()

        def half_step(c, slot, prefetch=True):
            if prefetch:
                s_ref[1 - slot] = scores(k_chunk(c + 1))
            update(lambda: s_ref[slot], [vt_ref[c * n_sub + i] for i in range(n_sub)])

        def body(cp, carry):
            half_step(2 * cp, 0)
            half_step(2 * cp + 1, 1)
            return carry

        n_pairs = (n_lat - 1) // 2
        lax.fori_loop(0, n_pairs, body, 0)
        for c in range(2 * n_pairs, n_lat):
            half_step(c, c % 2, prefetch=c + 1 < n_lat)

    @pl.when(qi >= s_lat // tq)
    def _():
        context_chunk()

    lam = (jnp.exp(jnp.sum(lq_ref[0:1, :] * lk_ref[0:1, :], axis=-1, keepdims=True))
           - jnp.exp(jnp.sum(lq_ref[1:2, :] * lk_ref[1:2, :], axis=-1, keepdims=True)) + lambda_init)
    o_all = acc_ref[0:hw, :] / acc_ref[hw:hw + 1, :]
    o = (o_all[:, 0:tq] - lam * o_all[:, tq:2 * tq]).T
    o_ref[0, 0] = (_rms(o, g_ref[...]) * (1.0 - lambda_init)).astype(o_ref.dtype)


def _attn_call(st, qkv, lam_q, lam_k, subln_g, lambda_init):
    tq = ATT_TQ
    nq = st.s_pad // tq
    hw = 2 * DA_HEAD_DIM
    nh = DA_HEADS
    return pl.pallas_call(
        functools.partial(_attn_kernel, s_lat=st.s_lat, s_ctx=st.s_ctx, lambda_init=lambda_init),
        grid=(st.batch, nh, nq),
        in_specs=[pl.BlockSpec((1, 1, tq, hw), lambda b, h, i: (b, h, i, 0)),
                  pl.BlockSpec((1, 1, st.s_pad, hw), lambda b, h, i: (b, nh + h, 0, 0)),
                  pl.BlockSpec((1, 1, st.s_pad, hw), lambda b, h, i: (b, 2 * nh + h, 0, 0)),
                  pl.BlockSpec((2, DA_HEAD_DIM), lambda b, h, i: (0, 0)),
                  pl.BlockSpec((2, DA_HEAD_DIM), lambda b, h, i: (0, 0)),
                  pl.BlockSpec((1, hw), lambda b, h, i: (0, 0))],
        out_specs=pl.BlockSpec((1, 1, tq, hw), lambda b, h, i: (b, h, i, 0)),
        out_shape=jax.ShapeDtypeStruct((st.batch, nh, st.s_pad, hw), BF16),
        scratch_shapes=[pltpu.VMEM((hw, 2 * tq), BF16),
                        pltpu.VMEM((st.s_pad // ATT_TV, hw + ATT_SUM_ROWS, ATT_TV), BF16),
                        pltpu.VMEM((1, 2 * tq), F32),
                        pltpu.VMEM((hw + ATT_SUM_ROWS, 2 * tq), F32),
                        pltpu.VMEM((2, ATT_TK, 2 * tq), F32)],
        compiler_params=_cparams(("parallel", "parallel", "arbitrary")),
        name="diff_attn",
    )(qkv, qkv, qkv, lam_q, lam_k, subln_g.reshape(1, hw))


def _extract_top(s, dst_ref, n):
    w = s
    rank = jnp.full_like(s, float(n))
    for k in range(n):
        m = jnp.max(w, axis=0, keepdims=True)
        dst_ref[k:k + 1, :] = m
        hit = w == m
        rank = jnp.where(hit, float(k), rank)
        if k + 1 < n:
            w = jnp.where(hit, -jnp.inf, w)
    return rank


def _gate_body(s1, s2, e1_ref, rank_ref, e2_ref, level_ref, a_ref, b_ref):
    k = PEER_TOPK
    n_lb = e2_ref.shape[3]
    rank = _extract_top(s1, a_ref, k)
    _extract_top(s2, b_ref, k)
    a = a_ref[...]
    b = b_ref[...]
    cand = jnp.concatenate(
        [a[0:1, :] + b] + [a[i:i + 1, :] + b[0:8, :] for i in range(1, 8)] + [a[8:16, :] + b[0:1, :]],
        axis=0)
    w = cand
    tau = None
    for it in range(k):
        tau = jnp.max(w, axis=0, keepdims=True)
        if it + 1 < k:
            w = jnp.where(w == tau, -jnp.inf, w)
    top = a[0:1, :] + b[0:1, :]
    z = jnp.sum(jnp.where(cand >= tau, jnp.exp(cand - top), 0.0), axis=0, keepdims=True)
    level = jnp.zeros_like(s2)
    for j in range(k):
        level = jnp.where(s2 >= tau - a[j:j + 1, :], float(j + 1), level)
    blk_shape = e1_ref.shape[2:3] + e1_ref.shape[4:]
    e1_ref[0, 0, :, 0] = jnp.exp(s1 - a[0:1, :]).reshape(blk_shape)
    rank_ref[0, 0, :, 0] = rank.reshape(blk_shape)
    e2 = (jnp.exp(s2 - b[0:1, :]) / z).astype(BF16)
    level = level.astype(BF16)
    for lb in range(n_lb):
        e2_ref[0, 0, 0, lb] = e2[:, lb * LANE:(lb + 1) * LANE]
        level_ref[0, 0, 0, lb] = level[:, lb * LANE:(lb + 1) * LANE]


def _expert_kernel(hq_ref, u0_ref, un_ref, vt_ref, e1_ref, rank_ref, level_ref, e2_ref, xr_ref, gate_ref, o_ref,
                   acc_ref, act_ref, p_ref, hqt_ref, *, gi):
    e = pl.program_id(2)
    nk = PEER_NKEYS
    tt = hq_ref.shape[1]
    half = tt // 2
    sqrt_half = np.float32(math.sqrt(0.5))

    def put_act(slot, r0, r1, c0, c1, val):
        for lb in range(c0 // LANE, c1 // LANE):
            act_ref[slot, lb, r0:r1, :] = val[:, lb * LANE - c0:(lb + 1) * LANE - c0]

    @pl.when(e == 0)
    def _():
        acc_ref[...] = jnp.zeros_like(acc_ref)
        hqt_ref[...] = hq_ref[0].astype(F32).T.astype(BF16)
        put_act(0, 0, PEER_EBLK, 0, tt, jnp.dot(u0_ref[...], hqt_ref[...], preferred_element_type=F32))

    def gate_tile(cur, a_list, t0):
        cols = slice(t0, t0 + LANE)
        lb = t0 // LANE
        e1 = [[e1_ref[0, 0, 0, h, a:a + 1, cols].astype(BF16) for h in range(PEER_HEADS)] for a in a_list]
        rank = [[rank_ref[0, 0, 0, h, a:a + 1, cols].astype(BF16) for h in range(PEER_HEADS)] for a in a_list]
        zero = jnp.zeros((PEER_RB, LANE), BF16)
        for r0 in range(0, nk, PEER_RB):
            rows = slice(r0, r0 + PEER_RB)
            g = [None] * len(a_list)
            for h in range(PEER_HEADS):
                level = level_ref[0, 0, h, lb, rows, :]
                e2 = e2_ref[0, 0, h, lb, rows, :]
                for ai in range(len(a_list)):
                    term = e1[ai][h] * jnp.where(rank[ai][h] < level, e2, zero)
                    g[ai] = term if g[ai] is None else g[ai] + term
            for ai, a in enumerate(a_list):
                x = act_ref[cur, lb, a * nk + r0:a * nk + r0 + PEER_RB, :]
                gelu = 0.5 * x * (1.0 + lax.erf(x * sqrt_half))
                p_ref[lb, a * nk + r0:a * nk + r0 + PEER_RB, :] = g[ai] * gelu.astype(BF16)

    def stages(cur, nxt):
        rows_per = 2 * nk
        act_rows = 2 * rows_per
        for c0 in (0, half):
            for r0 in range(0, PEER_EBLK, rows_per):
                if r0 % act_rows == 0:
                    put_act(nxt, r0, r0 + act_rows, c0, c0 + half, jnp.dot(
                        un_ref[r0:r0 + act_rows, :], hqt_ref[:, c0:c0 + half],
                        preferred_element_type=F32))
                a_list = list(range(r0 // nk, (r0 + rows_per) // nk))
                for t0 in range(c0, c0 + half, LANE):
                    gate_tile(cur, a_list, t0)
                p_blk = jnp.concatenate([p_ref[lb, r0:r0 + rows_per, :]
                                         for lb in range(c0 // LANE, (c0 + half) // LANE)], axis=1)
                out = jnp.dot(vt_ref[0, :, r0:r0 + rows_per], p_blk, preferred_element_type=F32)
                for j, lb in enumerate(range(c0 // LANE, (c0 + half) // LANE)):
                    acc_ref[lb] += out[:, j * LANE:(j + 1) * LANE]

    for parity in range(2):
        pl.when(e % 2 == parity)(functools.partial(stages, parity, 1 - parity))

    @pl.when(e == pl.num_programs(2) - 1)
    def _():
        acc = jnp.concatenate([acc_ref[lb] for lb in range(acc_ref.shape[0])], axis=1)
        o_ref[0] = xr_ref[0] + gate_ref[0][gi:gi + 1, :] * acc.T


def _expert_call(st, hq, u, vt, e1, rank, level, e2, x, modtab, gi):
    n_exp, d = u.shape
    nk = PEER_NKEYS
    rows = PEER_EBLK // nk
    tt = TOK_TILE
    n_blk = n_exp // PEER_EBLK
    return pl.pallas_call(
        functools.partial(_expert_kernel, gi=gi),
        grid=(st.batch, st.n_tok, n_blk),
        in_specs=[pl.BlockSpec((1, tt, d), lambda b, i, e: (b, i, 0)),
                  pl.BlockSpec((PEER_EBLK, d), lambda b, i, e: (0, 0)),
                  pl.BlockSpec((PEER_EBLK, d), lambda b, i, e: (jnp.minimum(e + 1, n_blk - 1), 0)),
                  pl.BlockSpec((1, d, PEER_EBLK), lambda b, i, e: (e, 0, 0)),
                  pl.BlockSpec((1, 1, 1, PEER_HEADS, rows, tt), lambda b, i, e: (b, i, e, 0, 0, 0)),
                  pl.BlockSpec((1, 1, 1, PEER_HEADS, rows, tt), lambda b, i, e: (b, i, e, 0, 0, 0)),
                  pl.BlockSpec((1, 1, PEER_HEADS, tt // LANE, nk, LANE), lambda b, i, e: (b, i, 0, 0, 0, 0)),
                  pl.BlockSpec((1, 1, PEER_HEADS, tt // LANE, nk, LANE), lambda b, i, e: (b, i, 0, 0, 0, 0)),
                  pl.BlockSpec((1, tt, d), lambda b, i, e: (b, i, 0)),
                  pl.BlockSpec((1, 6, d), lambda b, i, e: (st.mod_row(b, i), 0, 0))],
        out_specs=pl.BlockSpec((1, tt, d), lambda b, i, e: (b, i, 0)),
        out_shape=jax.ShapeDtypeStruct((st.batch, st.s_pad, d), F32),
        scratch_shapes=[pltpu.VMEM((tt // LANE, d, LANE), F32),
                        pltpu.VMEM((2, tt // LANE, PEER_EBLK, LANE), F32),
                        pltpu.VMEM((tt // LANE, PEER_EBLK, LANE), BF16),
                        pltpu.VMEM((d, tt), BF16)],
        compiler_params=_cparams(("parallel", "parallel", "arbitrary")),
        name="peer_experts",
    )(hq, u, u, vt, e1, rank, level, e2, x, modtab)


def _final_norm_kernel(x_ref, g_ref, o_ref):
    o_ref[0] = _rms(x_ref[0], g_ref[...])


def _final_norm(st, x, g):
    d = st.d
    return pl.pallas_call(
        _final_norm_kernel,
        grid=(st.batch, st.n_lat_tok),
        in_specs=[pl.BlockSpec((1, TOK_TILE, d), lambda b, i: (b, i, 0)),
                  pl.BlockSpec((1, d), lambda b, i: (0, 0))],
        out_specs=pl.BlockSpec((1, TOK_TILE, d), lambda b, i: (b, i, 0)),
        out_shape=jax.ShapeDtypeStruct((st.batch, st.s_lat, d), F32),
        compiler_params=_cparams(("parallel", "parallel")),
        name="final_norm",
    )(x, g.reshape(1, d))


def _rope_tables(st):
    rows = st.s_lat // GRID_W
    row = jnp.repeat(jnp.arange(rows, dtype=F32), GRID_W)
    col = jnp.tile(jnp.arange(GRID_W, dtype=F32), rows)
    half = DA_HEAD_DIM // 2
    freqs = ROPE_BASE ** (-jnp.arange(0, half, 2, dtype=F32) / half)
    ang_r = row[:, None] * freqs
    ang_c = col[:, None] * freqs
    ang = jnp.concatenate([ang_r, ang_r, ang_c, ang_c] * 2, axis=-1)
    extra = st.s_pad - st.s_lat
    cos = jnp.concatenate([jnp.cos(ang), jnp.ones((extra, LANE), F32)], axis=0)
    sin = jnp.concatenate([jnp.sin(ang), jnp.zeros((extra, LANE), F32)], axis=0)
    return cos, sin


def _qkv_weight(w):
    d = w.shape[0]
    qk = DA_HEADS * DA_HEAD_DIM * 2

    def regroup(t):
        return t.reshape(d, 2, DA_HEADS, DA_HEAD_DIM).transpose(0, 2, 1, 3).reshape(d, qk)

    return jnp.concatenate([regroup(w[:, :qk]), regroup(w[:, qk:2 * qk]), w[:, 2 * qk:]], axis=1).astype(BF16)


def _blocked_vt(v):
    n_exp, d = v.shape
    return v.reshape(n_exp // PEER_EBLK, PEER_EBLK, d).transpose(0, 2, 1).astype(BF16)


def _dt_weight(w_dt):
    d = w_dt.shape[0]
    t = w_dt.reshape(d, 2, SSD_HEADS)
    return jnp.pad(t, ((0, 0), (0, 0), (0, LANE - SSD_HEADS))).reshape(d, 2 * LANE).astype(BF16)


def _scan_params(dt_bias, a_log, d_skip):
    def lanes(t):
        return jnp.pad(t.reshape(2, SSD_HEADS), ((0, 0), (0, LANE - SSD_HEADS)))

    rows = jnp.stack([lanes(dt_bias), lanes(a_log)], axis=1)
    dlane = jnp.repeat(d_skip.reshape(2, SSD_HEADS), SSD_HEADDIM, axis=1)[:, None, :]
    return jnp.pad(rows, ((0, 0), (0, 6), (0, 0))).astype(F32), dlane.astype(F32)


def kernel(x, c, ctx, c_ctx, ada_w, ada_b, norm_mix_g, norm_ffn_g, ssd_in_w, ssd_conv_w, ssd_conv_b,
           ssd_dt_bias, ssd_a_log, ssd_d, ssd_norm_g, ssd_out_w, attn_qkv_w, attn_lambda_q, attn_lambda_k,
           attn_subln_g, attn_out_w, peer_q_w, peer_keys, peer_u, peer_v, final_norm_g):
    batch, s_lat, d = x.shape
    s_ctx = ctx.shape[1]
    depth = ada_w.shape[0]
    st = _Stream(batch, s_lat, s_ctx, d)

    xs = jnp.concatenate([x, ctx, jnp.zeros((batch, st.s_pad - st.s_real, d), F32)], axis=1)

    cc = jnp.zeros((8, d), F32).at[:batch].set(c).at[batch].set(c_ctx)
    mods = _ada_call(cc, ada_w, ada_b).reshape(depth, 8, 6, d)
    cos, sin = _rope_tables(st)
    zxw = SSD_D_INNER + SSD_CONV_DIM

    for i in range(depth):
        lat = mods[i, :batch]
        con = jnp.broadcast_to(mods[i, batch], (batch, 6, d))
        modtab = jnp.stack([con, lat], axis=1).reshape(batch * 2, 6, d)
        jm = i // 2
        if i % 2 == 0:
            w_in = ssd_in_w[jm]
            zx = _proj_mod(st, xs, norm_mix_g[i], modtab, w_in[:, :zxw].astype(BF16), 0, 1, 1024, BF16,
                           "ssd_in_proj")
            dtr = _proj_mod(st, xs, norm_mix_g[i], modtab, _dt_weight(w_in[:, zxw:]), 0, 1, LANE, F32,
                            "ssd_dt_proj")
            xc = _conv_call(st, zx, ssd_conv_w[jm], ssd_conv_b[jm])
            y2 = _scan_call(st, xc, dtr, *_scan_params(ssd_dt_bias[jm], ssd_a_log[jm], ssd_d[jm]))
            xs = _proj_gate_resid(st, y2, zx, ssd_norm_g[jm], ssd_out_w[jm].astype(BF16), xs, modtab, 2)
        else:
            lambda_init = 0.8 - 0.6 * math.exp(-0.3 * i)
            qkv = _proj_qkv(st, xs, norm_mix_g[i], modtab, _qkv_weight(attn_qkv_w[jm]), cos, sin, 0, 1)
            o = _attn_call(st, qkv, attn_lambda_q[jm], attn_lambda_k[jm], attn_subln_g[jm], lambda_init)
            xs = _proj_heads_resid(st, o, attn_out_w[jm].astype(BF16), xs, modtab, 2)
        e1, rank, e2, level, hq = _proj_scores(st, xs, norm_ffn_g[i], modtab, peer_q_w[i].astype(BF16),
                                               peer_keys[i], 3, 4)
        xs = _expert_call(st, hq, peer_u[i].astype(BF16), _blocked_vt(peer_v[i]), e1, rank, level, e2,
                          xs, modtab, 5)
    return _final_norm(st, xs, final_norm_g)
```

```python
import functools
import math

import numpy as np
import jax
import jax.numpy as jnp
from jax import lax
from jax.experimental import pallas as pl
from jax.experimental.pallas import tpu as pltpu

F32 = jnp.float32
BF16 = jnp.bfloat16

EPS = 1e-6
GRID_W = 64
ROPE_BASE = 10000.0

SSD_HEADDIM = 64
SSD_GROUPS = 4
SSD_HPG = 8
SSD_HEADS = SSD_GROUPS * SSD_HPG
SSD_STATE = 128
SSD_CONV_W = 5
SSD_CHUNK = 128
SSD_D_INNER = SSD_HEADS * SSD_HEADDIM
SSD_GN = SSD_GROUPS * SSD_STATE
SSD_CONV_DIM = SSD_D_INNER + 2 * SSD_GN

DA_HEADS = 8
DA_HEAD_DIM = 64

PEER_HEADS = 8
PEER_NKEYS = 128
PEER_TOPK = 16
PEER_EBLK = 1024
PEER_RB = 64

LANE = 128
TOK_TILE = 512
CONV_TILE = 256
ATT_TQ = 512
ATT_TK = 1024
ATT_TV = 512
ATT_SUM_ROWS = 16
VMEM_LIMIT = 56 * 1024 * 1024

NT_DIMS = (((1,), (1,)), ((), ()))


def _cparams(sem):
    return pltpu.CompilerParams(dimension_semantics=sem, vmem_limit_bytes=VMEM_LIMIT)


def _rms(xf, g):
    return xf * lax.rsqrt(jnp.mean(xf * xf, axis=-1, keepdims=True) + EPS) * g


def _silu(x):
    return x * (1.0 / (1.0 + jnp.exp(-x)))


def _ada_kernel(c_ref, w_ref, b_ref, o_ref):
    c = c_ref[...]
    o_ref[0] = jnp.dot(_silu(c), w_ref[0], preferred_element_type=F32,
                       precision=lax.Precision.HIGHEST) + b_ref[0]


def _ada_call(cc, ada_w, ada_b):
    depth, d, n = ada_w.shape
    tn = 1536
    return pl.pallas_call(
        _ada_kernel,
        grid=(depth, n // tn),
        in_specs=[pl.BlockSpec((8, d), lambda l, j: (0, 0)),
                  pl.BlockSpec((1, d, tn), lambda l, j: (l, 0, j)),
                  pl.BlockSpec((1, 1, tn), lambda l, j: (l, 0, j))],
        out_specs=pl.BlockSpec((1, 8, tn), lambda l, j: (l, 0, j)),
        out_shape=jax.ShapeDtypeStruct((depth, 8, n), F32),
        compiler_params=_cparams(("parallel", "parallel")),
        name="ada_mod",
    )(cc, ada_w, ada_b.reshape(depth, 1, n))


def _mod_prologue(x_ref, g_ref, mod_ref, hn_ref, sh, sc):
    m = mod_ref[0]
    hn = _rms(x_ref[0], g_ref[...]) * (1.0 + m[sc:sc + 1, :]) + m[sh:sh + 1, :]
    hn_ref[...] = hn.astype(BF16)


def _proj_mod_kernel(x_ref, g_ref, mod_ref, w_ref, o_ref, hn_ref, *, sh, sc):
    @pl.when(pl.program_id(2) == 0)
    def _():
        _mod_prologue(x_ref, g_ref, mod_ref, hn_ref, sh, sc)

    o_ref[0, 0] = jnp.dot(hn_ref[...], w_ref[...], preferred_element_type=F32).astype(o_ref.dtype)


def _rope_block(a, cos, sin, lane):
    fwd = pltpu.roll(a, LANE - 16, axis=1)
    bwd = pltpu.roll(a, 16, axis=1)
    rot = jnp.where(lane % 32 < 16, -fwd, bwd)
    return a * cos + rot * sin


def _proj_qkv_kernel(x_ref, g_ref, mod_ref, w_ref, cos_ref, sin_ref, o_ref, hn_ref, *, sh, sc):
    j = pl.program_id(2)

    @pl.when(j == 0)
    def _():
        _mod_prologue(x_ref, g_ref, mod_ref, hn_ref, sh, sc)

    acc = jnp.dot(hn_ref[...], w_ref[...], preferred_element_type=F32)
    tn = acc.shape[1]

    @pl.when(j < 2)
    def _():
        cos = cos_ref[...]
        sin = sin_ref[...]
        lane = lax.broadcasted_iota(jnp.int32, cos.shape, 1)
        scale = jnp.where(j == 0, DA_HEAD_DIM ** -0.5 * math.log2(math.e), 1.0).astype(F32)
        for cb in range(tn // LANE):
            a = acc[:, cb * LANE:(cb + 1) * LANE]
            o_ref[0, cb] = (_rope_block(a, cos, sin, lane) * scale).astype(o_ref.dtype)

    @pl.when(j >= 2)
    def _():
        for cb in range(tn // LANE):
            o_ref[0, cb] = acc[:, cb * LANE:(cb + 1) * LANE].astype(o_ref.dtype)


def _proj_gate_resid_kernel(yf_ref, yb_ref, z_ref, g_ref, w_ref, xr_ref, gate_ref, o_ref, hn_ref,
                            *, gi):
    @pl.when(pl.program_id(2) == 0)
    def _():
        z = jnp.concatenate([z_ref[c, 0] for c in range(z_ref.shape[0])], axis=1).astype(F32)
        u = (yf_ref[0, 0].astype(F32) + yb_ref[0, 0].astype(F32)) * _silu(z)
        hn_ref[...] = _rms(u, g_ref[...]).astype(BF16)

    acc = jnp.dot(hn_ref[...], w_ref[...], preferred_element_type=F32)
    o_ref[0] = xr_ref[0] + gate_ref[0][gi:gi + 1, :] * acc


def _proj_heads_resid_kernel(h_ref, w_ref, xr_ref, gate_ref, o_ref, *, gi):
    h = jnp.concatenate([h_ref[0, hh] for hh in range(h_ref.shape[1])], axis=1)
    acc = jnp.dot(h, w_ref[...], preferred_element_type=F32)
    o_ref[0] = xr_ref[0] + gate_ref[0][gi:gi + 1, :] * acc


def _proj_scores_kernel(x_ref, g_ref, mod_ref, w_ref, keys_ref, e1_ref, rank_ref, e2_ref, level_ref, hq_ref,
                        hn_ref, a_ref, b_ref, *, sh, sc):
    @pl.when(pl.program_id(2) == 0)
    def _():
        _mod_prologue(x_ref, g_ref, mod_ref, hn_ref, sh, sc)
        hq_ref[0] = hn_ref[...]

    q = jnp.dot(hn_ref[...], w_ref[...], preferred_element_type=F32)
    half = PEER_NKEYS
    s1, s2 = [lax.dot_general(keys_ref[0, c], q[:, c * half:(c + 1) * half], NT_DIMS,
                              preferred_element_type=F32, precision=lax.Precision.HIGHEST)
              for c in range(2)]
    _gate_body(s1, s2, e1_ref, rank_ref, e2_ref, level_ref, a_ref, b_ref)


class _Stream:
    def __init__(self, batch, s_lat, s_ctx, d):
        assert s_lat % TOK_TILE == 0 and s_ctx % CONV_TILE == 0 and s_lat % GRID_W == 0
        self.batch, self.s_lat, self.s_ctx, self.d = batch, s_lat, s_ctx, d
        self.s_real = s_lat + s_ctx
        self.s_pad = -(-self.s_real // TOK_TILE) * TOK_TILE
        self.n_tok = self.s_pad // TOK_TILE
        self.n_lat_tok = s_lat // TOK_TILE

    def mod_row(self, b, i):
        return b * 2 + jnp.where(i < self.n_lat_tok, 1, 0)


def _mod_specs(st, d):
    return [pl.BlockSpec((1, TOK_TILE, d), lambda b, i, j: (b, i, 0)),
            pl.BlockSpec((1, d), lambda b, i, j: (0, 0)),
            pl.BlockSpec((1, 6, d), lambda b, i, j: (st.mod_row(b, i), 0, 0))]


def _proj_mod(st, x, g, modtab, w, sh, sc, tn, out_dtype, name):
    d, n = w.shape
    return pl.pallas_call(
        functools.partial(_proj_mod_kernel, sh=sh, sc=sc),
        grid=(st.batch, st.n_tok, n // tn),
        in_specs=_mod_specs(st, d) + [pl.BlockSpec((d, tn), lambda b, i, j: (0, j))],
        out_specs=pl.BlockSpec((1, 1, TOK_TILE, tn), lambda b, i, j: (j, b, i, 0)),
        out_shape=jax.ShapeDtypeStruct((n // tn, st.batch, st.s_pad, tn), out_dtype),
        scratch_shapes=[pltpu.VMEM((TOK_TILE, d), BF16)],
        compiler_params=_cparams(("parallel", "parallel", "arbitrary")),
        name=name,
    )(x, g.reshape(1, d), modtab, w)


def _proj_qkv(st, x, g, modtab, w, cos, sin, sh, sc):
    d, n = w.shape
    tn = 1024
    return pl.pallas_call(
        functools.partial(_proj_qkv_kernel, sh=sh, sc=sc),
        grid=(st.batch, st.n_tok, n // tn),
        in_specs=_mod_specs(st, d) + [
            pl.BlockSpec((d, tn), lambda b, i, j: (0, j)),
            pl.BlockSpec((TOK_TILE, LANE), lambda b, i, j: (i, 0)),
            pl.BlockSpec((TOK_TILE, LANE), lambda b, i, j: (i, 0))],
        out_specs=pl.BlockSpec((1, tn // LANE, TOK_TILE, LANE), lambda b, i, j: (b, j, i, 0)),
        out_shape=jax.ShapeDtypeStruct((st.batch, n // LANE, st.s_pad, LANE), BF16),
        scratch_shapes=[pltpu.VMEM((TOK_TILE, d), BF16)],
        compiler_params=_cparams(("parallel", "parallel", "arbitrary")),
        name="attn_qkv_rope",
    )(x, g.reshape(1, d), modtab, w, cos, sin)


def _proj_gate_resid(st, y2, zx, g, w, x, modtab, gi):
    k, n = w.shape
    tn = n
    return pl.pallas_call(
        functools.partial(_proj_gate_resid_kernel, gi=gi),
        grid=(st.batch, st.n_tok, n // tn),
        in_specs=[pl.BlockSpec((1, 1, TOK_TILE, k), lambda b, i, j: (0, b, i, 0)),
                  pl.BlockSpec((1, 1, TOK_TILE, k), lambda b, i, j: (1, b, i, 0)),
                  pl.BlockSpec((k // zx.shape[3], 1, TOK_TILE, zx.shape[3]), lambda b, i, j: (0, b, i, 0)),
                  pl.BlockSpec((1, k), lambda b, i, j: (0, 0)),
                  pl.BlockSpec((k, tn), lambda b, i, j: (0, j)),
                  pl.BlockSpec((1, TOK_TILE, tn), lambda b, i, j: (b, i, j)),
                  pl.BlockSpec((1, 6, tn), lambda b, i, j: (st.mod_row(b, i), 0, j))],
        out_specs=pl.BlockSpec((1, TOK_TILE, tn), lambda b, i, j: (b, i, j)),
        out_shape=jax.ShapeDtypeStruct((st.batch, st.s_pad, n), F32),
        scratch_shapes=[pltpu.VMEM((TOK_TILE, k), BF16)],
        compiler_params=_cparams(("parallel", "parallel", "arbitrary")),
        name="ssd_out_proj",
    )(y2, y2, zx, g.reshape(1, k), w, x, modtab)


def _proj_heads_resid(st, h, w, x, modtab, gi):
    k, n = w.shape
    tn = n
    return pl.pallas_call(
        functools.partial(_proj_heads_resid_kernel, gi=gi),
        grid=(st.batch, st.n_tok, n // tn),
        in_specs=[pl.BlockSpec((1, k // LANE, TOK_TILE, LANE), lambda b, i, j: (b, 0, i, 0)),
                  pl.BlockSpec((k, tn), lambda b, i, j: (0, j)),
                  pl.BlockSpec((1, TOK_TILE, tn), lambda b, i, j: (b, i, j)),
                  pl.BlockSpec((1, 6, tn), lambda b, i, j: (st.mod_row(b, i), 0, j))],
        out_specs=pl.BlockSpec((1, TOK_TILE, tn), lambda b, i, j: (b, i, j)),
        out_shape=jax.ShapeDtypeStruct((st.batch, st.s_pad, n), F32),
        compiler_params=_cparams(("parallel", "parallel", "arbitrary")),
        name="attn_out_proj",
    )(h, w, x, modtab)


def _proj_scores(st, x, g, modtab, w, keys, sh, sc):
    d, n = w.shape
    nk = PEER_NKEYS
    tt = TOK_TILE
    rows = PEER_EBLK // nk
    n_blk = nk // rows
    n_lb = tt // LANE
    shp_blk = jax.ShapeDtypeStruct((st.batch, st.n_tok, n_blk, PEER_HEADS, rows, tt), F32)
    spec_blk = pl.BlockSpec((1, 1, n_blk, 1, rows, tt), lambda b, i, h: (b, i, 0, h, 0, 0))
    shp_e2 = jax.ShapeDtypeStruct((st.batch, st.n_tok, PEER_HEADS, n_lb, nk, LANE), BF16)
    spec_e2 = pl.BlockSpec((1, 1, 1, n_lb, nk, LANE), lambda b, i, h: (b, i, h, 0, 0, 0))
    return pl.pallas_call(
        functools.partial(_proj_scores_kernel, sh=sh, sc=sc),
        grid=(st.batch, st.n_tok, PEER_HEADS),
        in_specs=_mod_specs(st, d) + [
            pl.BlockSpec((d, 2 * nk), lambda b, i, j: (0, j)),
            pl.BlockSpec((1, 2, nk, nk), lambda b, i, j: (j, 0, 0, 0))],
        out_specs=[spec_blk, spec_blk, spec_e2, spec_e2,
                   pl.BlockSpec((1, tt, d), lambda b, i, j: (b, i, 0))],
        out_shape=[shp_blk, shp_blk, shp_e2, shp_e2,
                   jax.ShapeDtypeStruct((st.batch, st.s_pad, d), BF16)],
        scratch_shapes=[pltpu.VMEM((tt, d), BF16),
                        pltpu.VMEM((PEER_TOPK, tt), F32), pltpu.VMEM((PEER_TOPK, tt), F32)],
        compiler_params=_cparams(("parallel", "parallel", "arbitrary")),
        name="peer_scores_gates",
    )(x, g.reshape(1, d), modtab, w, keys)


def _conv_kernel(x_ref, prev_ref, next_ref, w_ref, b_ref, o_ref, buf_ref, *, seg_starts, n_tiles):
    i = pl.program_id(1)
    pad = SSD_CONV_W // 2
    first = functools.reduce(jnp.logical_or, [i == s for s in seg_starts])
    last = functools.reduce(jnp.logical_or, [i == s - 1 for s in seg_starts[1:] + (n_tiles,)])
    t = x_ref.shape[2]
    buf_ref[8:8 + t, :] = x_ref[0, 0].astype(F32)
    buf_ref[0:8, :] = jnp.where(first, 0.0, prev_ref[0, 0].astype(F32)[8:16, :])
    buf_ref[8 + t:16 + t, :] = jnp.where(last, 0.0, next_ref[0, 0].astype(F32)[0:8, :])
    w = w_ref[...]
    acc = b_ref[...] + w[0:1, :] * buf_ref[8 - pad:8 - pad + t, :]
    for k in range(1, SSD_CONV_W):
        acc = acc + w[k:k + 1, :] * buf_ref[8 - pad + k:8 - pad + k + t, :]
    o_ref[0, 0] = _silu(acc).astype(o_ref.dtype)


def _conv_call(st, zx, conv_w, conv_b):
    cw = 1024
    n_cblk = SSD_CONV_DIM // cw
    off = SSD_D_INNER // cw
    n_tiles = st.s_pad // CONV_TILE
    hb = CONV_TILE // 16
    n_hblk = st.s_pad // 16
    seg_starts = (0, st.s_lat // CONV_TILE, st.s_real // CONV_TILE)
    w8 = jnp.zeros((8, SSD_CONV_DIM), F32).at[:SSD_CONV_W].set(conv_w)
    return pl.pallas_call(
        functools.partial(_conv_kernel, seg_starts=seg_starts, n_tiles=n_tiles),
        grid=(st.batch, n_tiles, n_cblk),
        in_specs=[pl.BlockSpec((1, 1, CONV_TILE, cw), lambda b, i, c: (off + c, b, i, 0)),
                  pl.BlockSpec((1, 1, 16, cw), lambda b, i, c: (off + c, b, jnp.maximum(i * hb - 1, 0), 0)),
                  pl.BlockSpec((1, 1, 16, cw),
                               lambda b, i, c: (off + c, b, jnp.minimum((i + 1) * hb, n_hblk - 1), 0)),
                  pl.BlockSpec((8, cw), lambda b, i, c: (0, c)),
                  pl.BlockSpec((1, cw), lambda b, i, c: (0, c))],
        out_specs=pl.BlockSpec((1, 1, CONV_TILE, cw), lambda b, i, c: (c, b, i, 0)),
        out_shape=jax.ShapeDtypeStruct((n_cblk, st.batch, st.s_pad, cw), BF16),
        scratch_shapes=[pltpu.VMEM((CONV_TILE + 16, cw), F32)],
        compiler_params=_cparams(("parallel", "parallel", "parallel")),
        name="ssd_conv",
    )(zx, zx, zx, w8, conv_b.reshape(1, SSD_CONV_DIM))


def _scan_kernel(xs_ref, bc_ref, dt_ref, tri_ref, par_ref, dlane_ref, y_ref, h_ref):
    @pl.when(pl.program_id(2) == 0)
    def _():
        h_ref[...] = jnp.zeros_like(h_ref)

    par = par_ref[0]
    raw = dt_ref[0, 0] + par[0:1, :]
    dt = jnp.maximum(raw, 0.0) + jnp.log1p(jnp.exp(-jnp.abs(raw)))
    a = -jnp.exp(par[1:2, :])
    dta = dt * a
    tri = tri_ref[0]
    cs = jnp.dot(tri, dta, preferred_element_type=F32, precision=lax.Precision.HIGHEST)
    tot = jnp.sum(dta, axis=0, keepdims=True)
    e_tot = jnp.exp(tot)
    w_end = jnp.exp(tot - cs) * dt
    cs_t = cs.T
    dt_t = dt.T
    w_end_t = w_end.T
    mask = tri > 0.5

    e_cs_t = jnp.exp(cs_t)
    n = SSD_STATE
    q = tri.shape[0]
    lane = lax.broadcasted_iota(jnp.int32, (q, LANE), 1)
    left = lane < SSD_HEADDIM
    eye = lax.broadcasted_iota(jnp.int32, (q, q), 0) == lax.broadcasted_iota(jnp.int32, (q, q), 1)
    lane1 = lax.broadcasted_iota(jnp.int32, (1, LANE), 1) < SSD_HEADDIM
    for g in range(SSD_GROUPS):
        bm = bc_ref[0, 0, :, g * n:(g + 1) * n]
        cm = bc_ref[0, 0, :, SSD_GN + g * n:SSD_GN + (g + 1) * n]
        bm_t = bm.astype(F32).T
        cb = lax.dot_general(cm, bm, NT_DIMS, preferred_element_type=F32)
        for r in range(0, SSD_HPG, 2):
            hds = (g * SSD_HPG + r, g * SSD_HPG + r + 1)
            pair = hds[0] // 2
            xb, xo = divmod(pair * LANE, xs_ref.shape[3])
            xp = xs_ref[xb, 0, :, xo:xo + LANE]
            zero = jnp.zeros_like(xp)
            x_blk = jnp.concatenate([jnp.where(left, xp, zero), jnp.where(left, zero, xp)], axis=0)
            h_prev = h_ref[pair]
            ch = jnp.dot(cm, h_prev.astype(BF16), preferred_element_type=F32)
            ch_blk = jnp.concatenate([jnp.where(left, ch, 0.0), jnp.where(left, 0.0, ch)], axis=0)
            w_parts, d_parts, u_parts = [], [], []
            for hd in hds:
                seg = cs[:, hd:hd + 1] - cs_t[hd:hd + 1, :]
                decay = jnp.where(mask, jnp.exp(seg), 0.0)
                w_parts.append(cb * decay * dt_t[hd:hd + 1, :])
                d_parts.append(jnp.where(eye, e_cs_t[hd:hd + 1, :], 0.0))
                u_parts.append(bm_t * w_end_t[hd:hd + 1, :])
            lhs = jnp.concatenate(w_parts + d_parts, axis=1).astype(BF16)
            rhs = jnp.concatenate([x_blk, ch_blk.astype(BF16)], axis=0)
            y = jnp.dot(lhs, rhs, preferred_element_type=F32)
            y = y + dlane_ref[0, :, pair * LANE:(pair + 1) * LANE] * xp.astype(F32)
            y_ref[0, 0, :, pair * LANE:(pair + 1) * LANE] = y.astype(y_ref.dtype)
            upd = jnp.dot(jnp.concatenate(u_parts, axis=1).astype(BF16), x_blk, preferred_element_type=F32)
            e_tot_pair = jnp.where(lane1, e_tot[:, hds[0]:hds[0] + 1], e_tot[:, hds[1]:hds[1] + 1])
            h_ref[pair] = e_tot_pair * h_prev + upd


def _scan_call(st, xc, dtr, par, dlane):
    q = SSD_CHUNK
    n_lat, n_ctx = st.s_lat // q, st.s_ctx // q
    n_real = n_lat + n_ctx
    n_all = st.s_pad // q
    cw = xc.shape[3]
    n_xblk = SSD_D_INNER // cw

    def chunk(d, s):
        fwd = jnp.where(s < n_ctx, n_lat + s, s - n_ctx)
        bwd = n_real - 1 - s
        return jnp.where(s < n_real, jnp.where(d == 0, fwd, bwd), s)

    idx = np.arange(q)
    tri = jnp.asarray(np.stack([idx[:, None] >= idx[None, :], idx[:, None] <= idx[None, :]]), F32)
    return pl.pallas_call(
        _scan_kernel,
        grid=(st.batch, 2, n_all),
        in_specs=[pl.BlockSpec((n_xblk, 1, q, cw), lambda b, d, s: (0, b, chunk(d, s), 0)),
                  pl.BlockSpec((1, 1, q, cw), lambda b, d, s: (n_xblk, b, chunk(d, s), 0)),
                  pl.BlockSpec((1, 1, q, LANE), lambda b, d, s: (d, b, chunk(d, s), 0)),
                  pl.BlockSpec((1, q, q), lambda b, d, s: (d, 0, 0)),
                  pl.BlockSpec((1, 8, LANE), lambda b, d, s: (d, 0, 0)),
                  pl.BlockSpec((1, 1, SSD_D_INNER), lambda b, d, s: (d, 0, 0))],
        out_specs=pl.BlockSpec((1, 1, q, SSD_D_INNER), lambda b, d, s: (d, b, chunk(d, s), 0)),
        out_shape=jax.ShapeDtypeStruct((2, st.batch, st.s_pad, SSD_D_INNER), BF16),
        scratch_shapes=[pltpu.VMEM((SSD_HEADS // 2, SSD_STATE, 2 * SSD_HEADDIM), F32)],
        compiler_params=_cparams(("parallel", "parallel", "arbitrary")),
        name="ssd_scan",
    )(xc, xc, dtr, tri, par, dlane)


def _attn_kernel(q_ref, k_ref, v_ref, lq_ref, lk_ref, g_ref, o_ref, qs_ref, vt_ref, m_ref, acc_ref,
                 s_ref, *, s_lat, s_ctx, lambda_init):
    qi = pl.program_id(2)
    tq = q_ref.shape[2]
    tk = ATT_TK
    tv = vt_ref.shape[2]
    n_sub = tk // tv
    hw = q_ref.shape[3]

    @pl.when(qi == 0)
    def _():
        def tr(c, carry):
            rows = v_ref[0, 0, pl.ds(pl.multiple_of(c * tv, tv), tv), :]
            vt_ref[c, 0:hw, :] = rows.astype(F32).T.astype(BF16)
            sub = lax.broadcasted_iota(jnp.int32, (vt_ref.shape[1] - hw, tv), 0)
            vt_ref[c, hw:, :] = jnp.where(sub == 0, 1.0, 0.0).astype(BF16)
            return carry

        lax.fori_loop(0, vt_ref.shape[0], tr, 0)

    qt = q_ref[0, 0].astype(F32).T
    row = lax.broadcasted_iota(jnp.int32, qt.shape, 0)
    qs_ref[:, 0:tq] = jnp.where(row < DA_HEAD_DIM, qt, 0.0).astype(BF16)
    qs_ref[:, tq:2 * tq] = jnp.where(row >= DA_HEAD_DIM, qt, 0.0).astype(BF16)
    m_ref[...] = jnp.full_like(m_ref, -jnp.inf)
    acc_ref[...] = jnp.zeros_like(acc_ref)

    def scores(k):
        return jnp.dot(k, qs_ref[...], preferred_element_type=F32)

    def update(read_s, vts):
        m_prev = m_ref[...]
        m_new = jnp.maximum(m_prev, jnp.max(read_s(), axis=0, keepdims=True))
        alpha = jnp.exp2(m_prev - m_new)
        pb = jnp.exp2(read_s() - m_new).astype(BF16)
        rows = pb.shape[0] // len(vts)
        pv = None
        for i, vt in enumerate(vts):
            d = jnp.dot(vt, pb[i * rows:(i + 1) * rows, :], preferred_element_type=F32)
            pv = d if pv is None else pv + d
        acc_ref[...] = alpha * acc_ref[...] + pv
        m_ref[...] = m_new

    n_lat = s_lat // tk

    def k_chunk(c):
        return k_ref[0, 0, pl.ds(pl.multiple_of(c * tk, tk), tk), :]

    def context_chunk():
        s_ctx_keys = scores(k_ref[0, 0, s_lat:s_lat + s_ctx, :])
        update(lambda: s_ctx_keys, [vt_ref[s_lat // tv, :, 0:s_ctx]])

    @pl.when(qi < s_lat // tq)
    def _():
        s_ref[0] = scores(k_chunk(0))
        context_chunk()

        def half_step(c, slot, prefetch=True):
            if prefetch:
                s_ref[1 - slot] = scores(k_chunk(c + 1))
            update(lambda: s_ref[slot], [vt_ref[c * n_sub + i] for i in range(n_sub)])

        def body(cp, carry):
            half_step(2 * cp, 0)
            half_step(2 * cp + 1, 1)
            return carry

        n_pairs = (n_lat - 1) // 2
        lax.fori_loop(0, n_pairs, body, 0)
        for c in range(2 * n_pairs, n_lat):
            half_step(c, c % 2, prefetch=c + 1 < n_lat)

    @pl.when(qi >= s_lat // tq)
    def _():
        context_chunk()

    lam = (jnp.exp(jnp.sum(lq_ref[0:1, :] * lk_ref[0:1, :], axis=-1, keepdims=True))
           - jnp.exp(jnp.sum(lq_ref[1:2, :] * lk_ref[1:2, :], axis=-1, keepdims=True)) + lambda_init)
    o_all = acc_ref[0:hw, :] / acc_ref[hw:hw + 1, :]
    o = (o_all[:, 0:tq] - lam * o_all[:, tq:2 * tq]).T
    o_ref[0, 0] = (_rms(o, g_ref[...]) * (1.0 - lambda_init)).astype(o_ref.dtype)


def _attn_call(st, qkv, lam_q, lam_k, subln_g, lambda_init):
    tq = ATT_TQ
    nq = st.s_pad // tq
    hw = 2 * DA_HEAD_DIM
    nh = DA_HEADS
    return pl.pallas_call(
        functools.partial(_attn_kernel, s_lat=st.s_lat, s_ctx=st.s_ctx, lambda_init=lambda_init),
        grid=(st.batch, nh, nq),
        in_specs=[pl.BlockSpec((1, 1, tq, hw), lambda b, h, i: (b, h, i, 0)),
                  pl.BlockSpec((1, 1, st.s_pad, hw), lambda b, h, i: (b, nh + h, 0, 0)),
                  pl.BlockSpec((1, 1, st.s_pad, hw), lambda b, h, i: (b, 2 * nh + h, 0, 0)),
                  pl.BlockSpec((2, DA_HEAD_DIM), lambda b, h, i: (0, 0)),
                  pl.BlockSpec((2, DA_HEAD_DIM), lambda b, h, i: (0, 0)),
                  pl.BlockSpec((1, hw), lambda b, h, i: (0, 0))],
        out_specs=pl.BlockSpec((1, 1, tq, hw), lambda b, h, i: (b, h, i, 0)),
        out_shape=jax.ShapeDtypeStruct((st.batch, nh, st.s_pad, hw), BF16),
        scratch_shapes=[pltpu.VMEM((hw, 2 * tq), BF16),
                        pltpu.VMEM((st.s_pad // ATT_TV, hw + ATT_SUM_ROWS, ATT_TV), BF16),
                        pltpu.VMEM((1, 2 * tq), F32),
                        pltpu.VMEM((hw + ATT_SUM_ROWS, 2 * tq), F32),
                        pltpu.VMEM((2, ATT_TK, 2 * tq), F32)],
        compiler_params=_cparams(("parallel", "parallel", "arbitrary")),
        name="diff_attn",
    )(qkv, qkv, qkv, lam_q, lam_k, subln_g.reshape(1, hw))


def _extract_top(s, dst_ref, n):
    w = s
    rank = jnp.full_like(s, float(n))
    for k in range(n):
        m = jnp.max(w, axis=0, keepdims=True)
        dst_ref[k:k + 1, :] = m
        hit = w == m
        rank = jnp.where(hit, float(k), rank)
        if k + 1 < n:
            w = jnp.where(hit, -jnp.inf, w)
    return rank


def _gate_body(s1, s2, e1_ref, rank_ref, e2_ref, level_ref, a_ref, b_ref):
    k = PEER_TOPK
    n_lb = e2_ref.shape[3]
    rank = _extract_top(s1, a_ref, k)
    _extract_top(s2, b_ref, k)
    a = a_ref[...]
    b = b_ref[...]
    cand = jnp.concatenate(
        [a[0:1, :] + b] + [a[i:i + 1, :] + b[0:8, :] for i in range(1, 8)] + [a[8:16, :] + b[0:1, :]],
        axis=0)
    w = cand
    tau = None
    for it in range(k):
        tau = jnp.max(w, axis=0, keepdims=True)
        if it + 1 < k:
            w = jnp.where(w == tau, -jnp.inf, w)
    top = a[0:1, :] + b[0:1, :]
    z = jnp.sum(jnp.where(cand >= tau, jnp.exp(cand - top), 0.0), axis=0, keepdims=True)
    level = jnp.zeros_like(s2)
    for j in range(k):
        level = jnp.where(s2 >= tau - a[j:j + 1, :], float(j + 1), level)
    blk_shape = e1_ref.shape[2:3] + e1_ref.shape[4:]
    e1_ref[0, 0, :, 0] = jnp.exp(s1 - a[0:1, :]).reshape(blk_shape)
    rank_ref[0, 0, :, 0] = rank.reshape(blk_shape)
    e2 = (jnp.exp(s2 - b[0:1, :]) / z).astype(BF16)
    level = level.astype(BF16)
    for lb in range(n_lb):
        e2_ref[0, 0, 0, lb] = e2[:, lb * LANE:(lb + 1) * LANE]
        level_ref[0, 0, 0, lb] = level[:, lb * LANE:(lb + 1) * LANE]


def _expert_kernel(hq_ref, u0_ref, un_ref, vt_ref, e1_ref, rank_ref, level_ref, e2_ref, xr_ref, gate_ref, o_ref,
                   acc_ref, act_ref, p_ref, hqt_ref, *, gi):
    e = pl.program_id(2)
    nk = PEER_NKEYS
    tt = hq_ref.shape[1]
    half = tt // 2
    sqrt_half = np.float32(math.sqrt(0.5))

    def put_act(slot, r0, r1, c0, c1, val):
        for lb in range(c0 // LANE, c1 // LANE):
            act_ref[slot, lb, r0:r1, :] = val[:, lb * LANE - c0:(lb + 1) * LANE - c0]

    @pl.when(e == 0)
    def _():
        acc_ref[...] = jnp.zeros_like(acc_ref)
        hqt_ref[...] = hq_ref[0].astype(F32).T.astype(BF16)
        put_act(0, 0, PEER_EBLK, 0, tt, jnp.dot(u0_ref[...], hqt_ref[...], preferred_element_type=F32))

    def gate_tile(cur, a_list, t0):
        cols = slice(t0, t0 + LANE)
        lb = t0 // LANE
        e1 = [[e1_ref[0, 0, 0, h, a:a + 1, cols].astype(BF16) for h in range(PEER_HEADS)] for a in a_list]
        rank = [[rank_ref[0, 0, 0, h, a:a + 1, cols].astype(BF16) for h in range(PEER_HEADS)] for a in a_list]
        zero = jnp.zeros((PEER_RB, LANE), BF16)
        for r0 in range(0, nk, PEER_RB):
            rows = slice(r0, r0 + PEER_RB)
            g = [None] * len(a_list)
            for h in range(PEER_HEADS):
                level = level_ref[0, 0, h, lb, rows, :]
                e2 = e2_ref[0, 0, h, lb, rows, :]
                for ai in range(len(a_list)):
                    term = e1[ai][h] * jnp.where(rank[ai][h] < level, e2, zero)
                    g[ai] = term if g[ai] is None else g[ai] + term
            for ai, a in enumerate(a_list):
                x = act_ref[cur, lb, a * nk + r0:a * nk + r0 + PEER_RB, :]
                gelu = 0.5 * x * (1.0 + lax.erf(x * sqrt_half))
                p_ref[lb, a * nk + r0:a * nk + r0 + PEER_RB, :] = g[ai] * gelu.astype(BF16)

    def stages(cur, nxt):
        rows_per = 2 * nk
        act_rows = PEER_EBLK
        for c0 in (0, half):
            for r0 in range(0, PEER_EBLK, rows_per):
                if r0 % act_rows == 0:
                    put_act(nxt, r0, r0 + act_rows, c0, c0 + half, jnp.dot(
                        un_ref[r0:r0 + act_rows, :], hqt_ref[:, c0:c0 + half],
                        preferred_element_type=F32))
                a_list = list(range(r0 // nk, (r0 + rows_per) // nk))
                for t0 in range(c0, c0 + half, LANE):
                    gate_tile(cur, a_list, t0)
                p_blk = jnp.concatenate([p_ref[lb, r0:r0 + rows_per, :]
                                         for lb in range(c0 // LANE, (c0 + half) // LANE)], axis=1)
                out = jnp.dot(vt_ref[0, :, r0:r0 + rows_per], p_blk, preferred_element_type=F32)
                for j, lb in enumerate(range(c0 // LANE, (c0 + half) // LANE)):
                    acc_ref[lb] += out[:, j * LANE:(j + 1) * LANE]

    for parity in range(2):
        pl.when(e % 2 == parity)(functools.partial(stages, parity, 1 - parity))

    @pl.when(e == pl.num_programs(2) - 1)
    def _():
        acc = jnp.concatenate([acc_ref[lb] for lb in range(acc_ref.shape[0])], axis=1)
        o_ref[0] = xr_ref[0] + gate_ref[0][gi:gi + 1, :] * acc.T


def _expert_call(st, hq, u, vt, e1, rank, level, e2, x, modtab, gi):
    n_exp, d = u.shape
    nk = PEER_NKEYS
    rows = PEER_EBLK // nk
    tt = TOK_TILE
    n_blk = n_exp // PEER_EBLK
    return pl.pallas_call(
        functools.partial(_expert_kernel, gi=gi),
        grid=(st.batch, st.n_tok, n_blk),
        in_specs=[pl.BlockSpec((1, tt, d), lambda b, i, e: (b, i, 0)),
                  pl.BlockSpec((PEER_EBLK, d), lambda b, i, e: (0, 0)),
                  pl.BlockSpec((PEER_EBLK, d), lambda b, i, e: (jnp.minimum(e + 1, n_blk - 1), 0)),
                  pl.BlockSpec((1, d, PEER_EBLK), lambda b, i, e: (e, 0, 0)),
                  pl.BlockSpec((1, 1, 1, PEER_HEADS, rows, tt), lambda b, i, e: (b, i, e, 0, 0, 0)),
                  pl.BlockSpec((1, 1, 1, PEER_HEADS, rows, tt), lambda b, i, e: (b, i, e, 0, 0, 0)),
                  pl.BlockSpec((1, 1, PEER_HEADS, tt // LANE, nk, LANE), lambda b, i, e: (b, i, 0, 0, 0, 0)),
                  pl.BlockSpec((1, 1, PEER_HEADS, tt // LANE, nk, LANE), lambda b, i, e: (b, i, 0, 0, 0, 0)),
                  pl.BlockSpec((1, tt, d), lambda b, i, e: (b, i, 0)),
                  pl.BlockSpec((1, 6, d), lambda b, i, e: (st.mod_row(b, i), 0, 0))],
        out_specs=pl.BlockSpec((1, tt, d), lambda b, i, e: (b, i, 0)),
        out_shape=jax.ShapeDtypeStruct((st.batch, st.s_pad, d), F32),
        scratch_shapes=[pltpu.VMEM((tt // LANE, d, LANE), F32),
                        pltpu.VMEM((2, tt // LANE, PEER_EBLK, LANE), F32),
                        pltpu.VMEM((tt // LANE, PEER_EBLK, LANE), BF16),
                        pltpu.VMEM((d, tt), BF16)],
        compiler_params=_cparams(("parallel", "parallel", "arbitrary")),
        name="peer_experts",
    )(hq, u, u, vt, e1, rank, level, e2, x, modtab)


def _final_norm_kernel(x_ref, g_ref, o_ref):
    o_ref[0] = _rms(x_ref[0], g_ref[...])


def _final_norm(st, x, g):
    d = st.d
    return pl.pallas_call(
        _final_norm_kernel,
        grid=(st.batch, st.n_lat_tok),
        in_specs=[pl.BlockSpec((1, TOK_TILE, d), lambda b, i: (b, i, 0)),
                  pl.BlockSpec((1, d), lambda b, i: (0, 0))],
        out_specs=pl.BlockSpec((1, TOK_TILE, d), lambda b, i: (b, i, 0)),
        out_shape=jax.ShapeDtypeStruct((st.batch, st.s_lat, d), F32),
        compiler_params=_cparams(("parallel", "parallel")),
        name="final_norm",
    )(x, g.reshape(1, d))


def _rope_tables(st):
    rows = st.s_lat // GRID_W
    row = jnp.repeat(jnp.arange(rows, dtype=F32), GRID_W)
    col = jnp.tile(jnp.arange(GRID_W, dtype=F32), rows)
    half = DA_HEAD_DIM // 2
    freqs = ROPE_BASE ** (-jnp.arange(0, half, 2, dtype=F32) / half)
    ang_r = row[:, None] * freqs
    ang_c = col[:, None] * freqs
    ang = jnp.concatenate([ang_r, ang_r, ang_c, ang_c] * 2, axis=-1)
    extra = st.s_pad - st.s_lat
    cos = jnp.concatenate([jnp.cos(ang), jnp.ones((extra, LANE), F32)], axis=0)
    sin = jnp.concatenate([jnp.sin(ang), jnp.zeros((extra, LANE), F32)], axis=0)
    return cos, sin


def _qkv_weight(w):
    d = w.shape[0]
    qk = DA_HEADS * DA_HEAD_DIM * 2

    def regroup(t):
        return t.reshape(d, 2, DA_HEADS, DA_HEAD_DIM).transpose(0, 2, 1, 3).reshape(d, qk)

    return jnp.concatenate([regroup(w[:, :qk]), regroup(w[:, qk:2 * qk]), w[:, 2 * qk:]], axis=1).astype(BF16)


def _blocked_vt(v):
    n_exp, d = v.shape
    return v.reshape(n_exp // PEER_EBLK, PEER_EBLK, d).transpose(0, 2, 1).astype(BF16)


def _dt_weight(w_dt):
    d = w_dt.shape[0]
    t = w_dt.reshape(d, 2, SSD_HEADS)
    return jnp.pad(t, ((0, 0), (0, 0), (0, LANE - SSD_HEADS))).reshape(d, 2 * LANE).astype(BF16)


def _scan_params(dt_bias, a_log, d_skip):
    def lanes(t):
        return jnp.pad(t.reshape(2, SSD_HEADS), ((0, 0), (0, LANE - SSD_HEADS)))

    rows = jnp.stack([lanes(dt_bias), lanes(a_log)], axis=1)
    dlane = jnp.repeat(d_skip.reshape(2, SSD_HEADS), SSD_HEADDIM, axis=1)[:, None, :]
    return jnp.pad(rows, ((0, 0), (0, 6), (0, 0))).astype(F32), dlane.astype(F32)


def kernel(x, c, ctx, c_ctx, ada_w, ada_b, norm_mix_g, norm_ffn_g, ssd_in_w, ssd_conv_w, ssd_conv_b,
           ssd_dt_bias, ssd_a_log, ssd_d, ssd_norm_g, ssd_out_w, attn_qkv_w, attn_lambda_q, attn_lambda_k,
           attn_subln_g, attn_out_w, peer_q_w, peer_keys, peer_u, peer_v, final_norm_g):
    batch, s_lat, d = x.shape
    s_ctx = ctx.shape[1]
    depth = ada_w.shape[0]
    st = _Stream(batch, s_lat, s_ctx, d)

    xs = jnp.concatenate([x, ctx, jnp.zeros((batch, st.s_pad - st.s_real, d), F32)], axis=1)

    cc = jnp.zeros((8, d), F32).at[:batch].set(c).at[batch].set(c_ctx)
    mods = _ada_call(cc, ada_w, ada_b).reshape(depth, 8, 6, d)
    cos, sin = _rope_tables(st)
    zxw = SSD_D_INNER + SSD_CONV_DIM

    for i in range(depth):
        lat = mods[i, :batch]
        con = jnp.broadcast_to(mods[i, batch], (batch, 6, d))
        modtab = jnp.stack([con, lat], axis=1).reshape(batch * 2, 6, d)
        jm = i // 2
        if i % 2 == 0:
            w_in = ssd_in_w[jm]
            zx = _proj_mod(st, xs, norm_mix_g[i], modtab, w_in[:, :zxw].astype(BF16), 0, 1, 1024, BF16,
                           "ssd_in_proj")
            dtr = _proj_mod(st, xs, norm_mix_g[i], modtab, _dt_weight(w_in[:, zxw:]), 0, 1, LANE, F32,
                            "ssd_dt_proj")
            xc = _conv_call(st, zx, ssd_conv_w[jm], ssd_conv_b[jm])
            y2 = _scan_call(st, xc, dtr, *_scan_params(ssd_dt_bias[jm], ssd_a_log[jm], ssd_d[jm]))
            xs = _proj_gate_resid(st, y2, zx, ssd_norm_g[jm], ssd_out_w[jm].astype(BF16), xs, modtab, 2)
        else:
            lambda_init = 0.8 - 0.6 * math.exp(-0.3 * i)
            qkv = _proj_qkv(st, xs, norm_mix_g[i], modtab, _qkv_weight(attn_qkv_w[jm]), cos, sin, 0, 1)
            o = _attn_call(st, qkv, attn_lambda_q[jm], attn_lambda_k[jm], attn_subln_g[jm], lambda_init)
            xs = _proj_heads_resid(st, o, attn_out_w[jm].astype(BF16), xs, modtab, 2)
        e1, rank, e2, level, hq = _proj_scores(st, xs, norm_ffn_g[i], modtab, peer_q_w[i].astype(BF16),
                                               peer_keys[i], 3, 4)
        xs = _expert_call(st, hq, peer_u[i].astype(BF16), _blocked_vt(peer_v[i]), e1, rank, level, e2,
                          xs, modtab, 5)
    return _final_norm(st, xs, final_norm_g)
```
